```python
import math
import jax
import jax.numpy as jnp
from jax import lax
import numpy as np

D_MODEL = 1024
BATCH = 2
SEQ = 8192
DEPTH = 4
DEC_BATCH = 32
DEC_SEQ = 4
PAST_LEN = 8192
PAGE_SIZE = 128

DN_HEADS = 6
DN_DK = 64
DN_DV = 64
DN_W = DN_HEADS * DN_DV
DN_CONV_CH = 2 * DN_HEADS * DN_DK + DN_HEADS * DN_DV
CONV_K = 4
DN_CHUNK = 64
SSM_GROUPS = 16
SSM_GROUP_CH = 16
SSM_W = SSM_GROUPS * SSM_GROUP_CH
SSM_P = 64
DA_HEADS = 6
DA_HD = 32
DA_VD = 2 * DA_HD
DA_W = DA_HEADS * DA_VD
DA_QK_W = DA_HEADS * 2 * DA_HD
MIX_W = DN_W + SSM_W + DA_W
ROPE_THETA = 10000.0
NORM_EPS = 1e-6
Q_BLOCK = 128
SPLITS = (DN_CONV_CH, DN_HEADS, DN_HEADS, DN_W, SSM_W, SSM_W, DA_QK_W, DA_QK_W, DA_W, DA_W)
IN_W = sum(SPLITS)

kernel_name = 'hymba_delta_s5_diff_decoder'


def split_cols(a, sizes):
    out, s = [], 0
    for n in sizes:
        out.append(a[..., s:s + n])
        s += n
    return out


def rmsnorm(x, g):
    xf = x.astype(jnp.float32)
    return xf * lax.rsqrt(jnp.mean(xf * xf, axis=-1, keepdims=True) + NORM_EPS) * g.astype(jnp.float32)


def l2norm(x):
    xf = x.astype(jnp.float32)
    return xf * lax.rsqrt(jnp.sum(xf * xf, axis=-1, keepdims=True) + NORM_EPS)


def rope(x, pos):
    d = x.shape[-1]
    half = d // 2
    inv = jnp.power(ROPE_THETA, -jnp.arange(half, dtype=jnp.float32) * 2.0 / d)
    ang = pos[:, None] * inv[None, :]
    shape = (1, x.shape[1]) + (1,) * (x.ndim - 3) + (half,)
    cos = jnp.cos(ang).reshape(shape)
    sin = jnp.sin(ang).reshape(shape)
    x1, x2 = x[..., :half], x[..., half:]
    return jnp.concatenate([x1 * cos - x2 * sin, x2 * cos + x1 * sin], axis=-1)


def short_conv(x, buf, w):
    T = x.shape[1]
    xp = jnp.concatenate([buf.astype(jnp.float32), x.astype(jnp.float32)], axis=1)
    y = sum(xp[:, i:i + T] * w[i].astype(jnp.float32) for i in range(CONV_K))
    return jax.nn.silu(y), xp[:, xp.shape[1] - (CONV_K - 1):]


def gated_delta(q, k, v, beta, g, s0):
    f32 = jnp.float32
    B, T, H, dk = q.shape
    dv = v.shape[-1]
    L = DN_CHUNK if T % DN_CHUNK == 0 else T
    N = T // L

    def chunks(a):
        a = a.astype(f32).reshape((B, N, L) + a.shape[2:])
        return jnp.moveaxis(a, 2, 3)

    q = chunks(q) * (dk ** -0.5)
    k = chunks(k)
    v = chunks(v)
    beta = chunks(beta)
    g_cum = jnp.cumsum(chunks(g), axis=-1)
    idx = jnp.arange(L)
    causal = idx[:, None] >= idx[None, :]
    strict = idx[:, None] > idx[None, :]
    decay = jnp.exp(jnp.where(causal, g_cum[..., :, None] - g_cum[..., None, :], -jnp.inf))
    k_beta = k * beta[..., None]
    m = jnp.where(strict, jnp.einsum('bnhid,bnhjd->bnhij', k_beta, k) * decay, 0.0)
    rhs = jnp.concatenate([v * beta[..., None], k_beta * jnp.exp(g_cum)[..., None]], axis=-1)
    sol = lax.linalg.triangular_solve(m + jnp.eye(L, dtype=f32), rhs, left_side=True,
                                      lower=True, unit_diagonal=True)
    u, w = sol[..., :dv], sol[..., dv:]
    attn = jnp.einsum('bnhid,bnhjd->bnhij', q, k) * decay
    q_dec = q * jnp.exp(g_cum)[..., None]
    k_dec = k * jnp.exp(g_cum[..., -1:] - g_cum)[..., None]
    g_last = jnp.exp(g_cum[..., -1])

    def step(S, inp):
        qd, kd, uc, wc, ac, gl = inp
        v_new = uc - jnp.einsum('bhlk,bhkv->bhlv', wc, S)
        o = jnp.einsum('bhlk,bhkv->bhlv', qd, S) + jnp.einsum('bhij,bhjv->bhiv', ac, v_new)
        S = S * gl[..., None, None] + jnp.einsum('bhlk,bhlv->bhkv', kd, v_new)
        return S, o

    xs = tuple(jnp.moveaxis(a, 1, 0) for a in (q_dec, k_dec, u, w, attn, g_last))
    s_fin, o = lax.scan(step, s0.astype(f32), xs)
    o = jnp.moveaxis(jnp.moveaxis(o, 0, 1), 3, 2).reshape(B, T, H, dv)
    return o, s_fin


def ssm_combine(e1, e2):
    a1, b1 = e1
    a2, b2 = e2
    return a2 * a1, a2 * b1 + b2


def s5_branch(u, h0_re, h0_im, lam_re, lam_im, log_dt, b_re, b_im, c_re, c_im, d_skip, w_glu, b_glu):
    f32 = jnp.float32
    B, T, _ = u.shape
    uf = u.astype(f32)
    lam = lax.complex(lam_re.astype(f32), lam_im.astype(f32))
    dt = jnp.exp(log_dt.astype(f32))[:, None]
    lam_bar = jnp.exp(lam * dt)
    bmat = lax.complex(b_re.astype(f32), b_im.astype(f32))
    b_bar = ((lam_bar - 1.0) / lam)[..., None] * bmat
    cmat = lax.complex(c_re.astype(f32), c_im.astype(f32))
    ug = uf.reshape(B, T, SSM_GROUPS, SSM_GROUP_CH).astype(jnp.complex64)
    bu = jnp.einsum('gpc,btgc->btgp', b_bar, ug)
    h0 = lax.complex(h0_re.astype(f32), h0_im.astype(f32))
    bu = bu.at[:, 0].add(lam_bar[None] * h0)
    a = jnp.broadcast_to(lam_bar, bu.shape)
    _, hs = lax.associative_scan(ssm_combine, (a, bu), axis=1)
    y = jnp.real(jnp.einsum('gcp,btgp->btgc', cmat, hs)).reshape(B, T, SSM_W) + d_skip.astype(f32) * uf
    z = jax.nn.gelu(y)
    out = z * jax.nn.sigmoid(z @ w_glu.astype(f32) + b_glu.astype(f32))
    h_last = hs[:, -1]
    return out, jnp.real(h_last), jnp.imag(h_last)


def diff_attend(q, k, v, q_pos, k_pos, lam):
    B, Tq, H, _ = q.shape
    Tk = k.shape[1]
    qh = q.astype(jnp.float32).reshape(B, Tq, H, 2, DA_HD)
    kh = k.astype(jnp.float32).reshape(B, Tk, H, 2, DA_HD)
    s = jnp.einsum('bqhmd,bkhmd->mbhqk', qh, kh) * (DA_HD ** -0.5)
    mask = k_pos[None, :] <= q_pos[:, None]
    p = jax.nn.softmax(jnp.where(mask, s, -jnp.inf), axis=-1)
    wts = p[0] - lam * p[1]
    return jnp.einsum('bhqk,bkhd->bqhd', wts, v.astype(jnp.float32))


def blocked_diff_attention(q, k, v, q_pos, k_pos, lam):
    B, T = q.shape[:2]
    if T <= Q_BLOCK or T % Q_BLOCK != 0:
        return diff_attend(q, k, v, q_pos, k_pos, lam)
    nb = T // Q_BLOCK
    qb = jnp.moveaxis(q.reshape((B, nb, Q_BLOCK) + q.shape[2:]), 1, 0)
    pb = q_pos.reshape(nb, Q_BLOCK)
    ob = lax.map(lambda a: diff_attend(a[0], k, v, a[1], k_pos, lam), (qb, pb))
    return jnp.moveaxis(ob, 0, 1).reshape((B, T) + ob.shape[3:])


def run_layer(x, c, p, layer_idx, conv_buf, s0, h0_re, h0_im, k_past, v_past, start):
    f32 = jnp.float32
    B, T, D = x.shape
    mod = jax.nn.silu(c.astype(f32)) @ p['w_ada'].astype(f32) + p['b_ada'].astype(f32)
    shift, scale, gate = mod[:, :D], mod[:, D:2 * D], mod[:, 2 * D:]
    h = rmsnorm(x, p['norm_g']) * (1.0 + scale[:, None]) + shift[:, None]
    proj = h @ p['w_in'].astype(f32)
    dn_qkv, dn_b, dn_a, dn_gate, s_u, s_gate, da_q, da_k, da_v, da_gate = split_cols(proj, SPLITS)

    qkv, new_conv = short_conv(dn_qkv, conv_buf, p['conv_w'])
    dq, dk, dv = split_cols(qkv, (DN_HEADS * DN_DK, DN_HEADS * DN_DK, DN_W))
    dq = l2norm(dq.reshape(B, T, DN_HEADS, DN_DK))
    dk = l2norm(dk.reshape(B, T, DN_HEADS, DN_DK))
    dv = dv.reshape(B, T, DN_HEADS, DN_DV)
    beta = jax.nn.sigmoid(dn_b)
    g_log = -jnp.exp(p['dn_a_log'].astype(f32)) * jax.nn.softplus(dn_a + p['dn_dt_bias'].astype(f32))
    o_dn, s_new = gated_delta(dq, dk, dv, beta, g_log, s0)
    o_dn = rmsnorm(o_dn, p['dn_onorm']) * jax.nn.silu(dn_gate).reshape(B, T, DN_HEADS, DN_DV)
    o_dn = o_dn.reshape(B, T, DN_W)

    o_ssm, h_re, h_im = s5_branch(s_u, h0_re, h0_im, p['ssm_lam_re'], p['ssm_lam_im'], p['ssm_log_dt'],
                                  p['ssm_b_re'], p['ssm_b_im'], p['ssm_c_re'], p['ssm_c_im'],
                                  p['ssm_d'], p['ssm_w_glu'], p['ssm_b_glu'])
    o_ssm = o_ssm * jax.nn.silu(s_gate)

    q_pos = start + jnp.arange(T, dtype=jnp.int32)
    pos_f = q_pos.astype(f32)
    q = rope(da_q.reshape(B, T, DA_HEADS, 2, DA_HD), pos_f).reshape(B, T, DA_HEADS, 2 * DA_HD)
    k = rope(da_k.reshape(B, T, DA_HEADS, 2, DA_HD), pos_f).reshape(B, T, DA_HEADS, 2 * DA_HD)
    v = da_v.reshape(B, T, DA_HEADS, DA_VD)
    if k_past is None:
        k_all, v_all, k_pos = k, v, q_pos
    else:
        k_all = jnp.concatenate([k_past.astype(k.dtype), k], axis=1)
        v_all = jnp.concatenate([v_past.astype(v.dtype), v], axis=1)
        k_pos = jnp.arange(k_past.shape[1] + T, dtype=jnp.int32)
    lam_init = 0.8 - 0.6 * math.exp(-0.3 * layer_idx)
    lam = (jnp.exp(jnp.sum(p['da_lam_q1'].astype(f32) * p['da_lam_k1'].astype(f32)))
           - jnp.exp(jnp.sum(p['da_lam_q2'].astype(f32) * p['da_lam_k2'].astype(f32))) + lam_init)
    o_da = blocked_diff_attention(q, k_all, v_all, q_pos, k_pos, lam)
    o_da = rmsnorm(o_da, p['da_subln']) * (1.0 - lam_init) * jax.nn.silu(da_gate).reshape(B, T, DA_HEADS, DA_VD)

    mixed = jnp.concatenate([o_dn, o_ssm, o_da.reshape(B, T, DA_W)], axis=-1) @ p['w_out'].astype(f32)
    y = (x.astype(f32) + gate[:, None] * mixed).astype(x.dtype)
    return y, new_conv, s_new, h_re, h_im, k, v


def setup_inputs(seed: int = 0) -> dict:
    key = jax.random.key(seed)
    keys = jax.random.split(key, 48)
    ks = iter([keys[i] for i in range(48)])
    f32 = jnp.float32

    def normal(shape, std):
        return jax.random.normal(next(ks), shape, f32) * std

    def uniform(shape, lo, hi):
        return jax.random.uniform(next(ks), shape, f32, lo, hi)

    n_pages = PAST_LEN // PAGE_SIZE
    n_used = DEC_BATCH * n_pages
    n_pool = n_used + max(1, n_used // 4)
    page_table = jax.random.permutation(next(ks), n_pool)[:n_used].reshape(DEC_BATCH, n_pages).astype(jnp.int32)

    x_prompt = normal((BATCH, SEQ, D_MODEL), 1.0)
    x_sample = normal((DEC_BATCH, DEC_SEQ, D_MODEL), 1.0)
    c_prompt = normal((BATCH, D_MODEL), 1.0)
    c_sample = normal((DEC_BATCH, D_MODEL), 1.0)
    cache_k = normal((DEPTH, n_pool, PAGE_SIZE, DA_HEADS, 2 * DA_HD), 1.0)
    cache_v = normal((DEPTH, n_pool, PAGE_SIZE, DA_HEADS, DA_VD), 1.0)
    state_conv = normal((DEPTH, DEC_BATCH, CONV_K - 1, DN_CONV_CH), 1.0)
    state_delta = normal((DEPTH, DEC_BATCH, DN_HEADS, DN_DK, DN_DV), 0.1)
    state_ssm_re = normal((DEPTH, DEC_BATCH, SSM_GROUPS, SSM_P), 0.1)
    state_ssm_im = normal((DEPTH, DEC_BATCH, SSM_GROUPS, SSM_P), 0.1)

    norm_g = 1.0 + normal((DEPTH, D_MODEL), 0.02)
    w_ada = normal((DEPTH, D_MODEL, 3 * D_MODEL), 0.3 * D_MODEL ** -0.5)
    b_ada = normal((DEPTH, 3 * D_MODEL), 0.02)
    w_in = normal((DEPTH, D_MODEL, IN_W), D_MODEL ** -0.5)
    conv_w = normal((DEPTH, CONV_K, DN_CONV_CH), 0.5)
    dn_a_log = jnp.log(uniform((DEPTH, DN_HEADS), 1.0, 16.0))
    dt0 = jnp.exp(uniform((DEPTH, DN_HEADS), math.log(1e-3), math.log(1e-1)))
    dn_dt_bias = dt0 + jnp.log(-jnp.expm1(-dt0))
    dn_onorm = 1.0 + normal((DEPTH, DN_DV), 0.02)
    ssm_lam_re = -0.5 + normal((DEPTH, SSM_GROUPS, SSM_P), 0.01)
    ssm_lam_im = math.pi * jnp.arange(SSM_P, dtype=f32)[None, None, :] + normal((DEPTH, SSM_GROUPS, SSM_P), 0.01)
    ssm_log_dt = uniform((DEPTH, SSM_GROUPS), math.log(1e-3), math.log(1e-1))
    ssm_b_re = normal((DEPTH, SSM_GROUPS, SSM_P, SSM_GROUP_CH), (2.0 * SSM_GROUP_CH) ** -0.5)
    ssm_b_im = normal((DEPTH, SSM_GROUPS, SSM_P, SSM_GROUP_CH), (2.0 * SSM_GROUP_CH) ** -0.5)
    ssm_c_re = normal((DEPTH, SSM_GROUPS, SSM_GROUP_CH, SSM_P), SSM_P ** -0.5)
    ssm_c_im = normal((DEPTH, SSM_GROUPS, SSM_GROUP_CH, SSM_P), SSM_P ** -0.5)
    ssm_d = normal((DEPTH, SSM_W), 1.0)
    ssm_w_glu = normal((DEPTH, SSM_W, SSM_W), SSM_W ** -0.5)
    ssm_b_glu = normal((DEPTH, SSM_W), 0.02)
    da_lam_q1 = normal((DEPTH, DA_HD), 0.1)
    da_lam_k1 = normal((DEPTH, DA_HD), 0.1)
    da_lam_q2 = normal((DEPTH, DA_HD), 0.1)
    da_lam_k2 = normal((DEPTH, DA_HD), 0.1)
    da_subln = 1.0 + normal((DEPTH, DA_VD), 0.02)
    w_out = normal((DEPTH, MIX_W, D_MODEL), MIX_W ** -0.5)
    final_g = 1.0 + normal((D_MODEL,), 0.02)
    return {'x_prompt': x_prompt, 'x_sample': x_sample, 'c_prompt': c_prompt, 'c_sample': c_sample,
            'cache_k': cache_k, 'cache_v': cache_v, 'page_table': page_table,
            'state_conv': state_conv, 'state_delta': state_delta,
            'state_ssm_re': state_ssm_re, 'state_ssm_im': state_ssm_im,
            'norm_g': norm_g, 'w_ada': w_ada, 'b_ada': b_ada, 'w_in': w_in, 'conv_w': conv_w,
            'dn_a_log': dn_a_log, 'dn_dt_bias': dn_dt_bias, 'dn_onorm': dn_onorm,
            'ssm_lam_re': ssm_lam_re, 'ssm_lam_im': ssm_lam_im, 'ssm_log_dt': ssm_log_dt,
            'ssm_b_re': ssm_b_re, 'ssm_b_im': ssm_b_im, 'ssm_c_re': ssm_c_re, 'ssm_c_im': ssm_c_im,
            'ssm_d': ssm_d, 'ssm_w_glu': ssm_w_glu, 'ssm_b_glu': ssm_b_glu,
            'da_lam_q1': da_lam_q1, 'da_lam_k1': da_lam_k1, 'da_lam_q2': da_lam_q2, 'da_lam_k2': da_lam_k2,
            'da_subln': da_subln, 'w_out': w_out, 'final_g': final_g}


def reference(x_prompt, x_sample, c_prompt, c_sample, cache_k, cache_v, page_table,
              state_conv, state_delta, state_ssm_re, state_ssm_im,
              norm_g, w_ada, b_ada, w_in, conv_w, dn_a_log, dn_dt_bias, dn_onorm,
              ssm_lam_re, ssm_lam_im, ssm_log_dt, ssm_b_re, ssm_b_im, ssm_c_re, ssm_c_im,
              ssm_d, ssm_w_glu, ssm_b_glu, da_lam_q1, da_lam_k1, da_lam_q2, da_lam_k2,
              da_subln, w_out, final_g):
    f32 = jnp.float32
    bp = x_prompt.shape[0]
    bs = x_sample.shape[0]
    n_pages = page_table.shape[1]
    past = n_pages * PAGE_SIZE
    zero_conv = jnp.zeros((bp, CONV_K - 1, DN_CONV_CH), f32)
    zero_delta = jnp.zeros((bp, DN_HEADS, DN_DK, DN_DV), f32)
    zero_h = jnp.zeros((bp, SSM_GROUPS, SSM_P), f32)
    xp, xs = x_prompt, x_sample
    kp_l, vp_l, ks_l, vs_l = [], [], [], []
    cp_l, cs_l, dp_l, ds_l = [], [], [], []
    hrp_l, hip_l, hrs_l, his_l = [], [], [], []
    for l in range(DEPTH):
        p = {'norm_g': norm_g[l], 'w_ada': w_ada[l], 'b_ada': b_ada[l], 'w_in': w_in[l],
             'conv_w': conv_w[l], 'dn_a_log': dn_a_log[l], 'dn_dt_bias': dn_dt_bias[l],
             'dn_onorm': dn_onorm[l], 'ssm_lam_re': ssm_lam_re[l], 'ssm_lam_im': ssm_lam_im[l],
             'ssm_log_dt': ssm_log_dt[l], 'ssm_b_re': ssm_b_re[l], 'ssm_b_im': ssm_b_im[l],
             'ssm_c_re': ssm_c_re[l], 'ssm_c_im': ssm_c_im[l], 'ssm_d': ssm_d[l],
             'ssm_w_glu': ssm_w_glu[l], 'ssm_b_glu': ssm_b_glu[l],
             'da_lam_q1': da_lam_q1[l], 'da_lam_k1': da_lam_k1[l], 'da_lam_q2': da_lam_q2[l],
             'da_lam_k2': da_lam_k2[l], 'da_subln': da_subln[l], 'w_out': w_out[l]}
        xp, cp, dp, hrp, hip, kp, vp = run_layer(xp, c_prompt, p, l, zero_conv, zero_delta,
                                                 zero_h, zero_h, None, None, 0)
        k_past = cache_k[l][page_table].reshape(bs, past, DA_HEADS, 2 * DA_HD)
        v_past = cache_v[l][page_table].reshape(bs, past, DA_HEADS, DA_VD)
        xs, cs, ds, hrs, his, ks, vs = run_layer(xs, c_sample, p, l, state_conv[l], state_delta[l],
                                                 state_ssm_re[l], state_ssm_im[l], k_past, v_past, past)
        kp_l.append(kp); vp_l.append(vp); ks_l.append(ks); vs_l.append(vs)
        cp_l.append(cp); cs_l.append(cs); dp_l.append(dp); ds_l.append(ds)
        hrp_l.append(hrp); hip_l.append(hip); hrs_l.append(hrs); his_l.append(his)
    y_prompt = rmsnorm(xp, final_g).astype(x_prompt.dtype)
    y_sample = rmsnorm(xs, final_g).astype(x_sample.dtype)
    k_prompt = jnp.stack(kp_l)
    v_prompt = jnp.stack(vp_l)
    k_sample = jnp.stack(ks_l)
    v_sample = jnp.stack(vs_l)
    conv_prompt = jnp.stack(cp_l)
    conv_sample = jnp.stack(cs_l)
    delta_prompt = jnp.stack(dp_l)
    delta_sample = jnp.stack(ds_l)
    ssm_re_prompt = jnp.stack(hrp_l)
    ssm_im_prompt = jnp.stack(hip_l)
    ssm_re_sample = jnp.stack(hrs_l)
    ssm_im_sample = jnp.stack(his_l)
    return (y_prompt, y_sample, k_prompt, v_prompt, k_sample, v_sample, conv_prompt, conv_sample,
            delta_prompt, delta_sample, ssm_re_prompt, ssm_im_prompt, ssm_re_sample, ssm_im_sample)
```

```python
import functools
import math

import jax
import jax.numpy as jnp
from jax import lax
from jax.experimental import pallas as pl
from jax.experimental.pallas import tpu as pltpu

F32 = jnp.float32
BF16 = jnp.bfloat16

D_MODEL = 1024
DEPTH = 4
PAGE_SIZE = 128
DN_HEADS = 6
DN_DK = 64
DN_DV = 64
DN_W = DN_HEADS * DN_DV
DN_CONV_CH = 2 * DN_HEADS * DN_DK + DN_W
CONV_K = 4
DN_CHUNK = 64
SSM_GROUPS = 16
SSM_GROUP_CH = 16
SSM_W = SSM_GROUPS * SSM_GROUP_CH
SSM_P = 64
SSM_N = SSM_GROUPS * SSM_P
DA_HEADS = 6
DA_HD = 32
DA_VD = 2 * DA_HD
DA_W = DA_HEADS * DA_VD
MIX_W = DN_W + SSM_W + DA_W
ROPE_THETA = 10000.0
NORM_EPS = 1e-6
NEG_BIG = -1e30

SEG_QKV = (0, 1152)
SEG_DNG = (1152, 1536)
SEG_SU = (1536, 1792)
SEG_SG = (1792, 2048)
SEG_Q = (2048, 2432)
SEG_K = (2432, 2816)
SEG_V = (2816, 3200)
SEG_DAG = (3200, 3584)
SEG_SM = (3584, 3712)
IN_W_PAD = 3712

TM_PROJ = 512
DN_ROWS = 256
S5_ROWS = 512
S5_SUB = 64
S5_SLABS = SSM_N // 128
TQ = 512
PAGES_PER_STEP = 8
ATT_PAD = 16
VMEM_LIMIT = 56 * 1024 * 1024

Q_SCALE = (DA_HD ** -0.5) * math.log2(math.e)


def _mm(a, b):
    return jnp.dot(a.astype(BF16), b.astype(BF16), preferred_element_type=F32)


def _mm_nt(a, b):
    return lax.dot_general(a.astype(BF16), b.astype(BF16), (((1,), (1,)), ((), ())),
                           preferred_element_type=F32)


def _mm_tn(a, b):
    return lax.dot_general(a.astype(BF16), b.astype(BF16), (((0,), (0,)), ((), ())),
                           preferred_element_type=F32)


def _split(a):
    hi = a.astype(BF16)
    lo = (a - hi.astype(F32)).astype(BF16)
    return hi, lo


def _mm_split_lhs(a, b_bf16):
    hi, lo = _split(a)
    return (jnp.dot(hi, b_bf16, preferred_element_type=F32)
            + jnp.dot(lo, b_bf16, preferred_element_type=F32))


def _mm3(a, b):
    ah, al = _split(a)
    bh, bl = _split(b)
    return (jnp.dot(ah, bh, preferred_element_type=F32)
            + jnp.dot(ah, bl, preferred_element_type=F32)
            + jnp.dot(al, bh, preferred_element_type=F32))


def _silu(x):
    return x * jax.nn.sigmoid(x)


def _softplus(x):
    return jnp.maximum(x, 0.0) + jnp.log1p(jnp.exp(-jnp.abs(x)))


def _gelu_tanh(x):
    c = math.sqrt(2.0 / math.pi)
    return x * (0.5 * (1.0 + jnp.tanh(c * (x + 0.044715 * (x * x * x)))))


def _params(*sem):
    return pltpu.CompilerParams(dimension_semantics=sem, vmem_limit_bytes=VMEM_LIMIT)


def _ada_kernel(c_ref, w_ref, b_ref, o_ref):
    c = c_ref[...]
    o_ref[0] = _mm3(_silu(c), w_ref[0]) + b_ref[0]


def _ada_call(c_all, w_ada, b_ada):
    rows = c_all.shape[0]
    tn = 1024
    return pl.pallas_call(
        _ada_kernel,
        grid=(DEPTH, 3 * D_MODEL // tn),
        in_specs=[pl.BlockSpec((rows, D_MODEL), lambda l, n: (0, 0)),
                  pl.BlockSpec((1, D_MODEL, tn), lambda l, n: (l, 0, n)),
                  pl.BlockSpec((1, 1, tn), lambda l, n: (l, 0, n))],
        out_specs=pl.BlockSpec((1, rows, tn), lambda l, n: (l, 0, n)),
        out_shape=jax.ShapeDtypeStruct((DEPTH, rows, 3 * D_MODEL), F32),
        compiler_params=_params("parallel", "parallel"),
        name="adaln",
    )(c_all, w_ada, b_ada.reshape(DEPTH, 1, 3 * D_MODEL))


def _rope(x, cos, sin):
    lane = lax.broadcasted_iota(jnp.int32, cos.shape, 1)
    low = (lane & 16) == 0
    outs = []
    for c in range(x.shape[1] // 128):
        xc = x[:, c * 128:(c + 1) * 128]
        sw = jnp.where(low, pltpu.roll(xc, 112, 1), pltpu.roll(xc, 16, 1))
        outs.append(xc * cos + sw * sin)
    return jnp.concatenate(outs, axis=1)


def _inproj_kernel(x_ref, sc_ref, sh_ref, g_ref, cos_ref, sin_ref, w_ref, *outs, head_major):
    x = x_ref[0]
    ms = jnp.mean(x * x, axis=-1, keepdims=True)
    h = x * lax.rsqrt(ms + NORM_EPS) * g_ref[...] * (1.0 + sc_ref[0]) + sh_ref[0]
    hb = h.astype(BF16)

    def seg(ab):
        return jnp.dot(hb, w_ref[0, :, ab[0]:ab[1]], preferred_element_type=F32)

    qkv_o, dng_o, su_o, sg_o, dag_o, sm_o, q_o, k_o, v_o = outs
    qkv_o[0] = seg(SEG_QKV)
    dng_o[0] = seg(SEG_DNG)
    su_o[0] = seg(SEG_SU)
    sg_o[0] = seg(SEG_SG)
    dag_o[0] = seg(SEG_DAG)
    sm_o[0] = seg(SEG_SM)
    cos = cos_ref[...]
    sin = sin_ref[...]
    q = _rope(seg(SEG_Q), cos, sin) * Q_SCALE
    k = _rope(seg(SEG_K), cos, sin)
    v = seg(SEG_V)
    if head_major:
        for hd in range(DA_HEADS):
            sl = slice(hd * DA_VD, (hd + 1) * DA_VD)
            q_o[0, hd] = q[:, sl].T.astype(BF16)
            k_o[0, hd] = k[:, sl].T
            v_o[0, hd] = v[:, sl].T
    else:
        q_o[0] = q
        k_o[0] = k
        v_o[0] = v


def _inproj_call(x, scale, shift, norm_g, cos, sin, w_perm, layer, head_major):
    b, t, _ = x.shape
    tm = min(TM_PROJ, t)
    per_row = scale.shape[1] != 1
    tmm = tm if per_row else 1
    mod_map = (lambda bi, i: (bi, i, 0)) if per_row else (lambda bi, i: (bi, 0, 0))
    row_map = lambda bi, i: (bi, i, 0)

    def row_spec(w):
        return pl.BlockSpec((1, tm, w), row_map)

    def row_shape(w):
        return jax.ShapeDtypeStruct((b, t, w), F32)

    out_specs = [row_spec(1152), row_spec(384), row_spec(256), row_spec(256), row_spec(384), row_spec(128)]
    out_shape = [row_shape(1152), row_shape(384), row_shape(256), row_shape(256), row_shape(384), row_shape(128)]
    if head_major:
        for dt in (BF16, F32, F32):
            out_specs.append(pl.BlockSpec((1, DA_HEADS, DA_VD, tm), lambda bi, i: (bi, 0, 0, i)))
            out_shape.append(jax.ShapeDtypeStruct((b, DA_HEADS, DA_VD, t), dt))
    else:
        out_specs += [row_spec(384)] * 3
        out_shape += [row_shape(384)] * 3
    return pl.pallas_call(
        functools.partial(_inproj_kernel, head_major=head_major),
        grid=(b, t // tm),
        in_specs=[row_spec(D_MODEL),
                  pl.BlockSpec((1, tmm, D_MODEL), mod_map),
                  pl.BlockSpec((1, tmm, D_MODEL), mod_map),
                  pl.BlockSpec((1, D_MODEL), lambda bi, i: (0, 0)),
                  pl.BlockSpec((tm, 128), lambda bi, i: (i, 0)),
                  pl.BlockSpec((tm, 128), lambda bi, i: (i, 0)),
                  pl.BlockSpec((1, D_MODEL, IN_W_PAD), lambda bi, i: (layer, 0, 0))],
        out_specs=out_specs,
        out_shape=out_shape,
        compiler_params=_params("parallel", "parallel"),
        name="inproj",
    )(x, scale, shift, norm_g, cos, sin, w_perm)


def _delta_local_kernel(x_ref, prev_ref, buf_ref, cw_ref, sm_ref, alog_ref, dtb_ref, ltri_ref, bd_ref,
                        u_o, w_o, qd_o, kd_o, at_o, gc_o, xs_ref, *, rows, t_valid):
    i = pl.program_id(1)
    halo = jnp.where(i == 0, buf_ref[0], prev_ref[0])
    xs_ref[0:8, :] = halo
    xs_ref[8:8 + rows, :] = x_ref[0]
    cw = cw_ref[...]
    y = (xs_ref[pl.ds(5, rows), :] * cw[0:1] + xs_ref[pl.ds(6, rows), :] * cw[1:2]
         + xs_ref[pl.ds(7, rows), :] * cw[2:3] + xs_ref[pl.ds(8, rows), :] * cw[3:4])
    y = _silu(y)
    qk = y[:, 0:2 * DN_W]
    ss = _mm_split_lhs(qk * qk, bd_ref[...])
    qkn = qk * lax.rsqrt(ss + NORM_EPS)
    v_all = y[:, 2 * DN_W:]

    sm = sm_ref[0]
    row = lax.broadcasted_iota(jnp.int32, sm.shape, 0) + i * rows
    valid = row < t_valid
    beta = jnp.where(valid, jax.nn.sigmoid(sm), 0.0)
    g = jnp.where(valid, -jnp.exp(alog_ref[...]) * _softplus(sm + dtb_ref[...]), 0.0)
    gc = _mm_split_lhs_rhs(ltri_ref[...], g)
    gc_o[0] = gc
    gc_t = gc.T

    ri = lax.broadcasted_iota(jnp.int32, (rows, rows), 0)
    ci = lax.broadcasted_iota(jnp.int32, (rows, rows), 1)
    same_chunk = (ri >> 6) == (ci >> 6)
    causal_f = jnp.where(same_chunk & (ri >= ci), 1.0, 0.0)
    strict_f = jnp.where(same_chunk & (ri > ci), 1.0, 0.0)
    eye_f = jnp.where(ri == ci, 1.0, 0.0)
    level_f = []
    s = 1
    while s < DN_CHUNK:
        sh = s.bit_length()
        m = ((ri >> sh) == (ci >> sh)) & ((ri & s) != 0) & ((ci & s) == 0)
        level_f.append(jnp.where(m, 1.0, 0.0))
        s *= 2
    n_chunks = rows // DN_CHUNK
    scale = DN_DK ** -0.5

    for hd in range(DN_HEADS):
        qh = qkn[:, hd * DN_DK:(hd + 1) * DN_DK] * scale
        kh = qkn[:, DN_W + hd * DN_DK:DN_W + (hd + 1) * DN_DK]
        vh = v_all[:, hd * DN_DV:(hd + 1) * DN_DV]
        beta_c = beta[:, hd:hd + 1]
        gcol = gc[:, DN_HEADS + hd:DN_HEADS + hd + 1]
        grow = gc_t[DN_HEADS + hd:DN_HEADS + hd + 1, :]
        decay = jnp.exp(jnp.where(causal_f > 0.0, gcol - grow, NEG_BIG))
        kb = kh * beta_c
        m_full = _mm_nt(kb, kh) * decay
        m_strict = m_full * strict_f
        x_inv = eye_f - m_strict * level_f[0]
        for lf in level_f[1:]:
            x_inv = x_inv - _mm(_mm(x_inv, m_strict * lf), x_inv)
        e_g = jnp.exp(gcol)
        rhs = jnp.concatenate([vh * beta_c, kb * e_g], axis=1)
        sol = _mm(x_inv, rhs)
        attn_full = _mm_nt(qh, kh) * decay
        attn = jnp.concatenate(
            [attn_full[c * DN_CHUNK:(c + 1) * DN_CHUNK, c * DN_CHUNK:(c + 1) * DN_CHUNK]
             for c in range(n_chunks)], axis=0)
        g_last = jnp.concatenate(
            [jnp.broadcast_to(gcol[c * DN_CHUNK + DN_CHUNK - 1:(c + 1) * DN_CHUNK, :], (DN_CHUNK, 1))
             for c in range(n_chunks)], axis=0)
        u_o[0, hd] = sol[:, 0:DN_DV]
        w_o[0, hd] = sol[:, DN_DV:]
        qd_o[0, hd] = qh * e_g
        kd_o[0, hd] = kh * jnp.exp(g_last - gcol)
        at_o[0, hd] = attn


def _mm_split_lhs_rhs(a_bf16, b):
    hi, lo = _split(b)
    return (jnp.dot(a_bf16, hi, preferred_element_type=F32)
            + jnp.dot(a_bf16, lo, preferred_element_type=F32))


def _delta_local_call(qkv, buf8, conv_w8, sm, alog_row, dtb_row, rows, t_valid):
    b, t, _ = qkv.shape
    nblk = t // rows
    r = jnp.arange(rows)
    ltri = ((r[:, None] >= r[None, :]) & ((r[:, None] // DN_CHUNK) == (r[None, :] // DN_CHUNK))).astype(BF16)
    c = jnp.arange(2 * DN_W)
    bd = ((c[:, None] // DN_DK) == (c[None, :] // DN_DK)).astype(BF16)
    hm_spec = pl.BlockSpec((1, DN_HEADS, rows, DN_DV), lambda bi, i: (bi, 0, i, 0))
    hm_shape = jax.ShapeDtypeStruct((b, DN_HEADS, t, DN_DV), F32)
    return pl.pallas_call(
        functools.partial(_delta_local_kernel, rows=rows, t_valid=t_valid),
        grid=(b, nblk),
        in_specs=[pl.BlockSpec((1, rows, DN_CONV_CH), lambda bi, i: (bi, i, 0)),
                  pl.BlockSpec((1, 8, DN_CONV_CH), lambda bi, i: (bi, jnp.maximum(i * (rows // 8) - 1, 0), 0)),
                  pl.BlockSpec((1, 8, DN_CONV_CH), lambda bi, i: (bi, 0, 0)),
                  pl.BlockSpec((8, DN_CONV_CH), lambda bi, i: (0, 0)),
                  pl.BlockSpec((1, rows, 128), lambda bi, i: (bi, i, 0)),
                  pl.BlockSpec((1, 128), lambda bi, i: (0, 0)),
                  pl.BlockSpec((1, 128), lambda bi, i: (0, 0)),
                  pl.BlockSpec((rows, rows), lambda bi, i: (0, 0)),
                  pl.BlockSpec((2 * DN_W, 2 * DN_W), lambda bi, i: (0, 0))],
        out_specs=[hm_spec] * 5 + [pl.BlockSpec((1, rows, 128), lambda bi, i: (bi, i, 0))],
        out_shape=[hm_shape] * 5 + [jax.ShapeDtypeStruct((b, t, 128), F32)],
        scratch_shapes=[pltpu.VMEM((rows + 8, DN_CONV_CH), F32)],
        compiler_params=_params("parallel", "parallel"),
        name="delta_local",
    )(qkv, qkv, buf8, conv_w8, sm, alog_row, dtb_row, ltri, bd)


def _delta_scan_kernel(u_ref, w_ref, qd_ref, kd_ref, at_ref, gc_ref, s0_ref, o_ref, sf_ref, s_ref, *, bb):
    c = pl.program_id(1)

    @pl.when(c == 0)
    def _():
        s_ref[...] = s0_ref[...]

    for bi in range(bb):
        for hd in range(DN_HEADS):
            s = s_ref[bi, hd]
            wq = jnp.concatenate([w_ref[bi, hd], qd_ref[bi, hd]], axis=0)
            r = _mm(wq, s)
            v_new = u_ref[bi, hd] - r[0:DN_CHUNK]
            o_ref[bi, hd] = r[DN_CHUNK:] + _mm(at_ref[bi, hd], v_new)
            g_last = jnp.exp(gc_ref[bi, DN_CHUNK - 1:DN_CHUNK, DN_HEADS + hd:DN_HEADS + hd + 1])
            s_ref[bi, hd] = s * g_last + _mm_tn(kd_ref[bi, hd], v_new)

    @pl.when(c == pl.num_programs(1) - 1)
    def _():
        sf_ref[...] = s_ref[...]


def _delta_scan_call(u, w, qd, kd, at, gc, s0):
    b, _, t, _ = u.shape
    bb = 2
    hm_spec = pl.BlockSpec((bb, DN_HEADS, DN_CHUNK, DN_DV), lambda bi, c: (bi, 0, c, 0))
    st_spec = pl.BlockSpec((bb, DN_HEADS, DN_DK, DN_DV), lambda bi, c: (bi, 0, 0, 0))
    return pl.pallas_call(
        functools.partial(_delta_scan_kernel, bb=bb),
        grid=(b // bb, t // DN_CHUNK),
        in_specs=[hm_spec] * 5 + [pl.BlockSpec((bb, DN_CHUNK, 128), lambda bi, c: (bi, c, 0)), st_spec],
        out_specs=[hm_spec, st_spec],
        out_shape=[jax.ShapeDtypeStruct((b, DN_HEADS, t, DN_DV), F32),
                   jax.ShapeDtypeStruct((b, DN_HEADS, DN_DK, DN_DV), F32)],
        scratch_shapes=[pltpu.VMEM((bb, DN_HEADS, DN_DK, DN_DV), F32)],
        compiler_params=_params("parallel", "arbitrary"),
        name="delta_scan",
    )(u, w, qd, kd, at, gc, s0)


def _s5_epilogue(y, u, sg, d_ref, wglu_ref, bglu_ref):
    z = _gelu_tanh(y + d_ref[...] * u)
    gate = jax.nn.sigmoid(_mm(z, wglu_ref[...]) + bglu_ref[...])
    return z * gate * _silu(sg)


def _s5_kernel(u_ref, sg_ref, h0r_ref, h0i_ref, bblk_ref, ar_ref, ai_ref, apr_ref, api_ref, pw_ref,
               cblk_ref, d_ref, wglu_ref, bglu_ref, o_ref, hr_o, hi_o, hs_ref, cr_ref, ci_ref):
    i = pl.program_id(1)
    ns = S5_SLABS

    @pl.when(i == 0)
    def _():
        cr_ref[...] = h0r_ref[0]
        ci_ref[...] = h0i_ref[0]

    def slab(c):
        return slice(c * 128, (c + 1) * 128)

    u = u_ref[0]
    bu = _mm(u, bblk_ref[...])
    for c in range(2 * ns):
        hs_ref[c] = bu[:, slab(c)]
    ar = [jnp.broadcast_to(ar_ref[:, slab(c)], (8, 128)) for c in range(ns)]
    ai = [jnp.broadcast_to(ai_ref[:, slab(c)], (8, 128)) for c in range(ns)]

    def body(j, carry):
        new = []
        for c in range(ns):
            hr, hi = carry[2 * c], carry[2 * c + 1]
            rows = pl.ds(j, 8, stride=S5_SUB)
            nr = ar[c] * hr - ai[c] * hi + hs_ref[c, rows, :]
            ni = ar[c] * hi + ai[c] * hr + hs_ref[ns + c, rows, :]
            hs_ref[c, rows, :] = nr
            hs_ref[ns + c, rows, :] = ni
            new += [nr, ni]
        return tuple(new)

    zero = jnp.zeros((8, 128), F32)
    ends = lax.fori_loop(0, S5_SUB, body, (zero,) * (2 * ns))

    y = jnp.zeros((S5_ROWS, SSM_W), F32)
    for c in range(ns):
        pr = pw_ref[:, slab(c)]
        pi = pw_ref[:, slab(ns + c)]
        apr = apr_ref[:, slab(c)]
        api = api_ref[:, slab(c)]
        hr = cr_ref[:, slab(c)]
        hi = ci_ref[:, slab(c)]
        er, ei = ends[2 * c], ends[2 * c + 1]
        for s in range(8):
            rows = slice(s * S5_SUB, (s + 1) * S5_SUB)
            hs_ref[c, rows, :] = hs_ref[c, rows, :] + pr * hr - pi * hi
            hs_ref[ns + c, rows, :] = hs_ref[ns + c, rows, :] + pr * hi + pi * hr
            nr = apr * hr - api * hi + er[s:s + 1]
            ni = apr * hi + api * hr + ei[s:s + 1]
            hr, hi = nr, ni
        cr_ref[:, slab(c)] = hr
        ci_ref[:, slab(c)] = hi
        y = y + _mm(hs_ref[c], cblk_ref[slab(c), :]) + _mm(hs_ref[ns + c], cblk_ref[slab(ns + c), :])

    o_ref[0] = _s5_epilogue(y, u, sg_ref[0], d_ref, wglu_ref, bglu_ref)

    @pl.when(i == pl.num_programs(1) - 1)
    def _():
        hr_o[0] = cr_ref[...]
        hi_o[0] = ci_ref[...]


def _s5_call(u, sg, h0r, h0i, sp):
    b, t, _ = u.shape
    n = SSM_N
    row_spec = pl.BlockSpec((1, S5_ROWS, SSM_W), lambda bi, i: (bi, i, 0))
    st_spec = pl.BlockSpec((1, 1, n), lambda bi, i: (bi, 0, 0))
    full = lambda shape: pl.BlockSpec(shape, lambda bi, i: (0,) * len(shape))
    return pl.pallas_call(
        _s5_kernel,
        grid=(b, t // S5_ROWS),
        in_specs=[row_spec, row_spec, st_spec, st_spec,
                  full((SSM_W, 2 * n)), full((1, n)), full((1, n)), full((1, n)), full((1, n)),
                  full((S5_SUB, 2 * n)), full((2 * n, SSM_W)), full((1, SSM_W)),
                  full((SSM_W, SSM_W)), full((1, SSM_W))],
        out_specs=[row_spec, st_spec, st_spec],
        out_shape=[jax.ShapeDtypeStruct((b, t, SSM_W), F32),
                   jax.ShapeDtypeStruct((b, 1, n), F32), jax.ShapeDtypeStruct((b, 1, n), F32)],
        scratch_shapes=[pltpu.VMEM((2 * S5_SLABS, S5_ROWS, 128), F32), pltpu.VMEM((1, n), F32),
                        pltpu.VMEM((1, n), F32)],
        compiler_params=_params("parallel", "arbitrary"),
        name="s5",
    )(u, sg, h0r, h0i, sp["bblk"], sp["ar"], sp["ai"], sp["apr"], sp["api"], sp["pw"],
      sp["cblk"], sp["d"], sp["wglu"], sp["bglu"])


def _s5_step_kernel(u_ref, sg_ref, h0r_ref, h0i_ref, bblk_ref, ar_ref, ai_ref, cblk_ref, d_ref, wglu_ref,
                    bglu_ref, o_ref, hr_o, hi_o, *, nseq, t):
    hr = h0r_ref[...]
    hi = h0i_ref[...]
    ar = ar_ref[...]
    ai = ai_ref[...]
    n = SSM_N
    for step in range(t):
        u = u_ref[step]
        sg = sg_ref[step]
        bu = _mm(u, bblk_ref[...])
        nr = ar * hr - ai * hi + bu[:, 0:n]
        ni = ar * hi + ai * hr + bu[:, n:]
        hr, hi = nr, ni
        y = _mm(jnp.concatenate([hr, hi], axis=1), cblk_ref[...])
        o_ref[step] = _s5_epilogue(y, u, sg, d_ref, wglu_ref, bglu_ref)
    hr_o[...] = hr
    hi_o[...] = hi


def _s5_step_call(u, sg, h0r, h0i, sp, nseq, t):
    n = SSM_N
    return pl.pallas_call(
        functools.partial(_s5_step_kernel, nseq=nseq, t=t),
        out_shape=[jax.ShapeDtypeStruct((t, nseq, SSM_W), F32),
                   jax.ShapeDtypeStruct((nseq, n), F32), jax.ShapeDtypeStruct((nseq, n), F32)],
        compiler_params=pltpu.CompilerParams(vmem_limit_bytes=VMEM_LIMIT),
        name="s5_step",
    )(u, sg, h0r, h0i, sp["bblk"], sp["ar"], sp["ai"], sp["cblk"], sp["d"], sp["wglu"], sp["bglu"])


def _attn_prompt_kernel(qi_tab, ki_tab, lam_ref, q_ref, k_ref, v_ref, o_ref, qcat_ref, m_ref, acc_ref):
    p = pl.program_id(2)
    qi = qi_tab[p]
    ki = ki_tab[p]
    tq = q_ref.shape[3]

    @pl.when(ki == 0)
    def _():
        q = q_ref[0, 0]
        feat = lax.broadcasted_iota(jnp.int32, q.shape, 0)
        zero = jnp.zeros_like(q)
        qcat_ref[:, 0:tq] = jnp.where(feat < DA_HD, q, zero)
        qcat_ref[:, tq:2 * tq] = jnp.where(feat >= DA_HD, q, zero)
        m_ref[...] = jnp.full(m_ref.shape, NEG_BIG, F32)
        acc_ref[...] = jnp.zeros(acc_ref.shape, F32)

    def step(masked):
        s = _mm_tn(k_ref[0, 0], qcat_ref[...])
        if masked:
            kr = lax.broadcasted_iota(jnp.int32, s.shape, 0)
            qc = lax.broadcasted_iota(jnp.int32, s.shape, 1) & (tq - 1)
            s = jnp.where(kr <= qc, s, NEG_BIG)
        m_old = m_ref[...]
        m_new = jnp.maximum(m_old, jnp.max(s, axis=0, keepdims=True))
        alpha = jnp.exp2(m_old - m_new)
        pm = jnp.exp2(s - m_new).astype(BF16)
        v = v_ref[0, 0].astype(BF16)
        v_ext = jnp.concatenate([v, jnp.ones((ATT_PAD, v.shape[1]), BF16)], axis=0)
        acc_ref[...] = alpha * acc_ref[...] + jnp.dot(v_ext, pm, preferred_element_type=F32)
        m_ref[...] = m_new

    @pl.when(ki < qi)
    def _():
        step(False)

    @pl.when(ki == qi)
    def _():
        step(True)
        acc = acc_ref[...]
        o1 = acc[0:DA_VD, 0:tq] / acc[DA_VD:DA_VD + 1, 0:tq]
        o2 = acc[0:DA_VD, tq:] / acc[DA_VD:DA_VD + 1, tq:]
        o_ref[0, 0] = (o1 - lam_ref[...] * o2).T


def _attn_prompt_call(qh, kh, vh, lam):
    b, h, _, t = qh.shape
    tq = min(TQ, t)
    nq = t // tq
    qi_tab = jnp.asarray([i for i in range(nq) for _ in range(i + 1)], jnp.int32)
    ki_tab = jnp.asarray([j for i in range(nq) for j in range(i + 1)], jnp.int32)
    grid_spec = pltpu.PrefetchScalarGridSpec(
        num_scalar_prefetch=2,
        grid=(b, h, int(qi_tab.shape[0])),
        in_specs=[pl.BlockSpec((1, 1), lambda bi, hi, p, qt, kt: (0, 0)),
                  pl.BlockSpec((1, 1, DA_VD, tq), lambda bi, hi, p, qt, kt: (bi, hi, 0, qt[p])),
                  pl.BlockSpec((1, 1, DA_VD, tq), lambda bi, hi, p, qt, kt: (bi, hi, 0, kt[p])),
                  pl.BlockSpec((1, 1, DA_VD, tq), lambda bi, hi, p, qt, kt: (bi, hi, 0, kt[p]))],
        out_specs=pl.BlockSpec((1, 1, tq, DA_VD), lambda bi, hi, p, qt, kt: (bi, hi, qt[p], 0)),
        scratch_shapes=[pltpu.VMEM((DA_VD, 2 * tq), BF16), pltpu.VMEM((1, 2 * tq), F32),
                        pltpu.VMEM((DA_VD + ATT_PAD, 2 * tq), F32)])
    return pl.pallas_call(
        _attn_prompt_kernel,
        grid_spec=grid_spec,
        out_shape=jax.ShapeDtypeStruct((b, h, t, DA_VD), F32),
        compiler_params=_params("parallel", "parallel", "arbitrary"),
        name="attn_prompt",
    )(qi_tab, ki_tab, lam, qh, kh, vh)


def _attn_sample_kernel(pt_ref, lam_ref, q_ref, kn_ref, vn_ref, *rest, t_new, n_pages):
    del pt_ref
    pp = PAGES_PER_STEP
    k_refs = rest[0:pp]
    v_refs = rest[pp:2 * pp]
    o_ref, qrows_ref, m_ref, l_ref, acc_ref = rest[2 * pp:]
    j = pl.program_id(1)
    nrow = 2 * t_new * 8

    def update(s, values, mm):
        m_old = m_ref[...]
        m_new = jnp.maximum(m_old, jnp.max(s, axis=1, keepdims=True))
        alpha = jnp.exp2(m_old - m_new)
        pm = jnp.exp2(s - m_new)
        l_ref[...] = alpha * l_ref[...] + jnp.sum(pm, axis=1, keepdims=True)
        acc = alpha * acc_ref[...]
        width = s.shape[1] // len(values)
        for idx, v in enumerate(values):
            acc = acc + mm(pm[:, idx * width:(idx + 1) * width], v)
        acc_ref[...] = acc
        m_ref[...] = m_new

    @pl.when(j == 0)
    def _():
        q = q_ref[0]
        sub = lax.broadcasted_iota(jnp.int32, (8, DA_W), 0)
        lane = lax.broadcasted_iota(jnp.int32, (8, DA_W), 1)
        for mp in range(2):
            keep = ((lane >> 6) == sub) & (((lane >> 5) & 1) == mp)
            for qi in range(t_new):
                r0 = mp * t_new * 8 + qi * 8
                qb = jnp.broadcast_to(q[qi:qi + 1, :], (8, DA_W))
                qrows_ref[r0:r0 + 8, :] = jnp.where(keep, qb, 0.0).astype(BF16)
        m_ref[...] = jnp.full(m_ref.shape, NEG_BIG, F32)
        l_ref[...] = jnp.zeros(l_ref.shape, F32)
        acc_ref[...] = jnp.zeros(acc_ref.shape, F32)
        pad = jnp.zeros((16 - t_new, DA_W), F32)
        k8 = jnp.concatenate([kn_ref[0], pad], axis=0)
        v8 = jnp.concatenate([vn_ref[0], pad], axis=0)
        s = _mm_nt(qrows_ref[...], k8)
        key = lax.broadcasted_iota(jnp.int32, s.shape, 1)
        qidx = (lax.broadcasted_iota(jnp.int32, s.shape, 0) >> 3) & (t_new - 1)
        s = jnp.where(key <= qidx, s, NEG_BIG)
        update(s, [v8], _mm)

    s_all = jnp.concatenate([_mm(qrows_ref[...], kr[...]) for kr in k_refs], axis=1)
    update(s_all, [vr[...] for vr in v_refs], _mm_nt)

    @pl.when(j == n_pages // pp - 1)
    def _():
        o = acc_ref[...] / l_ref[...]
        half = nrow // 2
        oc = o[0:half] - lam_ref[...] * o[half:]
        hd = lax.broadcasted_iota(jnp.int32, oc.shape, 0) & 7
        lane = lax.broadcasted_iota(jnp.int32, oc.shape, 1)
        oc = jnp.where((lane >> 6) == hd, oc, 0.0)
        o_ref[0] = jnp.sum(oc.reshape(t_new, 8, DA_W), axis=1)


def _attn_sample_call(q, k_new, v_new, cache_k, cache_v, page_table, lam, layer):
    nseq, t_new, _ = q.shape
    n_pages = page_table.shape[1]
    pp = PAGES_PER_STEP
    nrow = 2 * t_new * 8
    pt_flat = page_table.reshape(-1).astype(jnp.int32)

    def page_spec(idx):
        return pl.BlockSpec((None, None, DA_W, PAGE_SIZE),
                            lambda bi, j, pt: (layer, pt[bi * n_pages + j * pp + idx], 0, 0))

    tok_spec = pl.BlockSpec((1, t_new, DA_W), lambda bi, j, pt: (bi, 0, 0))
    grid_spec = pltpu.PrefetchScalarGridSpec(
        num_scalar_prefetch=1,
        grid=(nseq, n_pages // pp),
        in_specs=[pl.BlockSpec((1, 1), lambda bi, j, pt: (0, 0)), tok_spec, tok_spec, tok_spec]
        + [page_spec(i) for i in range(pp)] + [page_spec(i) for i in range(pp)],
        out_specs=tok_spec,
        scratch_shapes=[pltpu.VMEM((nrow, DA_W), BF16), pltpu.VMEM((nrow, 1), F32),
                        pltpu.VMEM((nrow, 1), F32), pltpu.VMEM((nrow, DA_W), F32)])
    return pl.pallas_call(
        functools.partial(_attn_sample_kernel, t_new=t_new, n_pages=n_pages),
        grid_spec=grid_spec,
        out_shape=jax.ShapeDtypeStruct((nseq, t_new, DA_W), F32),
        compiler_params=_params("parallel", "arbitrary"),
        name="attn_sample",
    )(pt_flat, lam, q, k_new, v_new, *([cache_k] * pp), *([cache_v] * pp))


def _mix_kernel(x_ref, gate_ref, odn_ref, dng_ref, ossm_ref, oda_ref, dag_ref, onorm_ref, subln_ref, bd_ref,
                w_ref, y_ref, *, head_major):
    if head_major:
        odn = jnp.concatenate([odn_ref[0, hd] for hd in range(DN_HEADS)], axis=1)
        oda = jnp.concatenate([oda_ref[0, hd] for hd in range(DA_HEADS)], axis=1)
    else:
        odn = odn_ref[0]
        oda = oda_ref[0]

    def head_norm(o, gain):
        ms = _mm_split_lhs(o * o, bd_ref[...]) * (1.0 / DN_DV)
        return o * lax.rsqrt(ms + NORM_EPS) * gain

    a = head_norm(odn, onorm_ref[...]) * _silu(dng_ref[0])
    c = head_norm(oda, subln_ref[...]) * _silu(dag_ref[0])
    mixed = (jnp.dot(a.astype(BF16), w_ref[0, 0:DN_W, :], preferred_element_type=F32)
             + jnp.dot(ossm_ref[0].astype(BF16), w_ref[0, DN_W:DN_W + SSM_W, :], preferred_element_type=F32)
             + jnp.dot(c.astype(BF16), w_ref[0, DN_W + SSM_W:, :], preferred_element_type=F32))
    y_ref[0] = x_ref[0] + gate_ref[0] * mixed


def _mix_call(x, gate, odn, dng, ossm, oda, dag, onorm_row, subln_row, w_out_bf16, layer, head_major):
    b, t, _ = x.shape
    tm = min(TM_PROJ, t)
    per_row = gate.shape[1] != 1
    tmm = tm if per_row else 1
    mod_map = (lambda bi, i: (bi, i, 0)) if per_row else (lambda bi, i: (bi, 0, 0))
    row_map = lambda bi, i: (bi, i, 0)
    if head_major:
        o_spec = pl.BlockSpec((1, DN_HEADS, tm, DN_DV), lambda bi, i: (bi, 0, i, 0))
    else:
        o_spec = pl.BlockSpec((1, tm, DN_W), row_map)
    c = jnp.arange(DN_W)
    bd = ((c[:, None] // DN_DV) == (c[None, :] // DN_DV)).astype(BF16)
    return pl.pallas_call(
        functools.partial(_mix_kernel, head_major=head_major),
        grid=(b, t // tm),
        in_specs=[pl.BlockSpec((1, tm, D_MODEL), row_map),
                  pl.BlockSpec((1, tmm, D_MODEL), mod_map),
                  o_spec,
                  pl.BlockSpec((1, tm, DN_W), row_map),
                  pl.BlockSpec((1, tm, SSM_W), row_map),
                  o_spec,
                  pl.BlockSpec((1, tm, DA_W), row_map),
                  pl.BlockSpec((1, DN_W), lambda bi, i: (0, 0)),
                  pl.BlockSpec((1, DA_W), lambda bi, i: (0, 0)),
                  pl.BlockSpec((DN_W, DN_W), lambda bi, i: (0, 0)),
                  pl.BlockSpec((1, MIX_W, D_MODEL), lambda bi, i: (layer, 0, 0))],
        out_specs=pl.BlockSpec((1, tm, D_MODEL), row_map),
        out_shape=jax.ShapeDtypeStruct((b, t, D_MODEL), F32),
        compiler_params=_params("parallel", "parallel"),
        name="mix",
    )(x, gate, odn, dng, ossm, oda, dag, onorm_row, subln_row, bd, w_out_bf16)


def _final_norm_kernel(x_ref, g_ref, o_ref):
    x = x_ref[0]
    ms = jnp.mean(x * x, axis=-1, keepdims=True)
    o_ref[0] = x * lax.rsqrt(ms + NORM_EPS) * g_ref[...]


def _final_norm_call(x, g):
    b, t, _ = x.shape
    tm = min(1024, t)
    return pl.pallas_call(
        _final_norm_kernel,
        grid=(b, t // tm),
        in_specs=[pl.BlockSpec((1, tm, D_MODEL), lambda bi, i: (bi, i, 0)),
                  pl.BlockSpec((1, D_MODEL), lambda bi, i: (0, 0))],
        out_specs=pl.BlockSpec((1, tm, D_MODEL), lambda bi, i: (bi, i, 0)),
        out_shape=jax.ShapeDtypeStruct((b, t, D_MODEL), F32),
        compiler_params=_params("parallel", "parallel"),
        name="final_norm",
    )(x, g)


def _permute_w_in(w_in):
    splits = (DN_CONV_CH, DN_HEADS, DN_HEADS, DN_W, SSM_W, SSM_W, DA_W, DA_W, DA_W, DA_W)
    offs = [0]
    for n in splits:
        offs.append(offs[-1] + n)
    qkv, dnb, dna, dng, su, sg, q, k, v, dag = [w_in[..., offs[i]:offs[i + 1]] for i in range(10)]
    pad = jnp.zeros(w_in.shape[:-1] + (128 - 2 * DN_HEADS,), w_in.dtype)
    return jnp.concatenate([qkv, dng, su, sg, q, k, v, dag, dnb, dna, pad], axis=-1).astype(BF16)


def _rope_tables(pos):
    half = DA_HD // 2
    inv = jnp.power(ROPE_THETA, -jnp.arange(half, dtype=F32) * 2.0 / DA_HD)
    ang = pos.astype(F32)[:, None] * inv[None, :]
    cos = jnp.tile(jnp.cos(ang), (1, 128 // half))
    sin = jnp.tile(jnp.sin(ang), (1, 128 // half))
    sign = jnp.where((jnp.arange(128) & half) == 0, -1.0, 1.0).astype(F32)
    return cos, sin * sign[None, :]


def _s5_params(lam_re, lam_im, log_dt, b_re, b_im, c_re, c_im, d_skip, w_glu, b_glu):
    g, p, cg = SSM_GROUPS, SSM_P, SSM_GROUP_CH
    lam = lax.complex(lam_re.astype(F32), lam_im.astype(F32))
    dt = jnp.exp(log_dt.astype(F32))[:, None]
    lam_bar = jnp.exp(lam * dt)
    b_bar = ((lam_bar - 1.0) / lam)[..., None] * lax.complex(b_re.astype(F32), b_im.astype(F32))
    eye = jnp.eye(g, dtype=F32)
    b_t = jnp.transpose(b_bar, (0, 2, 1))
    bb_re = jnp.einsum("gcp,gh->gchp", jnp.real(b_t), eye).reshape(g * cg, g * p)
    bb_im = jnp.einsum("gcp,gh->gchp", jnp.imag(b_t), eye).reshape(g * cg, g * p)
    bblk = jnp.concatenate([bb_re, bb_im], axis=1).astype(BF16)
    c_t_re = jnp.transpose(c_re.astype(F32), (0, 2, 1))
    c_t_im = jnp.transpose(c_im.astype(F32), (0, 2, 1))
    cc_re = jnp.einsum("gpc,gh->gphc", c_t_re, eye).reshape(g * p, g * cg)
    cc_im = jnp.einsum("gpc,gh->gphc", c_t_im, eye).reshape(g * p, g * cg)
    cblk = jnp.concatenate([cc_re, -cc_im], axis=0).astype(BF16)
    a = lam_bar.reshape(1, g * p)
    steps = jnp.arange(1, S5_SUB + 1, dtype=F32)[:, None]
    pw = jnp.exp((lam * dt).reshape(1, g * p) * steps)
    ap = pw[S5_SUB - 1:S5_SUB]
    return {"bblk": bblk, "cblk": cblk,
            "ar": jnp.real(a), "ai": jnp.imag(a),
            "apr": jnp.real(ap), "api": jnp.imag(ap),
            "pw": jnp.concatenate([jnp.real(pw), jnp.imag(pw)], axis=1),
            "d": d_skip.astype(F32).reshape(1, SSM_W),
            "wglu": w_glu.astype(BF16), "bglu": b_glu.astype(F32).reshape(1, SSM_W)}


def _lane_row(vals, offset):
    return jnp.zeros((1, 128), F32).at[0, offset:offset + vals.shape[0]].set(vals.astype(F32))


def kernel(x_prompt, x_sample, c_prompt, c_sample, cache_k, cache_v, page_table, state_conv, state_delta, state_ssm_re, state_ssm_im, norm_g, w_ada, b_ada, w_in, conv_w, dn_a_log, dn_dt_bias, dn_onorm, ssm_lam_re, ssm_lam_im, ssm_log_dt, ssm_b_re, ssm_b_im, ssm_c_re, ssm_c_im, ssm_d, ssm_w_glu, ssm_b_glu, da_lam_q1, da_lam_k1, da_lam_q2, da_lam_k2, da_subln, w_out, final_g):
    bp, tp, _ = x_prompt.shape
    bs, ts, _ = x_sample.shape
    n_pages = page_table.shape[1]
    past = n_pages * PAGE_SIZE
    n_pool = cache_k.shape[1]
    rs = bs * ts

    n_c = bp + bs
    c_rows = -(-n_c // 8) * 8
    c_all = jnp.concatenate([c_prompt, c_sample, jnp.zeros((c_rows - n_c, D_MODEL), F32)], axis=0)
    mods = _ada_call(c_all, w_ada, b_ada)

    w_perm = _permute_w_in(w_in)
    w_out_bf = w_out.astype(BF16)
    cos_p, sin_p = _rope_tables(jnp.arange(tp, dtype=jnp.int32))
    cos_s, sin_s = _rope_tables(past + (jnp.arange(rs, dtype=jnp.int32) % ts))
    cache_k4 = jnp.transpose(cache_k, (0, 1, 3, 4, 2)).reshape(DEPTH, n_pool, DA_W, PAGE_SIZE)
    cache_v4 = jnp.transpose(cache_v, (0, 1, 3, 4, 2)).reshape(DEPTH, n_pool, DA_W, PAGE_SIZE)
    conv_w8 = jnp.concatenate([conv_w, jnp.zeros((DEPTH, 8 - CONV_K, DN_CONV_CH), F32)], axis=1)
    zero_buf = jnp.zeros((bp, 8, DN_CONV_CH), F32)
    zero_delta = jnp.zeros((bp, DN_HEADS, DN_DK, DN_DV), F32)
    zero_h = jnp.zeros((bp, 1, SSM_N), F32)
    dn_rows_s = DN_CHUNK

    xp = x_prompt
    xs = x_sample.reshape(1, rs, D_MODEL)
    outs = {k: [] for k in ("kp", "vp", "ks", "vs", "cp", "cs", "dp", "ds", "hrp", "hip", "hrs", "his")}
    for l in range(DEPTH):
        lam_init = 0.8 - 0.6 * math.exp(-0.3 * l)
        lam = (jnp.exp(jnp.sum(da_lam_q1[l].astype(F32) * da_lam_k1[l].astype(F32)))
               - jnp.exp(jnp.sum(da_lam_q2[l].astype(F32) * da_lam_k2[l].astype(F32))) + lam_init).reshape(1, 1)
        g_row = norm_g[l].reshape(1, D_MODEL)
        alog_row = _lane_row(dn_a_log[l], DN_HEADS)
        dtb_row = _lane_row(dn_dt_bias[l], DN_HEADS)
        onorm_row = jnp.tile(dn_onorm[l].astype(F32), DN_HEADS).reshape(1, DN_W)
        subln_row = (jnp.tile(da_subln[l].astype(F32), DA_HEADS) * (1.0 - lam_init)).reshape(1, DA_W)
        sp = _s5_params(ssm_lam_re[l], ssm_lam_im[l], ssm_log_dt[l], ssm_b_re[l], ssm_b_im[l],
                        ssm_c_re[l], ssm_c_im[l], ssm_d[l], ssm_w_glu[l], ssm_b_glu[l])

        mp = mods[l, 0:bp].reshape(bp, 1, 3 * D_MODEL)
        shift, scale, gate = mp[..., 0:D_MODEL], mp[..., D_MODEL:2 * D_MODEL], mp[..., 2 * D_MODEL:]
        qkv, dng, su, sg, dag, sm, qh, kh, vh = _inproj_call(
            xp, scale, shift, g_row, cos_p, sin_p, w_perm, l, True)
        u, w, qd, kd, at, gc = _delta_local_call(qkv, zero_buf, conv_w8[l], sm, alog_row, dtb_row,
                                                 min(DN_ROWS, tp), tp)
        odn, s_fin = _delta_scan_call(u, w, qd, kd, at, gc, zero_delta)
        ossm, hr, hi = _s5_call(su, sg, zero_h, zero_h, sp)
        oda = _attn_prompt_call(qh, kh, vh, lam)
        xp = _mix_call(xp, gate, odn, dng, ossm, oda, dag, onorm_row, subln_row, w_out_bf, l, True)
        outs["kp"].append(jnp.transpose(kh, (0, 3, 1, 2)))
        outs["vp"].append(jnp.transpose(vh, (0, 3, 1, 2)))
        outs["cp"].append(qkv[:, tp - (CONV_K - 1):, :])
        outs["dp"].append(s_fin)
        outs["hrp"].append(hr.reshape(bp, SSM_GROUPS, SSM_P))
        outs["hip"].append(hi.reshape(bp, SSM_GROUPS, SSM_P))

        ms_ = jnp.repeat(mods[l, bp:bp + bs], ts, axis=0).reshape(1, rs, 3 * D_MODEL)
        shift, scale, gate = ms_[..., 0:D_MODEL], ms_[..., D_MODEL:2 * D_MODEL], ms_[..., 2 * D_MODEL:]
        qkv, dng, su, sg, dag, sm, qrow, kf, vf = _inproj_call(
            xs, scale, shift, g_row, cos_s, sin_s, w_perm, l, False)
        qkv_seq = qkv.reshape(bs, ts, DN_CONV_CH)
        pad_t = dn_rows_s - ts
        qkv_pad = jnp.pad(qkv_seq, ((0, 0), (0, pad_t), (0, 0)))
        sm_pad = jnp.pad(sm.reshape(bs, ts, 128), ((0, 0), (0, pad_t), (0, 0)))
        buf8 = jnp.pad(state_conv[l], ((0, 0), (8 - (CONV_K - 1), 0), (0, 0)))
        u, w, qd, kd, at, gc = _delta_local_call(qkv_pad, buf8, conv_w8[l], sm_pad, alog_row, dtb_row,
                                                 dn_rows_s, ts)
        odn, s_fin = _delta_scan_call(u, w, qd, kd, at, gc, state_delta[l])
        odn = jnp.transpose(odn[:, :, 0:ts, :], (0, 2, 1, 3)).reshape(1, rs, DN_W)
        su_t = jnp.transpose(su.reshape(bs, ts, SSM_W), (1, 0, 2))
        sg_t = jnp.transpose(sg.reshape(bs, ts, SSM_W), (1, 0, 2))
        ossm, hr, hi = _s5_step_call(su_t, sg_t, state_ssm_re[l].reshape(bs, SSM_N),
                                     state_ssm_im[l].reshape(bs, SSM_N), sp, bs, ts)
        ossm = jnp.transpose(ossm, (1, 0, 2))
        oda = _attn_sample_call(qrow.reshape(bs, ts, DA_W), kf.reshape(bs, ts, DA_W), vf.reshape(bs, ts, DA_W),
                                cache_k4, cache_v4, page_table, lam, l)
        xs = _mix_call(xs, gate, odn, dng, ossm.reshape(1, rs, SSM_W), oda.reshape(1, rs, DA_W), dag,
                       onorm_row, subln_row, w_out_bf, l, False)
        xp_conv = jnp.concatenate([state_conv[l], qkv_seq], axis=1)
        outs["ks"].append(kf.reshape(bs, ts, DA_HEADS, DA_VD))
        outs["vs"].append(vf.reshape(bs, ts, DA_HEADS, DA_VD))
        outs["cs"].append(xp_conv[:, xp_conv.shape[1] - (CONV_K - 1):, :])
        outs["ds"].append(s_fin)
        outs["hrs"].append(hr.reshape(bs, SSM_GROUPS, SSM_P))
        outs["his"].append(hi.reshape(bs, SSM_GROUPS, SSM_P))

    fg = final_g.reshape(1, D_MODEL)
    y_prompt = _final_norm_call(xp, fg)
    y_sample = _final_norm_call(xs, fg).reshape(bs, ts, D_MODEL)
    st = {k: jnp.stack(v) for k, v in outs.items()}
    return (y_prompt, y_sample, st["kp"], st["vp"], st["ks"], st["vs"], st["cp"], st["cs"],
            st["dp"], st["ds"], st["hrp"], st["hip"], st["hrs"], st["his"])
```

```python
import functools
import math

import jax
import jax.numpy as jnp
from jax import lax
from jax.experimental import pallas as pl
from jax.experimental.pallas import tpu as pltpu

F32 = jnp.float32
BF16 = jnp.bfloat16

D_MODEL = 1024
DEPTH = 4
PAGE_SIZE = 128
DN_HEADS = 6
DN_DK = 64
DN_DV = 64
DN_W = DN_HEADS * DN_DV
DN_CONV_CH = 2 * DN_HEADS * DN_DK + DN_W
CONV_K = 4
DN_CHUNK = 64
SSM_GROUPS = 16
SSM_GROUP_CH = 16
SSM_W = SSM_GROUPS * SSM_GROUP_CH
SSM_P = 64
SSM_N = SSM_GROUPS * SSM_P
DA_HEADS = 6
DA_HD = 32
DA_VD = 2 * DA_HD
DA_W = DA_HEADS * DA_VD
MIX_W = DN_W + SSM_W + DA_W
ROPE_THETA = 10000.0
NORM_EPS = 1e-6
NEG_BIG = -1e30

SEG_QKV = (0, 1152)
SEG_DNG = (1152, 1536)
SEG_SU = (1536, 1792)
SEG_SG = (1792, 2048)
SEG_Q = (2048, 2432)
SEG_K = (2432, 2816)
SEG_V = (2816, 3200)
SEG_DAG = (3200, 3584)
SEG_SM = (3584, 3712)
IN_W_PAD = 3712

TM_PROJ = 512
DN_ROWS = 256
DN_CHUNK_S = 16
S5_ROWS = 512
S5_SUB = 64
S5_SLABS = SSM_N // 128
TQ = 1024
TK_SUB = 512
PAGES_PER_STEP = 16
ATT_PAD = 16
VMEM_LIMIT = 56 * 1024 * 1024

Q_SCALE = (DA_HD ** -0.5) * math.log2(math.e)


def _mm(a, b):
    return jnp.dot(a.astype(BF16), b.astype(BF16), preferred_element_type=F32)


def _mm_nt(a, b):
    return lax.dot_general(a.astype(BF16), b.astype(BF16), (((1,), (1,)), ((), ())),
                           preferred_element_type=F32)


def _mm_tn(a, b):
    return lax.dot_general(a.astype(BF16), b.astype(BF16), (((0,), (0,)), ((), ())),
                           preferred_element_type=F32)


def _split(a):
    hi = a.astype(BF16)
    lo = (a - hi.astype(F32)).astype(BF16)
    return hi, lo


def _mm_split_lhs(a, b_bf16):
    hi, lo = _split(a)
    return (jnp.dot(hi, b_bf16, preferred_element_type=F32)
            + jnp.dot(lo, b_bf16, preferred_element_type=F32))


def _mm3(a, b):
    ah, al = _split(a)
    bh, bl = _split(b)
    return (jnp.dot(ah, bh, preferred_element_type=F32)
            + jnp.dot(ah, bl, preferred_element_type=F32)
            + jnp.dot(al, bh, preferred_element_type=F32))


def _silu(x):
    return x * jax.nn.sigmoid(x)


def _softplus(x):
    return jnp.maximum(x, 0.0) + jnp.log1p(jnp.exp(-jnp.abs(x)))


def _gelu_tanh(x):
    c = math.sqrt(2.0 / math.pi)
    return x * (0.5 * (1.0 + jnp.tanh(c * (x + 0.044715 * (x * x * x)))))


def _params(*sem):
    return pltpu.CompilerParams(dimension_semantics=sem, vmem_limit_bytes=VMEM_LIMIT)


def _ada_kernel(c_ref, w_ref, b_ref, o_ref):
    c = c_ref[...]
    o_ref[0] = _mm3(_silu(c), w_ref[0]) + b_ref[0]


def _ada_call(c_all, w_ada, b_ada):
    rows = c_all.shape[0]
    tn = 1024
    return pl.pallas_call(
        _ada_kernel,
        grid=(DEPTH, 3 * D_MODEL // tn),
        in_specs=[pl.BlockSpec((rows, D_MODEL), lambda l, n: (0, 0)),
                  pl.BlockSpec((1, D_MODEL, tn), lambda l, n: (l, 0, n)),
                  pl.BlockSpec((1, 1, tn), lambda l, n: (l, 0, n))],
        out_specs=pl.BlockSpec((1, rows, tn), lambda l, n: (l, 0, n)),
        out_shape=jax.ShapeDtypeStruct((DEPTH, rows, 3 * D_MODEL), F32),
        compiler_params=_params("parallel", "parallel"),
        name="adaln",
    )(c_all, w_ada, b_ada.reshape(DEPTH, 1, 3 * D_MODEL))


def _rope(x, cos, sin):
    lane = lax.broadcasted_iota(jnp.int32, cos.shape, 1)
    low = (lane & 16) == 0
    outs = []
    for c in range(x.shape[1] // 128):
        xc = x[:, c * 128:(c + 1) * 128]
        sw = jnp.where(low, pltpu.roll(xc, 112, 1), pltpu.roll(xc, 16, 1))
        outs.append(xc * cos + sw * sin)
    return jnp.concatenate(outs, axis=1)


def _inproj_kernel(x_ref, sc_ref, sh_ref, g_ref, cos_ref, sin_ref, w_ref, *outs, head_major):
    x = x_ref[0]
    ms = jnp.mean(x * x, axis=-1, keepdims=True)
    h = x * lax.rsqrt(ms + NORM_EPS) * g_ref[...] * (1.0 + sc_ref[0]) + sh_ref[0]
    hb = h.astype(BF16)

    def seg(ab):
        return jnp.dot(hb, w_ref[0, :, ab[0]:ab[1]], preferred_element_type=F32)

    qkv_o, dng_o, su_o, sg_o, dag_o, sm_o, q_o, k_o, v_o = outs
    qkv_o[0] = seg(SEG_QKV)
    dng_o[0] = seg(SEG_DNG)
    su_o[0] = seg(SEG_SU)
    sg_o[0] = seg(SEG_SG)
    dag_o[0] = seg(SEG_DAG)
    sm_o[0] = seg(SEG_SM)
    cos = cos_ref[...]
    sin = sin_ref[...]
    q = _rope(seg(SEG_Q), cos, sin) * Q_SCALE
    k = _rope(seg(SEG_K), cos, sin)
    v = seg(SEG_V)
    if head_major:
        for hd in range(DA_HEADS):
            sl = slice(hd * DA_VD, (hd + 1) * DA_VD)
            q_o[0, hd] = q[:, sl].T.astype(BF16)
            k_o[0, hd] = k[:, sl].T
            v_o[0, hd] = v[:, sl].T
    else:
        q_o[0] = q
        k_o[0] = k
        v_o[0] = v


def _inproj_call(x, scale, shift, norm_g, cos, sin, w_perm, layer, head_major):
    b, t, _ = x.shape
    tm = min(TM_PROJ, t)
    per_row = scale.shape[1] != 1
    tmm = tm if per_row else 1
    mod_map = (lambda bi, i: (bi, i, 0)) if per_row else (lambda bi, i: (bi, 0, 0))
    row_map = lambda bi, i: (bi, i, 0)

    def row_spec(w):
        return pl.BlockSpec((1, tm, w), row_map)

    def row_shape(w):
        return jax.ShapeDtypeStruct((b, t, w), F32)

    out_specs = [row_spec(1152), row_spec(384), row_spec(256), row_spec(256), row_spec(384), row_spec(128)]
    out_shape = [row_shape(1152), row_shape(384), row_shape(256), row_shape(256), row_shape(384), row_shape(128)]
    if head_major:
        for dt in (BF16, F32, F32):
            out_specs.append(pl.BlockSpec((1, DA_HEADS, DA_VD, tm), lambda bi, i: (bi, 0, 0, i)))
            out_shape.append(jax.ShapeDtypeStruct((b, DA_HEADS, DA_VD, t), dt))
    else:
        out_specs += [row_spec(384)] * 3
        out_shape += [row_shape(384)] * 3
    return pl.pallas_call(
        functools.partial(_inproj_kernel, head_major=head_major),
        grid=(b, t // tm),
        in_specs=[row_spec(D_MODEL),
                  pl.BlockSpec((1, tmm, D_MODEL), mod_map),
                  pl.BlockSpec((1, tmm, D_MODEL), mod_map),
                  pl.BlockSpec((1, D_MODEL), lambda bi, i: (0, 0)),
                  pl.BlockSpec((tm, 128), lambda bi, i: (i, 0)),
                  pl.BlockSpec((tm, 128), lambda bi, i: (i, 0)),
                  pl.BlockSpec((1, D_MODEL, IN_W_PAD), lambda bi, i: (layer, 0, 0))],
        out_specs=out_specs,
        out_shape=out_shape,
        compiler_params=_params("parallel", "parallel"),
        name="inproj",
    )(x, scale, shift, norm_g, cos, sin, w_perm)


def _delta_local_kernel(x_ref, prev_ref, buf_ref, cw_ref, sm_ref, alog_ref, dtb_ref, ltri_ref, last_ref,
                        fold_ref, bd_ref, u_o, w_o, qd_o, kd_o, at_o, gc_o, xs_ref, *, rows, chunk, valid):
    i = pl.program_id(1)
    halo = jnp.where(i == 0, buf_ref[0], prev_ref[0])
    xs_ref[0:8, :] = halo
    xs_ref[8:8 + rows, :] = x_ref[0]
    cw = cw_ref[...]
    y = (xs_ref[pl.ds(5, rows), :] * cw[0:1] + xs_ref[pl.ds(6, rows), :] * cw[1:2]
         + xs_ref[pl.ds(7, rows), :] * cw[2:3] + xs_ref[pl.ds(8, rows), :] * cw[3:4])
    y = _silu(y)
    qk = y[:, 0:2 * DN_W]
    ss = _mm_split_lhs(qk * qk, bd_ref[...])
    qkn = qk * lax.rsqrt(ss + NORM_EPS)
    v_all = y[:, 2 * DN_W:]

    sm = sm_ref[0]
    pos = lax.broadcasted_iota(jnp.int32, sm.shape, 0) & (chunk - 1)
    real = (pos >= valid[0]) & (pos < valid[1])
    beta = jnp.where(real, jax.nn.sigmoid(sm), 0.0)
    g = jnp.where(real, -jnp.exp(alog_ref[...]) * _softplus(sm + dtb_ref[...]), 0.0)
    gc = _mm_split_lhs_rhs(ltri_ref[...], g)
    gc_last = _mm_split_lhs_rhs(last_ref[...], gc)
    gc_o[0] = gc
    gc_t = gc.T

    shift = chunk.bit_length() - 1
    ri = lax.broadcasted_iota(jnp.int32, (rows, rows), 0)
    ci = lax.broadcasted_iota(jnp.int32, (rows, rows), 1)
    same_chunk = (ri >> shift) == (ci >> shift)
    causal = same_chunk & (ri >= ci)
    strict_b = jnp.where(same_chunk & (ri > ci), 1.0, 0.0).astype(BF16)
    eye_f = jnp.where(ri == ci, 1.0, 0.0)
    level_b = []
    s = 1
    while s < chunk:
        sh = s.bit_length()
        m = ((ri >> sh) == (ci >> sh)) & ((ri & s) != 0) & ((ci & s) == 0)
        level_b.append(jnp.where(m, 1.0, 0.0).astype(BF16))
        s *= 2
    scale = DN_DK ** -0.5

    for hd in range(DN_HEADS):
        qh = qkn[:, hd * DN_DK:(hd + 1) * DN_DK] * scale
        kh = qkn[:, DN_W + hd * DN_DK:DN_W + (hd + 1) * DN_DK]
        vh = v_all[:, hd * DN_DV:(hd + 1) * DN_DV]
        beta_c = beta[:, hd:hd + 1]
        gcol = gc[:, DN_HEADS + hd:DN_HEADS + hd + 1]
        grow = gc_t[DN_HEADS + hd:DN_HEADS + hd + 1, :]
        decay = jnp.exp(jnp.where(causal, gcol - grow, NEG_BIG))
        kb = kh * beta_c
        kh_b = kh.astype(BF16)
        m_b = (_mm_nt(kb, kh_b) * decay).astype(BF16) * strict_b
        x_inv = eye_f - (m_b * level_b[0]).astype(F32)
        for lb in level_b[1:]:
            x_b = x_inv.astype(BF16)
            x_inv = x_inv - jnp.dot(jnp.dot(x_b, m_b * lb, preferred_element_type=F32).astype(BF16), x_b,
                                    preferred_element_type=F32)
        e_g = jnp.exp(gcol)
        rhs = jnp.concatenate([vh * beta_c, kb * e_g], axis=1)
        sol = _mm(x_inv, rhs)
        attn_full = _mm_nt(qh, kh_b) * decay
        u_o[0, hd] = sol[:, 0:DN_DV]
        w_o[0, hd] = sol[:, DN_DV:]
        qd_o[0, hd] = qh * e_g
        kd_o[0, hd] = kh * jnp.exp(gc_last[:, DN_HEADS + hd:DN_HEADS + hd + 1] - gcol)
        at_o[0, hd] = _mm(attn_full, fold_ref[...])


def _mm_split_lhs_rhs(a_bf16, b):
    hi, lo = _split(b)
    return (jnp.dot(a_bf16, hi, preferred_element_type=F32)
            + jnp.dot(a_bf16, lo, preferred_element_type=F32))


def _delta_local_call(qkv, buf8, conv_w8, sm, alog_row, dtb_row, rows, chunk, valid):
    b, t, _ = qkv.shape
    nblk = t // rows
    r = jnp.arange(rows)
    same = (r[:, None] // chunk) == (r[None, :] // chunk)
    ltri = ((r[:, None] >= r[None, :]) & same).astype(BF16)
    last = (r[None, :] == (r[:, None] | (chunk - 1))).astype(BF16)
    fold = ((r[:, None] & (chunk - 1)) == jnp.arange(DN_CHUNK)[None, :]).astype(BF16)
    c = jnp.arange(2 * DN_W)
    bd = ((c[:, None] // DN_DK) == (c[None, :] // DN_DK)).astype(BF16)
    hm_spec = pl.BlockSpec((1, DN_HEADS, rows, DN_DV), lambda bi, i: (bi, 0, i, 0))
    hm_shape = jax.ShapeDtypeStruct((b, DN_HEADS, t, DN_DV), F32)
    return pl.pallas_call(
        functools.partial(_delta_local_kernel, rows=rows, chunk=chunk, valid=valid),
        grid=(b, nblk),
        in_specs=[pl.BlockSpec((1, rows, DN_CONV_CH), lambda bi, i: (bi, i, 0)),
                  pl.BlockSpec((1, 8, DN_CONV_CH), lambda bi, i: (bi, jnp.maximum(i * (rows // 8) - 1, 0), 0)),
                  pl.BlockSpec((1, 8, DN_CONV_CH), lambda bi, i: (bi, 0, 0)),
                  pl.BlockSpec((8, DN_CONV_CH), lambda bi, i: (0, 0)),
                  pl.BlockSpec((1, rows, 128), lambda bi, i: (bi, i, 0)),
                  pl.BlockSpec((1, 128), lambda bi, i: (0, 0)),
                  pl.BlockSpec((1, 128), lambda bi, i: (0, 0)),
                  pl.BlockSpec((rows, rows), lambda bi, i: (0, 0)),
                  pl.BlockSpec((rows, rows), lambda bi, i: (0, 0)),
                  pl.BlockSpec((rows, DN_CHUNK), lambda bi, i: (0, 0)),
                  pl.BlockSpec((2 * DN_W, 2 * DN_W), lambda bi, i: (0, 0))],
        out_specs=[hm_spec] * 5 + [pl.BlockSpec((1, rows, 128), lambda bi, i: (bi, i, 0))],
        out_shape=[hm_shape] * 5 + [jax.ShapeDtypeStruct((b, t, 128), F32)],
        scratch_shapes=[pltpu.VMEM((rows + 8, DN_CONV_CH), F32)],
        compiler_params=_params("parallel", "parallel"),
        name="delta_local",
    )(qkv, qkv, buf8, conv_w8, sm, alog_row, dtb_row, ltri, last, fold, bd)


def _delta_scan_kernel(u_ref, w_ref, qd_ref, kd_ref, at_ref, gc_ref, s0_ref, o_ref, sf_ref, s_ref, *, bb):
    c = pl.program_id(1)

    @pl.when(c == 0)
    def _():
        s_ref[...] = s0_ref[...]

    for bi in range(bb):
        for hd in range(DN_HEADS):
            s = s_ref[bi, hd]
            wq = jnp.concatenate([w_ref[bi, hd], qd_ref[bi, hd]], axis=0)
            r = _mm(wq, s)
            v_new = u_ref[bi, hd] - r[0:DN_CHUNK]
            o_ref[bi, hd] = r[DN_CHUNK:] + _mm(at_ref[bi, hd], v_new)
            g_last = jnp.exp(gc_ref[bi, DN_CHUNK - 1:DN_CHUNK, DN_HEADS + hd:DN_HEADS + hd + 1])
            s_ref[bi, hd] = s * g_last + _mm_tn(kd_ref[bi, hd], v_new)

    @pl.when(c == pl.num_programs(1) - 1)
    def _():
        sf_ref[...] = s_ref[...]


def _delta_scan_call(u, w, qd, kd, at, gc, s0):
    b, _, t, _ = u.shape
    bb = 2
    hm_spec = pl.BlockSpec((bb, DN_HEADS, DN_CHUNK, DN_DV), lambda bi, c: (bi, 0, c, 0))
    st_spec = pl.BlockSpec((bb, DN_HEADS, DN_DK, DN_DV), lambda bi, c: (bi, 0, 0, 0))
    return pl.pallas_call(
        functools.partial(_delta_scan_kernel, bb=bb),
        grid=(b // bb, t // DN_CHUNK),
        in_specs=[hm_spec] * 5 + [pl.BlockSpec((bb, DN_CHUNK, 128), lambda bi, c: (bi, c, 0)), st_spec],
        out_specs=[hm_spec, st_spec],
        out_shape=[jax.ShapeDtypeStruct((b, DN_HEADS, t, DN_DV), F32),
                   jax.ShapeDtypeStruct((b, DN_HEADS, DN_DK, DN_DV), F32)],
        scratch_shapes=[pltpu.VMEM((bb, DN_HEADS, DN_DK, DN_DV), F32)],
        compiler_params=_params("parallel", "arbitrary"),
        name="delta_scan",
    )(u, w, qd, kd, at, gc, s0)


def _delta_step_kernel(u_ref, w_ref, qd_ref, kd_ref, at_ref, gc_ref, s0_ref, o_ref, sf_ref, *, bb, chunk):
    for bi in range(bb):
        rows = slice(bi * chunk, (bi + 1) * chunk)
        for hd in range(DN_HEADS):
            s = s0_ref[bi, hd]
            wq = jnp.concatenate([w_ref[0, hd, rows, :], qd_ref[0, hd, rows, :]], axis=0)
            r = _mm(wq, s)
            v_new = u_ref[0, hd, rows, :] - r[0:chunk]
            o_ref[0, hd, rows, :] = r[chunk:] + _mm(at_ref[0, hd, rows, 0:chunk], v_new)
            last = (bi + 1) * chunk - 1
            g_last = jnp.exp(gc_ref[0, last:last + 1, DN_HEADS + hd:DN_HEADS + hd + 1])
            sf_ref[bi, hd] = s * g_last + _mm_tn(kd_ref[0, hd, rows, :], v_new)


def _delta_step_call(u, w, qd, kd, at, gc, s0, chunk):
    nseq = s0.shape[0]
    bb = 4
    hm_spec = pl.BlockSpec((1, DN_HEADS, bb * chunk, DN_DV), lambda i: (0, 0, i, 0))
    st_spec = pl.BlockSpec((bb, DN_HEADS, DN_DK, DN_DV), lambda i: (i, 0, 0, 0))
    return pl.pallas_call(
        functools.partial(_delta_step_kernel, bb=bb, chunk=chunk),
        grid=(nseq // bb,),
        in_specs=[hm_spec] * 5 + [pl.BlockSpec((1, bb * chunk, 128), lambda i: (0, i, 0)), st_spec],
        out_specs=[hm_spec, st_spec],
        out_shape=[jax.ShapeDtypeStruct(u.shape, F32),
                   jax.ShapeDtypeStruct((nseq, DN_HEADS, DN_DK, DN_DV), F32)],
        compiler_params=_params("parallel"),
        name="delta_step",
    )(u, w, qd, kd, at, gc, s0)


def _s5_epilogue(y, u, sg, d_ref, wglu_ref, bglu_ref):
    z = _gelu_tanh(y + d_ref[...] * u)
    gate = jax.nn.sigmoid(_mm(z, wglu_ref[...]) + bglu_ref[...])
    return z * gate * _silu(sg)


def _s5_kernel(u_ref, sg_ref, h0r_ref, h0i_ref, bblk_ref, ar_ref, ai_ref, apr_ref, api_ref, pw_ref,
               cblk_ref, d_ref, wglu_ref, bglu_ref, o_ref, hr_o, hi_o, hs_ref, cr_ref, ci_ref):
    i = pl.program_id(1)
    ns = S5_SLABS

    @pl.when(i == 0)
    def _():
        cr_ref[...] = h0r_ref[0]
        ci_ref[...] = h0i_ref[0]

    def slab(c):
        return slice(c * 128, (c + 1) * 128)

    u = u_ref[0]
    bu = _mm(u, bblk_ref[...])
    for c in range(2 * ns):
        hs_ref[c] = bu[:, slab(c)]
    ar = [jnp.broadcast_to(ar_ref[:, slab(c)], (8, 128)) for c in range(ns)]
    ai = [jnp.broadcast_to(ai_ref[:, slab(c)], (8, 128)) for c in range(ns)]

    def scan_body(j, carry):
        rows = pl.ds(pl.multiple_of(j * 8, 8), 8)
        new = []
        for c in range(ns):
            hr, hi = carry[2 * c], carry[2 * c + 1]
            nr = ar[c] * hr - ai[c] * hi + hs_ref[c, rows, :]
            ni = ar[c] * hi + ai[c] * hr + hs_ref[ns + c, rows, :]
            hs_ref[c, rows, :] = nr
            hs_ref[ns + c, rows, :] = ni
            new += [nr, ni]
        return tuple(new)

    zero = jnp.zeros((8, 128), F32)
    ends = lax.fori_loop(0, S5_SUB, scan_body, (zero,) * (2 * ns))

    h_in = []
    for c in range(ns):
        apr = apr_ref[:, slab(c)]
        api = api_ref[:, slab(c)]
        hr = cr_ref[:, slab(c)]
        hi = ci_ref[:, slab(c)]
        er, ei = ends[2 * c], ends[2 * c + 1]
        rows_r, rows_i = [], []
        for s in range(8):
            rows_r.append(hr)
            rows_i.append(hi)
            nr = apr * hr - api * hi + er[s:s + 1]
            ni = apr * hi + api * hr + ei[s:s + 1]
            hr, hi = nr, ni
        cr_ref[:, slab(c)] = hr
        ci_ref[:, slab(c)] = hi
        h_in += [jnp.concatenate(rows_r, axis=0), jnp.concatenate(rows_i, axis=0)]

    def fix_body(j, carry):
        rows = pl.ds(pl.multiple_of(j * 8, 8), 8)
        for c in range(ns):
            pr = pw_ref[rows, slab(c)]
            pi = pw_ref[rows, slab(ns + c)]
            hr, hi = h_in[2 * c], h_in[2 * c + 1]
            hs_ref[c, rows, :] = hs_ref[c, rows, :] + pr * hr - pi * hi
            hs_ref[ns + c, rows, :] = hs_ref[ns + c, rows, :] + pr * hi + pi * hr
        return carry

    lax.fori_loop(0, S5_SUB, fix_body, 0)

    y = jnp.zeros((S5_ROWS, SSM_W), F32)
    for c in range(2 * ns):
        y = y + _mm(hs_ref[c], cblk_ref[slab(c), :])

    o_ref[0] = _s5_epilogue(y, u, sg_ref[0], d_ref, wglu_ref, bglu_ref)

    @pl.when(i == pl.num_programs(1) - 1)
    def _():
        hr_o[0] = cr_ref[...]
        hi_o[0] = ci_ref[...]


def _s5_call(u, sg, h0r, h0i, sp):
    b, t, _ = u.shape
    n = SSM_N
    row_spec = pl.BlockSpec((1, S5_ROWS, SSM_W), lambda bi, i: (bi, i, 0))
    st_spec = pl.BlockSpec((1, 1, n), lambda bi, i: (bi, 0, 0))
    full = lambda shape: pl.BlockSpec(shape, lambda bi, i: (0,) * len(shape))
    return pl.pallas_call(
        _s5_kernel,
        grid=(b, t // S5_ROWS),
        in_specs=[row_spec, row_spec, st_spec, st_spec,
                  full((SSM_W, 2 * n)), full((1, n)), full((1, n)), full((1, n)), full((1, n)),
                  full((S5_ROWS, 2 * n)), full((2 * n, SSM_W)), full((1, SSM_W)),
                  full((SSM_W, SSM_W)), full((1, SSM_W))],
        out_specs=[row_spec, st_spec, st_spec],
        out_shape=[jax.ShapeDtypeStruct((b, t, SSM_W), F32),
                   jax.ShapeDtypeStruct((b, 1, n), F32), jax.ShapeDtypeStruct((b, 1, n), F32)],
        scratch_shapes=[pltpu.VMEM((2 * S5_SLABS, S5_ROWS, 128), F32), pltpu.VMEM((1, n), F32),
                        pltpu.VMEM((1, n), F32)],
        compiler_params=_params("parallel", "arbitrary"),
        name="s5",
    )(u, sg, h0r, h0i, sp["bblk"], sp["ar"], sp["ai"], sp["apr"], sp["api"], sp["pw"],
      sp["cblk"], sp["d"], sp["wglu"], sp["bglu"])


def _s5_step_kernel(u_ref, sg_ref, h0r_ref, h0i_ref, bblk_ref, ar_ref, ai_ref, cblk_ref, d_ref, wglu_ref,
                    bglu_ref, o_ref, hr_o, hi_o, *, nseq, t):
    hr = h0r_ref[...]
    hi = h0i_ref[...]
    ar = ar_ref[...]
    ai = ai_ref[...]
    n = SSM_N
    for step in range(t):
        u = u_ref[step]
        sg = sg_ref[step]
        bu = _mm(u, bblk_ref[...])
        nr = ar * hr - ai * hi + bu[:, 0:n]
        ni = ar * hi + ai * hr + bu[:, n:]
        hr, hi = nr, ni
        y = _mm(jnp.concatenate([hr, hi], axis=1), cblk_ref[...])
        o_ref[step] = _s5_epilogue(y, u, sg, d_ref, wglu_ref, bglu_ref)
    hr_o[...] = hr
    hi_o[...] = hi


def _s5_step_call(u, sg, h0r, h0i, sp, nseq, t):
    n = SSM_N
    return pl.pallas_call(
        functools.partial(_s5_step_kernel, nseq=nseq, t=t),
        out_shape=[jax.ShapeDtypeStruct((t, nseq, SSM_W), F32),
                   jax.ShapeDtypeStruct((nseq, n), F32), jax.ShapeDtypeStruct((nseq, n), F32)],
        compiler_params=pltpu.CompilerParams(vmem_limit_bytes=VMEM_LIMIT),
        name="s5_step",
    )(u, sg, h0r, h0i, sp["bblk"], sp["ar"], sp["ai"], sp["cblk"], sp["d"], sp["wglu"], sp["bglu"])


def _attn_prompt_kernel(qi_tab, ki_tab, lam_ref, q_ref, k_ref, v_ref, o_ref, qcat_ref, m_ref, acc_ref):
    p = pl.program_id(2)
    qi = qi_tab[p]
    ki = ki_tab[p]
    tq = q_ref.shape[3]

    @pl.when(ki == 0)
    def _():
        q = q_ref[0, 0]
        feat = lax.broadcasted_iota(jnp.int32, q.shape, 0)
        zero = jnp.zeros_like(q)
        qcat_ref[:, 0:tq] = jnp.where(feat < DA_HD, q, zero)
        qcat_ref[:, tq:2 * tq] = jnp.where(feat >= DA_HD, q, zero)
        m_ref[...] = jnp.full(m_ref.shape, NEG_BIG, F32)
        acc_ref[...] = jnp.zeros(acc_ref.shape, F32)

    def step(masked):
        m = m_ref[...]
        acc = acc_ref[...]
        qcat = qcat_ref[...]
        for c in range(k_ref.shape[3] // TK_SUB):
            cols = slice(c * TK_SUB, (c + 1) * TK_SUB)
            s = _mm_tn(k_ref[0, 0, :, cols], qcat)
            if masked:
                kr = lax.broadcasted_iota(jnp.int32, s.shape, 0) + c * TK_SUB
                qc = lax.broadcasted_iota(jnp.int32, s.shape, 1) & (tq - 1)
                s = jnp.where(kr <= qc, s, NEG_BIG)
            m_new = jnp.maximum(m, jnp.max(s, axis=0, keepdims=True))
            alpha = jnp.exp2(m - m_new)
            pm = jnp.exp2(s - m_new).astype(BF16)
            v = v_ref[0, 0, :, cols].astype(BF16)
            v_ext = jnp.concatenate([v, jnp.ones((ATT_PAD, TK_SUB), BF16)], axis=0)
            acc = alpha * acc + jnp.dot(v_ext, pm, preferred_element_type=F32)
            m = m_new
        acc_ref[...] = acc
        m_ref[...] = m

    @pl.when(ki < qi)
    def _():
        step(False)

    @pl.when(ki == qi)
    def _():
        step(True)
        acc = acc_ref[...]
        o1 = acc[0:DA_VD, 0:tq] / acc[DA_VD:DA_VD + 1, 0:tq]
        o2 = acc[0:DA_VD, tq:] / acc[DA_VD:DA_VD + 1, tq:]
        o_ref[0, 0] = (o1 - lam_ref[...] * o2).T


def _attn_prompt_call(qh, kh, vh, lam):
    b, h, _, t = qh.shape
    tq = min(TQ, t)
    nq = t // tq
    qi_tab = jnp.asarray([i for i in range(nq) for _ in range(i + 1)], jnp.int32)
    ki_tab = jnp.asarray([j for i in range(nq) for j in range(i + 1)], jnp.int32)
    grid_spec = pltpu.PrefetchScalarGridSpec(
        num_scalar_prefetch=2,
        grid=(b, h, int(qi_tab.shape[0])),
        in_specs=[pl.BlockSpec((1, 1), lambda bi, hi, p, qt, kt: (0, 0)),
                  pl.BlockSpec((1, 1, DA_VD, tq), lambda bi, hi, p, qt, kt: (bi, hi, 0, qt[p])),
                  pl.BlockSpec((1, 1, DA_VD, tq), lambda bi, hi, p, qt, kt: (bi, hi, 0, kt[p])),
                  pl.BlockSpec((1, 1, DA_VD, tq), lambda bi, hi, p, qt, kt: (bi, hi, 0, kt[p]))],
        out_specs=pl.BlockSpec((1, 1, tq, DA_VD), lambda bi, hi, p, qt, kt: (bi, hi, qt[p], 0)),
        scratch_shapes=[pltpu.VMEM((DA_VD, 2 * tq), BF16), pltpu.VMEM((1, 2 * tq), F32),
                        pltpu.VMEM((DA_VD + ATT_PAD, 2 * tq), F32)])
    return pl.pallas_call(
        _attn_prompt_kernel,
        grid_spec=grid_spec,
        out_shape=jax.ShapeDtypeStruct((b, h, t, DA_VD), F32),
        compiler_params=_params("parallel", "parallel", "arbitrary"),
        name="attn_prompt",
    )(qi_tab, ki_tab, lam, qh, kh, vh)


def _attn_sample_kernel(pt_ref, lam_ref, q_ref, kn_ref, vn_ref, *rest, t_new, n_pages):
    del pt_ref
    pp = PAGES_PER_STEP
    k_refs = rest[0:pp]
    v_refs = rest[pp:2 * pp]
    o_ref, qrows_ref, m_ref, l_ref, acc_ref = rest[2 * pp:]
    j = pl.program_id(1)
    nrow = 2 * t_new * 8

    def update(s, values, mm):
        m_old = m_ref[...]
        m_new = jnp.maximum(m_old, jnp.max(s, axis=1, keepdims=True))
        alpha = jnp.exp2(m_old - m_new)
        pm = jnp.exp2(s - m_new)
        l_ref[...] = alpha * l_ref[...] + jnp.sum(pm, axis=1, keepdims=True)
        acc = alpha * acc_ref[...]
        width = s.shape[1] // len(values)
        for idx, v in enumerate(values):
            acc = acc + mm(pm[:, idx * width:(idx + 1) * width], v)
        acc_ref[...] = acc
        m_ref[...] = m_new

    @pl.when(j == 0)
    def _():
        q = q_ref[0]
        sub = lax.broadcasted_iota(jnp.int32, (8, DA_W), 0)
        lane = lax.broadcasted_iota(jnp.int32, (8, DA_W), 1)
        for mp in range(2):
            keep = ((lane >> 6) == sub) & (((lane >> 5) & 1) == mp)
            for qi in range(t_new):
                r0 = mp * t_new * 8 + qi * 8
                qb = jnp.broadcast_to(q[qi:qi + 1, :], (8, DA_W))
                qrows_ref[r0:r0 + 8, :] = jnp.where(keep, qb, 0.0).astype(BF16)
        m_ref[...] = jnp.full(m_ref.shape, NEG_BIG, F32)
        l_ref[...] = jnp.zeros(l_ref.shape, F32)
        acc_ref[...] = jnp.zeros(acc_ref.shape, F32)
        pad = jnp.zeros((16 - t_new, DA_W), F32)
        k8 = jnp.concatenate([kn_ref[0], pad], axis=0)
        v8 = jnp.concatenate([vn_ref[0], pad], axis=0)
        s = _mm_nt(qrows_ref[...], k8)
        key = lax.broadcasted_iota(jnp.int32, s.shape, 1)
        qidx = (lax.broadcasted_iota(jnp.int32, s.shape, 0) >> 3) & (t_new - 1)
        s = jnp.where(key <= qidx, s, NEG_BIG)
        update(s, [v8], _mm)

    def pairs(refs):
        return [jnp.concatenate([refs[i][...].astype(BF16), refs[i + 1][...].astype(BF16)], axis=1)
                for i in range(0, pp, 2)]

    s_all = jnp.concatenate([_mm(qrows_ref[...], kp) for kp in pairs(k_refs)], axis=1)
    update(s_all, pairs(v_refs), _mm_nt)

    @pl.when(j == n_pages // pp - 1)
    def _():
        o = acc_ref[...] / l_ref[...]
        half = nrow // 2
        oc = o[0:half] - lam_ref[...] * o[half:]
        hd = lax.broadcasted_iota(jnp.int32, oc.shape, 0) & 7
        lane = lax.broadcasted_iota(jnp.int32, oc.shape, 1)
        oc = jnp.where((lane >> 6) == hd, oc, 0.0)
        o_ref[0] = jnp.sum(oc.reshape(t_new, 8, DA_W), axis=1)


def _attn_sample_call(q, k_new, v_new, cache_k, cache_v, page_table, lam, layer):
    nseq, t_new, _ = q.shape
    n_pages = page_table.shape[1]
    pp = PAGES_PER_STEP
    nrow = 2 * t_new * 8
    pt_flat = page_table.reshape(-1).astype(jnp.int32)

    def page_spec(idx):
        return pl.BlockSpec((None, None, DA_W, PAGE_SIZE),
                            lambda bi, j, pt: (layer, pt[bi * n_pages + j * pp + idx], 0, 0))

    tok_spec = pl.BlockSpec((1, t_new, DA_W), lambda bi, j, pt: (bi, 0, 0))
    grid_spec = pltpu.PrefetchScalarGridSpec(
        num_scalar_prefetch=1,
        grid=(nseq, n_pages // pp),
        in_specs=[pl.BlockSpec((1, 1), lambda bi, j, pt: (0, 0)), tok_spec, tok_spec, tok_spec]
        + [page_spec(i) for i in range(pp)] + [page_spec(i) for i in range(pp)],
        out_specs=tok_spec,
        scratch_shapes=[pltpu.VMEM((nrow, DA_W), BF16), pltpu.VMEM((nrow, 1), F32),
                        pltpu.VMEM((nrow, 1), F32), pltpu.VMEM((nrow, DA_W), F32)])
    return pl.pallas_call(
        functools.partial(_attn_sample_kernel, t_new=t_new, n_pages=n_pages),
        grid_spec=grid_spec,
        out_shape=jax.ShapeDtypeStruct((nseq, t_new, DA_W), F32),
        compiler_params=_params("parallel", "arbitrary"),
        name="attn_sample",
    )(pt_flat, lam, q, k_new, v_new, *([cache_k] * pp), *([cache_v] * pp))


def _mix_kernel(x_ref, gate_ref, odn_ref, dng_ref, ossm_ref, oda_ref, dag_ref, onorm_ref, subln_ref, bd_ref,
                w_ref, y_ref, *, head_major):
    if head_major:
        odn = jnp.concatenate([odn_ref[0, hd] for hd in range(DN_HEADS)], axis=1)
        oda = jnp.concatenate([oda_ref[0, hd] for hd in range(DA_HEADS)], axis=1)
    else:
        odn = odn_ref[0]
        oda = oda_ref[0]

    def head_norm(o, gain):
        ms = _mm_split_lhs(o * o, bd_ref[...]) * (1.0 / DN_DV)
        return o * lax.rsqrt(ms + NORM_EPS) * gain

    a = head_norm(odn, onorm_ref[...]) * _silu(dng_ref[0])
    c = head_norm(oda, subln_ref[...]) * _silu(dag_ref[0])
    mixed = (jnp.dot(a.astype(BF16), w_ref[0, 0:DN_W, :], preferred_element_type=F32)
             + jnp.dot(ossm_ref[0].astype(BF16), w_ref[0, DN_W:DN_W + SSM_W, :], preferred_element_type=F32)
             + jnp.dot(c.astype(BF16), w_ref[0, DN_W + SSM_W:, :], preferred_element_type=F32))
    y_ref[0] = x_ref[0] + gate_ref[0] * mixed


def _mix_call(x, gate, odn, dng, ossm, oda, dag, onorm_row, subln_row, w_out_bf16, layer, head_major):
    b, t, _ = x.shape
    tm = min(TM_PROJ, t)
    per_row = gate.shape[1] != 1
    tmm = tm if per_row else 1
    mod_map = (lambda bi, i: (bi, i, 0)) if per_row else (lambda bi, i: (bi, 0, 0))
    row_map = lambda bi, i: (bi, i, 0)
    if head_major:
        o_spec = pl.BlockSpec((1, DN_HEADS, tm, DN_DV), lambda bi, i: (bi, 0, i, 0))
    else:
        o_spec = pl.BlockSpec((1, tm, DN_W), row_map)
    c = jnp.arange(DN_W)
    bd = ((c[:, None] // DN_DV) == (c[None, :] // DN_DV)).astype(BF16)
    return pl.pallas_call(
        functools.partial(_mix_kernel, head_major=head_major),
        grid=(b, t // tm),
        in_specs=[pl.BlockSpec((1, tm, D_MODEL), row_map),
                  pl.BlockSpec((1, tmm, D_MODEL), mod_map),
                  o_spec,
                  pl.BlockSpec((1, tm, DN_W), row_map),
                  pl.BlockSpec((1, tm, SSM_W), row_map),
                  o_spec,
                  pl.BlockSpec((1, tm, DA_W), row_map),
                  pl.BlockSpec((1, DN_W), lambda bi, i: (0, 0)),
                  pl.BlockSpec((1, DA_W), lambda bi, i: (0, 0)),
                  pl.BlockSpec((DN_W, DN_W), lambda bi, i: (0, 0)),
                  pl.BlockSpec((1, MIX_W, D_MODEL), lambda bi, i: (layer, 0, 0))],
        out_specs=pl.BlockSpec((1, tm, D_MODEL), row_map),
        out_shape=jax.ShapeDtypeStruct((b, t, D_MODEL), F32),
        compiler_params=_params("parallel", "parallel"),
        name="mix",
    )(x, gate, odn, dng, ossm, oda, dag, onorm_row, subln_row, bd, w_out_bf16)


def _final_norm_kernel(x_ref, g_ref, o_ref):
    x = x_ref[0]
    ms = jnp.mean(x * x, axis=-1, keepdims=True)
    o_ref[0] = x * lax.rsqrt(ms + NORM_EPS) * g_ref[...]


def _final_norm_call(x, g):
    b, t, _ = x.shape
    tm = min(1024, t)
    return pl.pallas_call(
        _final_norm_kernel,
        grid=(b, t // tm),
        in_specs=[pl.BlockSpec((1, tm, D_MODEL), lambda bi, i: (bi, i, 0)),
                  pl.BlockSpec((1, D_MODEL), lambda bi, i: (0, 0))],
        out_specs=pl.BlockSpec((1, tm, D_MODEL), lambda bi, i: (bi, i, 0)),
        out_shape=jax.ShapeDtypeStruct((b, t, D_MODEL), F32),
        compiler_params=_params("parallel", "parallel"),
        name="final_norm",
    )(x, g)


def _permute_w_in(w_in):
    splits = (DN_CONV_CH, DN_HEADS, DN_HEADS, DN_W, SSM_W, SSM_W, DA_W, DA_W, DA_W, DA_W)
    offs = [0]
    for n in splits:
        offs.append(offs[-1] + n)
    qkv, dnb, dna, dng, su, sg, q, k, v, dag = [w_in[..., offs[i]:offs[i + 1]] for i in range(10)]
    pad = jnp.zeros(w_in.shape[:-1] + (128 - 2 * DN_HEADS,), w_in.dtype)
    return jnp.concatenate([qkv, dng, su, sg, q, k, v, dag, dnb, dna, pad], axis=-1).astype(BF16)


def _rope_tables(pos):
    half = DA_HD // 2
    inv = jnp.power(ROPE_THETA, -jnp.arange(half, dtype=F32) * 2.0 / DA_HD)
    ang = pos.astype(F32)[:, None] * inv[None, :]
    cos = jnp.tile(jnp.cos(ang), (1, 128 // half))
    sin = jnp.tile(jnp.sin(ang), (1, 128 // half))
    sign = jnp.where((jnp.arange(128) & half) == 0, -1.0, 1.0).astype(F32)
    return cos, sin * sign[None, :]


def _s5_params(lam_re, lam_im, log_dt, b_re, b_im, c_re, c_im, d_skip, w_glu, b_glu):
    g, p, cg = SSM_GROUPS, SSM_P, SSM_GROUP_CH
    lam = lax.complex(lam_re.astype(F32), lam_im.astype(F32))
    dt = jnp.exp(log_dt.astype(F32))[:, None]
    lam_bar = jnp.exp(lam * dt)
    b_bar = ((lam_bar - 1.0) / lam)[..., None] * lax.complex(b_re.astype(F32), b_im.astype(F32))
    eye = jnp.eye(g, dtype=F32)
    b_t = jnp.transpose(b_bar, (0, 2, 1))
    bb_re = jnp.einsum("gcp,gh->gchp", jnp.real(b_t), eye).reshape(g * cg, g * p)
    bb_im = jnp.einsum("gcp,gh->gchp", jnp.imag(b_t), eye).reshape(g * cg, g * p)
    bblk = jnp.concatenate([bb_re, bb_im], axis=1).astype(BF16)
    c_t_re = jnp.transpose(c_re.astype(F32), (0, 2, 1))
    c_t_im = jnp.transpose(c_im.astype(F32), (0, 2, 1))
    cc_re = jnp.einsum("gpc,gh->gphc", c_t_re, eye).reshape(g * p, g * cg)
    cc_im = jnp.einsum("gpc,gh->gphc", c_t_im, eye).reshape(g * p, g * cg)
    cblk = jnp.concatenate([cc_re, -cc_im], axis=0).astype(BF16)
    a = lam_bar.reshape(1, g * p)
    steps = jnp.arange(1, S5_SUB + 1, dtype=F32)[:, None]
    pw = jnp.exp((lam * dt).reshape(1, g * p) * steps)
    ap = pw[S5_SUB - 1:S5_SUB]
    return {"bblk": bblk, "cblk": cblk,
            "ar": jnp.real(a), "ai": jnp.imag(a),
            "apr": jnp.real(ap), "api": jnp.imag(ap),
            "pw": jnp.repeat(jnp.concatenate([jnp.real(pw), jnp.imag(pw)], axis=1), 8, axis=0),
            "d": d_skip.astype(F32).reshape(1, SSM_W),
            "wglu": w_glu.astype(BF16), "bglu": b_glu.astype(F32).reshape(1, SSM_W)}


def _s5_block_order(a, outer, inner):
    b, t, w = a.shape
    a = a.reshape(b, t // (outer * inner), outer, inner, w)
    return jnp.transpose(a, (0, 1, 3, 2, 4)).reshape(b, t, w)


def _lane_row(vals, offset):
    return jnp.zeros((1, 128), F32).at[0, offset:offset + vals.shape[0]].set(vals.astype(F32))


def kernel(x_prompt, x_sample, c_prompt, c_sample, cache_k, cache_v, page_table, state_conv, state_delta, state_ssm_re, state_ssm_im, norm_g, w_ada, b_ada, w_in, conv_w, dn_a_log, dn_dt_bias, dn_onorm, ssm_lam_re, ssm_lam_im, ssm_log_dt, ssm_b_re, ssm_b_im, ssm_c_re, ssm_c_im, ssm_d, ssm_w_glu, ssm_b_glu, da_lam_q1, da_lam_k1, da_lam_q2, da_lam_k2, da_subln, w_out, final_g):
    bp, tp, _ = x_prompt.shape
    bs, ts, _ = x_sample.shape
    n_pages = page_table.shape[1]
    past = n_pages * PAGE_SIZE
    n_pool = cache_k.shape[1]
    rs = bs * ts

    n_c = bp + bs
    c_rows = -(-n_c // 8) * 8
    c_all = jnp.concatenate([c_prompt, c_sample, jnp.zeros((c_rows - n_c, D_MODEL), F32)], axis=0)
    mods = _ada_call(c_all, w_ada, b_ada)

    w_perm = _permute_w_in(w_in)
    w_out_bf = w_out.astype(BF16)
    cos_p, sin_p = _rope_tables(jnp.arange(tp, dtype=jnp.int32))
    cos_s, sin_s = _rope_tables(past + (jnp.arange(rs, dtype=jnp.int32) % ts))
    cache_k4 = jnp.transpose(cache_k, (0, 1, 3, 4, 2)).reshape(DEPTH, n_pool, DA_W, PAGE_SIZE)
    cache_v4 = jnp.transpose(cache_v, (0, 1, 3, 4, 2)).reshape(DEPTH, n_pool, DA_W, PAGE_SIZE)
    conv_w8 = jnp.concatenate([conv_w, jnp.zeros((DEPTH, 8 - CONV_K, DN_CONV_CH), F32)], axis=1)
    zero_buf = jnp.zeros((bp, 8, DN_CONV_CH), F32)
    zero_delta = jnp.zeros((bp, DN_HEADS, DN_DK, DN_DV), F32)
    zero_h = jnp.zeros((bp, 1, SSM_N), F32)

    xp = x_prompt
    xs = x_sample.reshape(1, rs, D_MODEL)
    outs = {k: [] for k in ("kp", "vp", "ks", "vs", "cp", "cs", "dp", "ds", "hrp", "hip", "hrs", "his")}
    for l in range(DEPTH):
        lam_init = 0.8 - 0.6 * math.exp(-0.3 * l)
        lam = (jnp.exp(jnp.sum(da_lam_q1[l].astype(F32) * da_lam_k1[l].astype(F32)))
               - jnp.exp(jnp.sum(da_lam_q2[l].astype(F32) * da_lam_k2[l].astype(F32))) + lam_init).reshape(1, 1)
        g_row = norm_g[l].reshape(1, D_MODEL)
        alog_row = _lane_row(dn_a_log[l], DN_HEADS)
        dtb_row = _lane_row(dn_dt_bias[l], DN_HEADS)
        onorm_row = jnp.tile(dn_onorm[l].astype(F32), DN_HEADS).reshape(1, DN_W)
        subln_row = (jnp.tile(da_subln[l].astype(F32), DA_HEADS) * (1.0 - lam_init)).reshape(1, DA_W)
        sp = _s5_params(ssm_lam_re[l], ssm_lam_im[l], ssm_log_dt[l], ssm_b_re[l], ssm_b_im[l],
                        ssm_c_re[l], ssm_c_im[l], ssm_d[l], ssm_w_glu[l], ssm_b_glu[l])

        mp = mods[l, 0:bp].reshape(bp, 1, 3 * D_MODEL)
        shift, scale, gate = mp[..., 0:D_MODEL], mp[..., D_MODEL:2 * D_MODEL], mp[..., 2 * D_MODEL:]
        qkv, dng, su, sg, dag, sm, qh, kh, vh = _inproj_call(
            xp, scale, shift, g_row, cos_p, sin_p, w_perm, l, True)
        u, w, qd, kd, at, gc = _delta_local_call(qkv, zero_buf, conv_w8[l], sm, alog_row, dtb_row,
                                                 min(DN_ROWS, tp), DN_CHUNK, (0, DN_CHUNK))
        odn, s_fin = _delta_scan_call(u, w, qd, kd, at, gc, zero_delta)
        ossm, hr, hi = _s5_call(_s5_block_order(su, 8, S5_SUB), _s5_block_order(sg, 8, S5_SUB),
                                zero_h, zero_h, sp)
        ossm = _s5_block_order(ossm, S5_SUB, 8)
        oda = _attn_prompt_call(qh, kh, vh, lam)
        xp = _mix_call(xp, gate, odn, dng, ossm, oda, dag, onorm_row, subln_row, w_out_bf, l, True)
        outs["kp"].append(jnp.transpose(kh, (0, 3, 1, 2)))
        outs["vp"].append(jnp.transpose(vh, (0, 3, 1, 2)))
        outs["cp"].append(qkv[:, tp - (CONV_K - 1):, :])
        outs["dp"].append(s_fin)
        outs["hrp"].append(hr.reshape(bp, SSM_GROUPS, SSM_P))
        outs["hip"].append(hi.reshape(bp, SSM_GROUPS, SSM_P))

        ms_ = jnp.repeat(mods[l, bp:bp + bs], ts, axis=0).reshape(1, rs, 3 * D_MODEL)
        shift, scale, gate = ms_[..., 0:D_MODEL], ms_[..., D_MODEL:2 * D_MODEL], ms_[..., 2 * D_MODEL:]
        qkv, dng, su, sg, dag, sm, qrow, kf, vf = _inproj_call(
            xs, scale, shift, g_row, cos_s, sin_s, w_perm, l, False)
        qkv_seq = qkv.reshape(bs, ts, DN_CONV_CH)
        nb = CONV_K - 1
        pad_t = DN_CHUNK_S - nb - ts
        qkv_cat = jnp.pad(jnp.concatenate([state_conv[l], qkv_seq], axis=1), ((0, 0), (0, pad_t), (0, 0)))
        sm_cat = jnp.pad(sm.reshape(bs, ts, 128), ((0, 0), (nb, pad_t), (0, 0)))
        u, w, qd, kd, at, gc = _delta_local_call(
            qkv_cat.reshape(1, bs * DN_CHUNK_S, DN_CONV_CH), zero_buf[0:1], conv_w8[l],
            sm_cat.reshape(1, bs * DN_CHUNK_S, 128), alog_row, dtb_row,
            min(DN_ROWS, bs * DN_CHUNK_S), DN_CHUNK_S, (nb, nb + ts))
        odn, s_fin = _delta_step_call(u, w, qd, kd, at, gc, state_delta[l], DN_CHUNK_S)
        odn = odn.reshape(DN_HEADS, bs, DN_CHUNK_S, DN_DV)[:, :, nb:nb + ts, :]
        odn = jnp.transpose(odn, (1, 2, 0, 3)).reshape(1, rs, DN_W)
        su_t = jnp.transpose(su.reshape(bs, ts, SSM_W), (1, 0, 2))
        sg_t = jnp.transpose(sg.reshape(bs, ts, SSM_W), (1, 0, 2))
        ossm, hr, hi = _s5_step_call(su_t, sg_t, state_ssm_re[l].reshape(bs, SSM_N),
                                     state_ssm_im[l].reshape(bs, SSM_N), sp, bs, ts)
        ossm = jnp.transpose(ossm, (1, 0, 2))
        oda = _attn_sample_call(qrow.reshape(bs, ts, DA_W), kf.reshape(bs, ts, DA_W), vf.reshape(bs, ts, DA_W),
                                cache_k4, cache_v4, page_table, lam, l)
        xs = _mix_call(xs, gate, odn, dng, ossm.reshape(1, rs, SSM_W), oda.reshape(1, rs, DA_W), dag,
                       onorm_row, subln_row, w_out_bf, l, False)
        xp_conv = jnp.concatenate([state_conv[l], qkv_seq], axis=1)
        outs["ks"].append(kf.reshape(bs, ts, DA_HEADS, DA_VD))
        outs["vs"].append(vf.reshape(bs, ts, DA_HEADS, DA_VD))
        outs["cs"].append(xp_conv[:, xp_conv.shape[1] - (CONV_K - 1):, :])
        outs["ds"].append(s_fin)
        outs["hrs"].append(hr.reshape(bs, SSM_GROUPS, SSM_P))
        outs["his"].append(hi.reshape(bs, SSM_GROUPS, SSM_P))

    fg = final_g.reshape(1, D_MODEL)
    y_prompt = _final_norm_call(xp, fg)
    y_sample = _final_norm_call(xs, fg).reshape(bs, ts, D_MODEL)
    st = {k: jnp.stack(v) for k, v in outs.items()}
    return (y_prompt, y_sample, st["kp"], st["vp"], st["ks"], st["vs"], st["cp"], st["cs"],
            st["dp"], st["ds"], st["hrp"], st["hip"], st["hrs"], st["his"])
```

```python
import functools
import math

import jax
import jax.numpy as jnp
from jax import lax
from jax.experimental import pallas as pl
from jax.experimental.pallas import tpu as pltpu

F32 = jnp.float32
BF16 = jnp.bfloat16

D_MODEL = 1024
DEPTH = 4
PAGE_SIZE = 128
DN_HEADS = 6
DN_DK = 64
DN_DV = 64
DN_W = DN_HEADS * DN_DV
DN_CONV_CH = 2 * DN_HEADS * DN_DK + DN_W
CONV_K = 4
DN_CHUNK = 64
SSM_GROUPS = 16
SSM_GROUP_CH = 16
SSM_W = SSM_GROUPS * SSM_GROUP_CH
SSM_P = 64
SSM_N = SSM_GROUPS * SSM_P
DA_HEADS = 6
DA_HD = 32
DA_VD = 2 * DA_HD
DA_W = DA_HEADS * DA_VD
MIX_W = DN_W + SSM_W + DA_W
ROPE_THETA = 10000.0
NORM_EPS = 1e-6
NEG_BIG = -1e30

SEG_QKV = (0, 1152)
SEG_DNG = (1152, 1536)
SEG_SU = (1536, 1792)
SEG_SG = (1792, 2048)
SEG_Q = (2048, 2432)
SEG_K = (2432, 2816)
SEG_V = (2816, 3200)
SEG_DAG = (3200, 3584)
SEG_SM = (3584, 3712)
IN_W_PAD = 3712

TM_PROJ = 512
DN_ROWS = 256
DN_CHUNK_S = 16
S5_ROWS = 512
S5_SUB = 64
S5_SLABS = SSM_N // 128
TQ = 2048
TK_SUB = 512
PAGES_PER_STEP = 16
ATT_PAD = 16
VMEM_LIMIT = 56 * 1024 * 1024

Q_SCALE = (DA_HD ** -0.5) * math.log2(math.e)


def _mm(a, b):
    return jnp.dot(a.astype(BF16), b.astype(BF16), preferred_element_type=F32)


def _mm_nt(a, b):
    return lax.dot_general(a.astype(BF16), b.astype(BF16), (((1,), (1,)), ((), ())),
                           preferred_element_type=F32)


def _mm_tn(a, b):
    return lax.dot_general(a.astype(BF16), b.astype(BF16), (((0,), (0,)), ((), ())),
                           preferred_element_type=F32)


def _split(a):
    hi = a.astype(BF16)
    lo = (a - hi.astype(F32)).astype(BF16)
    return hi, lo


def _mm_split_lhs(a, b_bf16):
    hi, lo = _split(a)
    return (jnp.dot(hi, b_bf16, preferred_element_type=F32)
            + jnp.dot(lo, b_bf16, preferred_element_type=F32))


def _mm3(a, b):
    ah, al = _split(a)
    bh, bl = _split(b)
    return (jnp.dot(ah, bh, preferred_element_type=F32)
            + jnp.dot(ah, bl, preferred_element_type=F32)
            + jnp.dot(al, bh, preferred_element_type=F32))


def _silu(x):
    return x * jax.nn.sigmoid(x)


def _softplus(x):
    return jnp.maximum(x, 0.0) + jnp.log1p(jnp.exp(-jnp.abs(x)))


def _gelu_tanh(x):
    c = math.sqrt(2.0 / math.pi)
    return x * (0.5 * (1.0 + jnp.tanh(c * (x + 0.044715 * (x * x * x)))))


def _params(*sem):
    return pltpu.CompilerParams(dimension_semantics=sem, vmem_limit_bytes=VMEM_LIMIT)


def _ada_kernel(c_ref, w_ref, b_ref, o_ref):
    c = c_ref[...]
    o_ref[0] = _mm3(_silu(c), w_ref[0]) + b_ref[0]


def _ada_call(c_all, w_ada, b_ada):
    rows = c_all.shape[0]
    tn = 1024
    return pl.pallas_call(
        _ada_kernel,
        grid=(DEPTH, 3 * D_MODEL // tn),
        in_specs=[pl.BlockSpec((rows, D_MODEL), lambda l, n: (0, 0)),
                  pl.BlockSpec((1, D_MODEL, tn), lambda l, n: (l, 0, n)),
                  pl.BlockSpec((1, 1, tn), lambda l, n: (l, 0, n))],
        out_specs=pl.BlockSpec((1, rows, tn), lambda l, n: (l, 0, n)),
        out_shape=jax.ShapeDtypeStruct((DEPTH, rows, 3 * D_MODEL), F32),
        compiler_params=_params("parallel", "parallel"),
        name="adaln",
    )(c_all, w_ada, b_ada.reshape(DEPTH, 1, 3 * D_MODEL))


def _rope(x, cos, sin):
    lane = lax.broadcasted_iota(jnp.int32, cos.shape, 1)
    low = (lane & 16) == 0
    outs = []
    for c in range(x.shape[1] // 128):
        xc = x[:, c * 128:(c + 1) * 128]
        sw = jnp.where(low, pltpu.roll(xc, 112, 1), pltpu.roll(xc, 16, 1))
        outs.append(xc * cos + sw * sin)
    return jnp.concatenate(outs, axis=1)


def _inproj_kernel(x_ref, sc_ref, sh_ref, g_ref, cos_ref, sin_ref, w_ref, *outs, head_major):
    x = x_ref[0]
    ms = jnp.mean(x * x, axis=-1, keepdims=True)
    h = x * lax.rsqrt(ms + NORM_EPS) * g_ref[...] * (1.0 + sc_ref[0]) + sh_ref[0]
    hb = h.astype(BF16)

    def seg(ab):
        return jnp.dot(hb, w_ref[0, :, ab[0]:ab[1]], preferred_element_type=F32)

    qkv_o, dng_o, su_o, sg_o, dag_o, sm_o, q_o, k_o, v_o = outs
    qkv_o[0] = seg(SEG_QKV)
    dng_o[0] = seg(SEG_DNG)
    su_o[0] = seg(SEG_SU)
    sg_o[0] = seg(SEG_SG)
    dag_o[0] = seg(SEG_DAG)
    sm_o[0] = seg(SEG_SM)
    cos = cos_ref[...]
    sin = sin_ref[...]
    q = _rope(seg(SEG_Q), cos, sin) * Q_SCALE
    k = _rope(seg(SEG_K), cos, sin)
    v = seg(SEG_V)
    if head_major:
        for hd in range(DA_HEADS):
            sl = slice(hd * DA_VD, (hd + 1) * DA_VD)
            q_o[0, hd] = q[:, sl].T.astype(BF16)
            k_o[0, hd] = k[:, sl].T
            v_o[0, hd] = v[:, sl].T
    else:
        q_o[0] = q
        k_o[0] = k
        v_o[0] = v


def _inproj_call(x, scale, shift, norm_g, cos, sin, w_perm, layer, head_major):
    b, t, _ = x.shape
    tm = min(TM_PROJ, t)
    per_row = scale.shape[1] != 1
    tmm = tm if per_row else 1
    mod_map = (lambda bi, i: (bi, i, 0)) if per_row else (lambda bi, i: (bi, 0, 0))
    row_map = lambda bi, i: (bi, i, 0)

    def row_spec(w):
        return pl.BlockSpec((1, tm, w), row_map)

    def row_shape(w):
        return jax.ShapeDtypeStruct((b, t, w), F32)

    out_specs = [row_spec(1152), row_spec(384), row_spec(256), row_spec(256), row_spec(384), row_spec(128)]
    out_shape = [row_shape(1152), row_shape(384), row_shape(256), row_shape(256), row_shape(384), row_shape(128)]
    if head_major:
        for dt in (BF16, F32, F32):
            out_specs.append(pl.BlockSpec((1, DA_HEADS, DA_VD, tm), lambda bi, i: (bi, 0, 0, i)))
            out_shape.append(jax.ShapeDtypeStruct((b, DA_HEADS, DA_VD, t), dt))
    else:
        out_specs += [row_spec(384)] * 3
        out_shape += [row_shape(384)] * 3
    return pl.pallas_call(
        functools.partial(_inproj_kernel, head_major=head_major),
        grid=(b, t // tm),
        in_specs=[row_spec(D_MODEL),
                  pl.BlockSpec((1, tmm, D_MODEL), mod_map),
                  pl.BlockSpec((1, tmm, D_MODEL), mod_map),
                  pl.BlockSpec((1, D_MODEL), lambda bi, i: (0, 0)),
                  pl.BlockSpec((tm, 128), lambda bi, i: (i, 0)),
                  pl.BlockSpec((tm, 128), lambda bi, i: (i, 0)),
                  pl.BlockSpec((1, D_MODEL, IN_W_PAD), lambda bi, i: (layer, 0, 0))],
        out_specs=out_specs,
        out_shape=out_shape,
        compiler_params=_params("parallel", "parallel"),
        name="inproj",
    )(x, scale, shift, norm_g, cos, sin, w_perm)


def _delta_kernel(x_ref, prev_ref, buf_ref, cw_ref, sm_ref, alog_ref, dtb_ref, ltri_ref, last_ref,
                  fold_ref, bd_ref, *rest, rows, chunk, valid, scan):
    if scan:
        s0_ref, o_ref, sf_ref, xs_ref, s_ref = rest
    else:
        u_o, w_o, qd_o, kd_o, at_o, gc_o, xs_ref = rest
    i = pl.program_id(1)

    if scan:
        @pl.when(i == 0)
        def _():
            s_ref[...] = s0_ref[0]
    halo = jnp.where(i == 0, buf_ref[0], prev_ref[0])
    xs_ref[0:8, :] = halo
    xs_ref[8:8 + rows, :] = x_ref[0]
    cw = cw_ref[...]
    y = (xs_ref[pl.ds(5, rows), :] * cw[0:1] + xs_ref[pl.ds(6, rows), :] * cw[1:2]
         + xs_ref[pl.ds(7, rows), :] * cw[2:3] + xs_ref[pl.ds(8, rows), :] * cw[3:4])
    y = _silu(y)
    qk = y[:, 0:2 * DN_W]
    ss = _mm_split_lhs(qk * qk, bd_ref[...])
    qkn = qk * lax.rsqrt(ss + NORM_EPS)
    v_all = y[:, 2 * DN_W:]

    sm = sm_ref[0]
    pos = lax.broadcasted_iota(jnp.int32, sm.shape, 0) & (chunk - 1)
    real = (pos >= valid[0]) & (pos < valid[1])
    beta = jnp.where(real, jax.nn.sigmoid(sm), 0.0)
    g = jnp.where(real, -jnp.exp(alog_ref[...]) * _softplus(sm + dtb_ref[...]), 0.0)
    gc = _mm_split_lhs_rhs(ltri_ref[...], g)
    gc_last = _mm_split_lhs_rhs(last_ref[...], gc)
    if not scan:
        gc_o[0] = gc
    gc_t = gc.T

    shift = chunk.bit_length() - 1
    ri = lax.broadcasted_iota(jnp.int32, (rows, rows), 0)
    ci = lax.broadcasted_iota(jnp.int32, (rows, rows), 1)
    same_chunk = (ri >> shift) == (ci >> shift)
    causal = same_chunk & (ri >= ci)
    strict_b = jnp.where(same_chunk & (ri > ci), 1.0, 0.0).astype(BF16)
    eye_f = jnp.where(ri == ci, 1.0, 0.0)
    level_b = []
    s = 1
    while s < chunk:
        sh = s.bit_length()
        m = ((ri >> sh) == (ci >> sh)) & ((ri & s) != 0) & ((ci & s) == 0)
        level_b.append(jnp.where(m, 1.0, 0.0).astype(BF16))
        s *= 2
    scale = DN_DK ** -0.5
    per_head = []

    for hd in range(DN_HEADS):
        qh = qkn[:, hd * DN_DK:(hd + 1) * DN_DK] * scale
        kh = qkn[:, DN_W + hd * DN_DK:DN_W + (hd + 1) * DN_DK]
        vh = v_all[:, hd * DN_DV:(hd + 1) * DN_DV]
        beta_c = beta[:, hd:hd + 1]
        gcol = gc[:, DN_HEADS + hd:DN_HEADS + hd + 1]
        grow = gc_t[DN_HEADS + hd:DN_HEADS + hd + 1, :]
        decay = jnp.exp(jnp.where(causal, gcol - grow, NEG_BIG))
        kb = kh * beta_c
        kh_b = kh.astype(BF16)
        m_b = (_mm_nt(kb, kh_b) * decay).astype(BF16) * strict_b
        x_inv = eye_f - (m_b * level_b[0]).astype(F32)
        for lb in level_b[1:]:
            x_b = x_inv.astype(BF16)
            x_inv = x_inv - jnp.dot(jnp.dot(x_b, m_b * lb, preferred_element_type=F32).astype(BF16), x_b,
                                    preferred_element_type=F32)
        e_g = jnp.exp(gcol)
        rhs = jnp.concatenate([vh * beta_c, kb * e_g], axis=1)
        sol = _mm(x_inv, rhs)
        attn_full = _mm_nt(qh, kh_b) * decay
        u = sol[:, 0:DN_DV]
        w = sol[:, DN_DV:]
        qd = qh * e_g
        gl_col = gc_last[:, DN_HEADS + hd:DN_HEADS + hd + 1]
        kd = kh * jnp.exp(gl_col - gcol)
        if scan:
            per_head.append((u, w.astype(BF16), qd.astype(BF16), kd.astype(BF16), attn_full.astype(BF16),
                             jnp.exp(gl_col)))
        else:
            u_o[0, hd] = u
            w_o[0, hd] = w
            qd_o[0, hd] = qd
            kd_o[0, hd] = kd
            at_o[0, hd] = _mm(attn_full, fold_ref[...])

    if scan:
        states = [s_ref[hd] for hd in range(DN_HEADS)]
        v_news = [[] for _ in range(DN_HEADS)]
        o_states = [[] for _ in range(DN_HEADS)]
        for c in range(rows // chunk):
            rc = slice(c * chunk, (c + 1) * chunk)
            for hd in range(DN_HEADS):
                u, w_b, qd_b, kd_b, _, e_last = per_head[hd]
                r = _mm(jnp.concatenate([w_b[rc], qd_b[rc]], axis=0), states[hd])
                v_new = u[rc] - r[0:chunk]
                o_states[hd].append(r[chunk:])
                states[hd] = states[hd] * e_last[c * chunk:c * chunk + 1] + _mm_tn(kd_b[rc], v_new)
                v_news[hd].append(v_new)
        for hd in range(DN_HEADS):
            s_ref[hd] = states[hd]
            o_ref[0, hd] = (jnp.concatenate(o_states[hd], axis=0)
                            + _mm(per_head[hd][4], jnp.concatenate(v_news[hd], axis=0)))

        @pl.when(i == pl.num_programs(1) - 1)
        def _():
            sf_ref[0] = s_ref[...]


def _mm_split_lhs_rhs(a_bf16, b):
    hi, lo = _split(b)
    return (jnp.dot(a_bf16, hi, preferred_element_type=F32)
            + jnp.dot(a_bf16, lo, preferred_element_type=F32))


def _delta_call(qkv, buf8, conv_w8, sm, alog_row, dtb_row, rows, chunk, valid, s0=None):
    b, t, _ = qkv.shape
    nblk = t // rows
    scan = s0 is not None
    r = jnp.arange(rows)
    same = (r[:, None] // chunk) == (r[None, :] // chunk)
    ltri = ((r[:, None] >= r[None, :]) & same).astype(BF16)
    last = (r[None, :] == (r[:, None] | (chunk - 1))).astype(BF16)
    fold = ((r[:, None] & (chunk - 1)) == jnp.arange(DN_CHUNK)[None, :]).astype(BF16)
    c = jnp.arange(2 * DN_W)
    bd = ((c[:, None] // DN_DK) == (c[None, :] // DN_DK)).astype(BF16)
    hm_spec = pl.BlockSpec((1, DN_HEADS, rows, DN_DV), lambda bi, i: (bi, 0, i, 0))
    hm_shape = jax.ShapeDtypeStruct((b, DN_HEADS, t, DN_DV), F32)
    st_spec = pl.BlockSpec((1, DN_HEADS, DN_DK, DN_DV), lambda bi, i: (bi, 0, 0, 0))
    in_specs = [pl.BlockSpec((1, rows, DN_CONV_CH), lambda bi, i: (bi, i, 0)),
                pl.BlockSpec((1, 8, DN_CONV_CH), lambda bi, i: (bi, jnp.maximum(i * (rows // 8) - 1, 0), 0)),
                pl.BlockSpec((1, 8, DN_CONV_CH), lambda bi, i: (bi, 0, 0)),
                pl.BlockSpec((8, DN_CONV_CH), lambda bi, i: (0, 0)),
                pl.BlockSpec((1, rows, 128), lambda bi, i: (bi, i, 0)),
                pl.BlockSpec((1, 128), lambda bi, i: (0, 0)),
                pl.BlockSpec((1, 128), lambda bi, i: (0, 0)),
                pl.BlockSpec((rows, rows), lambda bi, i: (0, 0)),
                pl.BlockSpec((rows, rows), lambda bi, i: (0, 0)),
                pl.BlockSpec((rows, DN_CHUNK), lambda bi, i: (0, 0)),
                pl.BlockSpec((2 * DN_W, 2 * DN_W), lambda bi, i: (0, 0))]
    args = [qkv, qkv, buf8, conv_w8, sm, alog_row, dtb_row, ltri, last, fold, bd]
    scratch = [pltpu.VMEM((rows + 8, DN_CONV_CH), F32)]
    if scan:
        in_specs.append(st_spec)
        args.append(s0)
        out_specs = [hm_spec, st_spec]
        out_shape = [hm_shape, jax.ShapeDtypeStruct((b, DN_HEADS, DN_DK, DN_DV), F32)]
        scratch.append(pltpu.VMEM((DN_HEADS, DN_DK, DN_DV), F32))
    else:
        out_specs = [hm_spec] * 5 + [pl.BlockSpec((1, rows, 128), lambda bi, i: (bi, i, 0))]
        out_shape = [hm_shape] * 5 + [jax.ShapeDtypeStruct((b, t, 128), F32)]
    return pl.pallas_call(
        functools.partial(_delta_kernel, rows=rows, chunk=chunk, valid=valid, scan=scan),
        grid=(b, nblk),
        in_specs=in_specs,
        out_specs=out_specs,
        out_shape=out_shape,
        scratch_shapes=scratch,
        compiler_params=_params("parallel", "arbitrary" if scan else "parallel"),
        name="delta_scan" if scan else "delta_local",
    )(*args)


def _delta_step_kernel(u_ref, w_ref, qd_ref, kd_ref, at_ref, gc_ref, s0_ref, o_ref, sf_ref, *, bb, chunk):
    for bi in range(bb):
        rows = slice(bi * chunk, (bi + 1) * chunk)
        for hd in range(DN_HEADS):
            s = s0_ref[bi, hd]
            wq = jnp.concatenate([w_ref[0, hd, rows, :], qd_ref[0, hd, rows, :]], axis=0)
            r = _mm(wq, s)
            v_new = u_ref[0, hd, rows, :] - r[0:chunk]
            o_ref[0, hd, rows, :] = r[chunk:] + _mm(at_ref[0, hd, rows, 0:chunk], v_new)
            last = (bi + 1) * chunk - 1
            g_last = jnp.exp(gc_ref[0, last:last + 1, DN_HEADS + hd:DN_HEADS + hd + 1])
            sf_ref[bi, hd] = s * g_last + _mm_tn(kd_ref[0, hd, rows, :], v_new)


def _delta_step_call(u, w, qd, kd, at, gc, s0, chunk):
    nseq = s0.shape[0]
    bb = 4
    hm_spec = pl.BlockSpec((1, DN_HEADS, bb * chunk, DN_DV), lambda i: (0, 0, i, 0))
    st_spec = pl.BlockSpec((bb, DN_HEADS, DN_DK, DN_DV), lambda i: (i, 0, 0, 0))
    return pl.pallas_call(
        functools.partial(_delta_step_kernel, bb=bb, chunk=chunk),
        grid=(nseq // bb,),
        in_specs=[hm_spec] * 5 + [pl.BlockSpec((1, bb * chunk, 128), lambda i: (0, i, 0)), st_spec],
        out_specs=[hm_spec, st_spec],
        out_shape=[jax.ShapeDtypeStruct(u.shape, F32),
                   jax.ShapeDtypeStruct((nseq, DN_HEADS, DN_DK, DN_DV), F32)],
        compiler_params=_params("parallel"),
        name="delta_step",
    )(u, w, qd, kd, at, gc, s0)


def _s5_epilogue(y, u, sg, d_ref, wglu_ref, bglu_ref):
    z = _gelu_tanh(y + d_ref[...] * u)
    gate = jax.nn.sigmoid(_mm(z, wglu_ref[...]) + bglu_ref[...])
    return z * gate * _silu(sg)


def _s5_kernel(u_ref, sg_ref, h0r_ref, h0i_ref, bblk_ref, ar_ref, ai_ref, apr_ref, api_ref, pw_ref,
               cblk_ref, d_ref, wglu_ref, bglu_ref, o_ref, hr_o, hi_o, hs_ref, cr_ref, ci_ref):
    i = pl.program_id(1)
    ns = S5_SLABS

    @pl.when(i == 0)
    def _():
        cr_ref[...] = h0r_ref[0]
        ci_ref[...] = h0i_ref[0]

    def slab(c):
        return slice(c * 128, (c + 1) * 128)

    u = u_ref[0]
    bu = _mm(u, bblk_ref[...])
    for c in range(2 * ns):
        hs_ref[c] = bu[:, slab(c)]
    ar = [jnp.broadcast_to(ar_ref[:, slab(c)], (8, 128)) for c in range(ns)]
    ai = [jnp.broadcast_to(ai_ref[:, slab(c)], (8, 128)) for c in range(ns)]

    def scan_body(j, carry):
        rows = pl.ds(pl.multiple_of(j * 8, 8), 8)
        new = []
        for c in range(ns):
            hr, hi = carry[2 * c], carry[2 * c + 1]
            nr = ar[c] * hr - ai[c] * hi + hs_ref[c, rows, :]
            ni = ar[c] * hi + ai[c] * hr + hs_ref[ns + c, rows, :]
            hs_ref[c, rows, :] = nr
            hs_ref[ns + c, rows, :] = ni
            new += [nr, ni]
        return tuple(new)

    zero = jnp.zeros((8, 128), F32)
    ends = lax.fori_loop(0, S5_SUB, scan_body, (zero,) * (2 * ns))

    h_in = []
    for c in range(ns):
        apr = apr_ref[:, slab(c)]
        api = api_ref[:, slab(c)]
        hr = cr_ref[:, slab(c)]
        hi = ci_ref[:, slab(c)]
        er, ei = ends[2 * c], ends[2 * c + 1]
        rows_r, rows_i = [], []
        for s in range(8):
            rows_r.append(hr)
            rows_i.append(hi)
            nr = apr * hr - api * hi + er[s:s + 1]
            ni = apr * hi + api * hr + ei[s:s + 1]
            hr, hi = nr, ni
        cr_ref[:, slab(c)] = hr
        ci_ref[:, slab(c)] = hi
        h_in += [jnp.concatenate(rows_r, axis=0), jnp.concatenate(rows_i, axis=0)]

    def fix_body(j, carry):
        rows = pl.ds(pl.multiple_of(j * 8, 8), 8)
        for c in range(ns):
            pr = pw_ref[rows, slab(c)]
            pi = pw_ref[rows, slab(ns + c)]
            hr, hi = h_in[2 * c], h_in[2 * c + 1]
            hs_ref[c, rows, :] = hs_ref[c, rows, :] + pr * hr - pi * hi
            hs_ref[ns + c, rows, :] = hs_ref[ns + c, rows, :] + pr * hi + pi * hr
        return carry

    lax.fori_loop(0, S5_SUB, fix_body, 0)

    y = jnp.zeros((S5_ROWS, SSM_W), F32)
    for c in range(2 * ns):
        y = y + _mm(hs_ref[c], cblk_ref[slab(c), :])

    o_ref[0] = _s5_epilogue(y, u, sg_ref[0], d_ref, wglu_ref, bglu_ref)

    @pl.when(i == pl.num_programs(1) - 1)
    def _():
        hr_o[0] = cr_ref[...]
        hi_o[0] = ci_ref[...]


def _s5_call(u, sg, h0r, h0i, sp):
    b, t, _ = u.shape
    n = SSM_N
    row_spec = pl.BlockSpec((1, S5_ROWS, SSM_W), lambda bi, i: (bi, i, 0))
    st_spec = pl.BlockSpec((1, 1, n), lambda bi, i: (bi, 0, 0))
    full = lambda shape: pl.BlockSpec(shape, lambda bi, i: (0,) * len(shape))
    return pl.pallas_call(
        _s5_kernel,
        grid=(b, t // S5_ROWS),
        in_specs=[row_spec, row_spec, st_spec, st_spec,
                  full((SSM_W, 2 * n)), full((1, n)), full((1, n)), full((1, n)), full((1, n)),
                  full((S5_ROWS, 2 * n)), full((2 * n, SSM_W)), full((1, SSM_W)),
                  full((SSM_W, SSM_W)), full((1, SSM_W))],
        out_specs=[row_spec, st_spec, st_spec],
        out_shape=[jax.ShapeDtypeStruct((b, t, SSM_W), F32),
                   jax.ShapeDtypeStruct((b, 1, n), F32), jax.ShapeDtypeStruct((b, 1, n), F32)],
        scratch_shapes=[pltpu.VMEM((2 * S5_SLABS, S5_ROWS, 128), F32), pltpu.VMEM((1, n), F32),
                        pltpu.VMEM((1, n), F32)],
        compiler_params=_params("parallel", "arbitrary"),
        name="s5",
    )(u, sg, h0r, h0i, sp["bblk"], sp["ar"], sp["ai"], sp["apr"], sp["api"], sp["pw"],
      sp["cblk"], sp["d"], sp["wglu"], sp["bglu"])


def _s5_step_kernel(u_ref, sg_ref, h0r_ref, h0i_ref, bblk_ref, ar_ref, ai_ref, cblk_ref, d_ref, wglu_ref,
                    bglu_ref, o_ref, hr_o, hi_o, *, nseq, t):
    hr = h0r_ref[...]
    hi = h0i_ref[...]
    ar = ar_ref[...]
    ai = ai_ref[...]
    n = SSM_N
    for step in range(t):
        u = u_ref[step]
        sg = sg_ref[step]
        bu = _mm(u, bblk_ref[...])
        nr = ar * hr - ai * hi + bu[:, 0:n]
        ni = ar * hi + ai * hr + bu[:, n:]
        hr, hi = nr, ni
        y = _mm(jnp.concatenate([hr, hi], axis=1), cblk_ref[...])
        o_ref[step] = _s5_epilogue(y, u, sg, d_ref, wglu_ref, bglu_ref)
    hr_o[...] = hr
    hi_o[...] = hi


def _s5_step_call(u, sg, h0r, h0i, sp, nseq, t):
    n = SSM_N
    return pl.pallas_call(
        functools.partial(_s5_step_kernel, nseq=nseq, t=t),
        out_shape=[jax.ShapeDtypeStruct((t, nseq, SSM_W), F32),
                   jax.ShapeDtypeStruct((nseq, n), F32), jax.ShapeDtypeStruct((nseq, n), F32)],
        compiler_params=pltpu.CompilerParams(vmem_limit_bytes=VMEM_LIMIT),
        name="s5_step",
    )(u, sg, h0r, h0i, sp["bblk"], sp["ar"], sp["ai"], sp["cblk"], sp["d"], sp["wglu"], sp["bglu"])


def _attn_prompt_kernel(qi_tab, ki_tab, lam_ref, q_ref, k_ref, v_ref, o_ref, qcat_ref, m_ref, acc_ref):
    p = pl.program_id(2)
    qi = qi_tab[p]
    ki = ki_tab[p]
    tq = q_ref.shape[3]

    sub = TK_SUB
    nsub = tq // sub

    @pl.when(ki == 0)
    def _():
        q = q_ref[0, 0]
        feat = lax.broadcasted_iota(jnp.int32, (DA_VD, sub), 0)
        zero = jnp.zeros((DA_VD, sub), q.dtype)
        for blk in range(nsub):
            qb = q[:, blk * sub:(blk + 1) * sub]
            qcat_ref[:, 2 * blk * sub:(2 * blk + 1) * sub] = jnp.where(feat < DA_HD, qb, zero)
            qcat_ref[:, (2 * blk + 1) * sub:(2 * blk + 2) * sub] = jnp.where(feat >= DA_HD, qb, zero)
        m_ref[...] = jnp.full(m_ref.shape, NEG_BIG, F32)
        acc_ref[...] = jnp.zeros(acc_ref.shape, F32)

    def step(diagonal):
        for c in range(nsub):
            keys = slice(c * sub, (c + 1) * sub)
            cols = slice(2 * c * sub, 2 * tq) if diagonal else slice(0, 2 * tq)
            s = _mm_tn(k_ref[0, 0, :, keys], qcat_ref[:, cols])
            if diagonal:
                kr = lax.broadcasted_iota(jnp.int32, (sub, 2 * sub), 0)
                qc = lax.broadcasted_iota(jnp.int32, (sub, 2 * sub), 1) & (sub - 1)
                own = jnp.where(kr <= qc, s[:, 0:2 * sub], NEG_BIG)
                s = own if c == nsub - 1 else jnp.concatenate([own, s[:, 2 * sub:]], axis=1)
            m = m_ref[:, cols]
            m_new = jnp.maximum(m, jnp.max(s, axis=0, keepdims=True))
            alpha = jnp.exp2(m - m_new)
            pm = jnp.exp2(s - m_new).astype(BF16)
            v = v_ref[0, 0, :, keys].astype(BF16)
            v_ext = jnp.concatenate([v, jnp.ones((ATT_PAD, sub), BF16)], axis=0)
            acc_ref[:, cols] = alpha * acc_ref[:, cols] + jnp.dot(v_ext, pm, preferred_element_type=F32)
            m_ref[:, cols] = m_new

    @pl.when(ki < qi)
    def _():
        step(False)

    @pl.when(ki == qi)
    def _():
        step(True)
        acc = acc_ref[...]
        outs = []
        for blk in range(nsub):
            a1 = acc[:, 2 * blk * sub:(2 * blk + 1) * sub]
            a2 = acc[:, (2 * blk + 1) * sub:(2 * blk + 2) * sub]
            o1 = a1[0:DA_VD] / a1[DA_VD:DA_VD + 1]
            o2 = a2[0:DA_VD] / a2[DA_VD:DA_VD + 1]
            outs.append(o1 - lam_ref[...] * o2)
        o_ref[0, 0] = jnp.concatenate(outs, axis=1).T


def _attn_prompt_call(qh, kh, vh, lam):
    b, h, _, t = qh.shape
    tq = min(TQ, t)
    nq = t // tq
    qi_tab = jnp.asarray([i for i in range(nq) for _ in range(i + 1)], jnp.int32)
    ki_tab = jnp.asarray([j for i in range(nq) for j in range(i + 1)], jnp.int32)
    grid_spec = pltpu.PrefetchScalarGridSpec(
        num_scalar_prefetch=2,
        grid=(b, h, int(qi_tab.shape[0])),
        in_specs=[pl.BlockSpec((1, 1), lambda bi, hi, p, qt, kt: (0, 0)),
                  pl.BlockSpec((1, 1, DA_VD, tq), lambda bi, hi, p, qt, kt: (bi, hi, 0, qt[p])),
                  pl.BlockSpec((1, 1, DA_VD, tq), lambda bi, hi, p, qt, kt: (bi, hi, 0, kt[p])),
                  pl.BlockSpec((1, 1, DA_VD, tq), lambda bi, hi, p, qt, kt: (bi, hi, 0, kt[p]))],
        out_specs=pl.BlockSpec((1, 1, tq, DA_VD), lambda bi, hi, p, qt, kt: (bi, hi, qt[p], 0)),
        scratch_shapes=[pltpu.VMEM((DA_VD, 2 * tq), BF16), pltpu.VMEM((1, 2 * tq), F32),
                        pltpu.VMEM((DA_VD + ATT_PAD, 2 * tq), F32)])
    return pl.pallas_call(
        _attn_prompt_kernel,
        grid_spec=grid_spec,
        out_shape=jax.ShapeDtypeStruct((b, h, t, DA_VD), F32),
        compiler_params=_params("parallel", "parallel", "arbitrary"),
        name="attn_prompt",
    )(qi_tab, ki_tab, lam, qh, kh, vh)


def _attn_sample_kernel(pt_ref, lam_ref, q_ref, kn_ref, vn_ref, *rest, t_new, n_pages):
    del pt_ref
    pp = PAGES_PER_STEP
    k_refs = rest[0:pp]
    v_refs = rest[pp:2 * pp]
    o_ref, qrows_ref, m_ref, l_ref, acc_ref = rest[2 * pp:]
    j = pl.program_id(1)
    nrow = 2 * t_new * 8

    def update(s, values, mm):
        m_old = m_ref[...]
        m_new = jnp.maximum(m_old, jnp.max(s, axis=1, keepdims=True))
        alpha = jnp.exp2(m_old - m_new)
        pm = jnp.exp2(s - m_new)
        l_ref[...] = alpha * l_ref[...] + jnp.sum(pm, axis=1, keepdims=True)
        acc = alpha * acc_ref[...]
        width = s.shape[1] // len(values)
        for idx, v in enumerate(values):
            acc = acc + mm(pm[:, idx * width:(idx + 1) * width], v)
        acc_ref[...] = acc
        m_ref[...] = m_new

    @pl.when(j == 0)
    def _():
        q = q_ref[0]
        sub = lax.broadcasted_iota(jnp.int32, (8, DA_W), 0)
        lane = lax.broadcasted_iota(jnp.int32, (8, DA_W), 1)
        for mp in range(2):
            keep = ((lane >> 6) == sub) & (((lane >> 5) & 1) == mp)
            for qi in range(t_new):
                r0 = mp * t_new * 8 + qi * 8
                qb = jnp.broadcast_to(q[qi:qi + 1, :], (8, DA_W))
                qrows_ref[r0:r0 + 8, :] = jnp.where(keep, qb, 0.0).astype(BF16)
        m_ref[...] = jnp.full(m_ref.shape, NEG_BIG, F32)
        l_ref[...] = jnp.zeros(l_ref.shape, F32)
        acc_ref[...] = jnp.zeros(acc_ref.shape, F32)
        pad = jnp.zeros((16 - t_new, DA_W), F32)
        k8 = jnp.concatenate([kn_ref[0], pad], axis=0)
        v8 = jnp.concatenate([vn_ref[0], pad], axis=0)
        s = _mm_nt(qrows_ref[...], k8)
        key = lax.broadcasted_iota(jnp.int32, s.shape, 1)
        qidx = (lax.broadcasted_iota(jnp.int32, s.shape, 0) >> 3) & (t_new - 1)
        s = jnp.where(key <= qidx, s, NEG_BIG)
        update(s, [v8], _mm)

    def pairs(refs):
        return [jnp.concatenate([refs[i][...].astype(BF16), refs[i + 1][...].astype(BF16)], axis=1)
                for i in range(0, pp, 2)]

    s_all = jnp.concatenate([_mm(qrows_ref[...], kp) for kp in pairs(k_refs)], axis=1)
    update(s_all, pairs(v_refs), _mm_nt)

    @pl.when(j == n_pages // pp - 1)
    def _():
        o = acc_ref[...] / l_ref[...]
        half = nrow // 2
        oc = o[0:half] - lam_ref[...] * o[half:]
        hd = lax.broadcasted_iota(jnp.int32, oc.shape, 0) & 7
        lane = lax.broadcasted_iota(jnp.int32, oc.shape, 1)
        oc = jnp.where((lane >> 6) == hd, oc, 0.0)
        o_ref[0] = jnp.sum(oc.reshape(t_new, 8, DA_W), axis=1)


def _attn_sample_call(q, k_new, v_new, cache_k, cache_v, page_table, lam, layer):
    nseq, t_new, _ = q.shape
    n_pages = page_table.shape[1]
    pp = PAGES_PER_STEP
    nrow = 2 * t_new * 8
    pt_flat = page_table.reshape(-1).astype(jnp.int32)

    def page_spec(idx):
        return pl.BlockSpec((None, None, DA_W, PAGE_SIZE),
                            lambda bi, j, pt: (layer, pt[bi * n_pages + j * pp + idx], 0, 0))

    tok_spec = pl.BlockSpec((1, t_new, DA_W), lambda bi, j, pt: (bi, 0, 0))
    grid_spec = pltpu.PrefetchScalarGridSpec(
        num_scalar_prefetch=1,
        grid=(nseq, n_pages // pp),
        in_specs=[pl.BlockSpec((1, 1), lambda bi, j, pt: (0, 0)), tok_spec, tok_spec, tok_spec]
        + [page_spec(i) for i in range(pp)] + [page_spec(i) for i in range(pp)],
        out_specs=tok_spec,
        scratch_shapes=[pltpu.VMEM((nrow, DA_W), BF16), pltpu.VMEM((nrow, 1), F32),
                        pltpu.VMEM((nrow, 1), F32), pltpu.VMEM((nrow, DA_W), F32)])
    return pl.pallas_call(
        functools.partial(_attn_sample_kernel, t_new=t_new, n_pages=n_pages),
        grid_spec=grid_spec,
        out_shape=jax.ShapeDtypeStruct((nseq, t_new, DA_W), F32),
        compiler_params=_params("parallel", "arbitrary"),
        name="attn_sample",
    )(pt_flat, lam, q, k_new, v_new, *([cache_k] * pp), *([cache_v] * pp))


def _mix_kernel(x_ref, gate_ref, odn_ref, dng_ref, ossm_ref, oda_ref, dag_ref, onorm_ref, subln_ref, bd_ref,
                w_ref, y_ref, *, head_major):
    if head_major:
        odn = jnp.concatenate([odn_ref[0, hd] for hd in range(DN_HEADS)], axis=1)
        oda = jnp.concatenate([oda_ref[0, hd] for hd in range(DA_HEADS)], axis=1)
    else:
        odn = odn_ref[0]
        oda = oda_ref[0]

    def head_norm(o, gain):
        ms = _mm_split_lhs(o * o, bd_ref[...]) * (1.0 / DN_DV)
        return o * lax.rsqrt(ms + NORM_EPS) * gain

    a = head_norm(odn, onorm_ref[...]) * _silu(dng_ref[0])
    c = head_norm(oda, subln_ref[...]) * _silu(dag_ref[0])
    mixed = (jnp.dot(a.astype(BF16), w_ref[0, 0:DN_W, :], preferred_element_type=F32)
             + jnp.dot(ossm_ref[0].astype(BF16), w_ref[0, DN_W:DN_W + SSM_W, :], preferred_element_type=F32)
             + jnp.dot(c.astype(BF16), w_ref[0, DN_W + SSM_W:, :], preferred_element_type=F32))
    y_ref[0] = x_ref[0] + gate_ref[0] * mixed


def _mix_call(x, gate, odn, dng, ossm, oda, dag, onorm_row, subln_row, w_out_bf16, layer, head_major):
    b, t, _ = x.shape
    tm = min(TM_PROJ, t)
    per_row = gate.shape[1] != 1
    tmm = tm if per_row else 1
    mod_map = (lambda bi, i: (bi, i, 0)) if per_row else (lambda bi, i: (bi, 0, 0))
    row_map = lambda bi, i: (bi, i, 0)
    if head_major:
        o_spec = pl.BlockSpec((1, DN_HEADS, tm, DN_DV), lambda bi, i: (bi, 0, i, 0))
    else:
        o_spec = pl.BlockSpec((1, tm, DN_W), row_map)
    c = jnp.arange(DN_W)
    bd = ((c[:, None] // DN_DV) == (c[None, :] // DN_DV)).astype(BF16)
    return pl.pallas_call(
        functools.partial(_mix_kernel, head_major=head_major),
        grid=(b, t // tm),
        in_specs=[pl.BlockSpec((1, tm, D_MODEL), row_map),
                  pl.BlockSpec((1, tmm, D_MODEL), mod_map),
                  o_spec,
                  pl.BlockSpec((1, tm, DN_W), row_map),
                  pl.BlockSpec((1, tm, SSM_W), row_map),
                  o_spec,
                  pl.BlockSpec((1, tm, DA_W), row_map),
                  pl.BlockSpec((1, DN_W), lambda bi, i: (0, 0)),
                  pl.BlockSpec((1, DA_W), lambda bi, i: (0, 0)),
                  pl.BlockSpec((DN_W, DN_W), lambda bi, i: (0, 0)),
                  pl.BlockSpec((1, MIX_W, D_MODEL), lambda bi, i: (layer, 0, 0))],
        out_specs=pl.BlockSpec((1, tm, D_MODEL), row_map),
        out_shape=jax.ShapeDtypeStruct((b, t, D_MODEL), F32),
        compiler_params=_params("parallel", "parallel"),
        name="mix",
    )(x, gate, odn, dng, ossm, oda, dag, onorm_row, subln_row, bd, w_out_bf16)


def _final_norm_kernel(x_ref, g_ref, o_ref):
    x = x_ref[0]
    ms = jnp.mean(x * x, axis=-1, keepdims=True)
    o_ref[0] = x * lax.rsqrt(ms + NORM_EPS) * g_ref[...]


def _final_norm_call(x, g):
    b, t, _ = x.shape
    tm = min(1024, t)
    return pl.pallas_call(
        _final_norm_kernel,
        grid=(b, t // tm),
        in_specs=[pl.BlockSpec((1, tm, D_MODEL), lambda bi, i: (bi, i, 0)),
                  pl.BlockSpec((1, D_MODEL), lambda bi, i: (0, 0))],
        out_specs=pl.BlockSpec((1, tm, D_MODEL), lambda bi, i: (bi, i, 0)),
        out_shape=jax.ShapeDtypeStruct((b, t, D_MODEL), F32),
        compiler_params=_params("parallel", "parallel"),
        name="final_norm",
    )(x, g)


def _permute_w_in(w_in):
    splits = (DN_CONV_CH, DN_HEADS, DN_HEADS, DN_W, SSM_W, SSM_W, DA_W, DA_W, DA_W, DA_W)
    offs = [0]
    for n in splits:
        offs.append(offs[-1] + n)
    qkv, dnb, dna, dng, su, sg, q, k, v, dag = [w_in[..., offs[i]:offs[i + 1]] for i in range(10)]
    pad = jnp.zeros(w_in.shape[:-1] + (128 - 2 * DN_HEADS,), w_in.dtype)
    return jnp.concatenate([qkv, dng, su, sg, q, k, v, dag, dnb, dna, pad], axis=-1).astype(BF16)


def _rope_tables(pos):
    half = DA_HD // 2
    inv = jnp.power(ROPE_THETA, -jnp.arange(half, dtype=F32) * 2.0 / DA_HD)
    ang = pos.astype(F32)[:, None] * inv[None, :]
    cos = jnp.tile(jnp.cos(ang), (1, 128 // half))
    sin = jnp.tile(jnp.sin(ang), (1, 128 // half))
    sign = jnp.where((jnp.arange(128) & half) == 0, -1.0, 1.0).astype(F32)
    return cos, sin * sign[None, :]


def _s5_params(lam_re, lam_im, log_dt, b_re, b_im, c_re, c_im, d_skip, w_glu, b_glu):
    g, p, cg = SSM_GROUPS, SSM_P, SSM_GROUP_CH
    lam = lax.complex(lam_re.astype(F32), lam_im.astype(F32))
    dt = jnp.exp(log_dt.astype(F32))[:, None]
    lam_bar = jnp.exp(lam * dt)
    b_bar = ((lam_bar - 1.0) / lam)[..., None] * lax.complex(b_re.astype(F32), b_im.astype(F32))
    eye = jnp.eye(g, dtype=F32)
    b_t = jnp.transpose(b_bar, (0, 2, 1))
    bb_re = jnp.einsum("gcp,gh->gchp", jnp.real(b_t), eye).reshape(g * cg, g * p)
    bb_im = jnp.einsum("gcp,gh->gchp", jnp.imag(b_t), eye).reshape(g * cg, g * p)
    bblk = jnp.concatenate([bb_re, bb_im], axis=1).astype(BF16)
    c_t_re = jnp.transpose(c_re.astype(F32), (0, 2, 1))
    c_t_im = jnp.transpose(c_im.astype(F32), (0, 2, 1))
    cc_re = jnp.einsum("gpc,gh->gphc", c_t_re, eye).reshape(g * p, g * cg)
    cc_im = jnp.einsum("gpc,gh->gphc", c_t_im, eye).reshape(g * p, g * cg)
    cblk = jnp.concatenate([cc_re, -cc_im], axis=0).astype(BF16)
    a = lam_bar.reshape(1, g * p)
    steps = jnp.arange(1, S5_SUB + 1, dtype=F32)[:, None]
    pw = jnp.exp((lam * dt).reshape(1, g * p) * steps)
    ap = pw[S5_SUB - 1:S5_SUB]
    return {"bblk": bblk, "cblk": cblk,
            "ar": jnp.real(a), "ai": jnp.imag(a),
            "apr": jnp.real(ap), "api": jnp.imag(ap),
            "pw": jnp.repeat(jnp.concatenate([jnp.real(pw), jnp.imag(pw)], axis=1), 8, axis=0),
            "d": d_skip.astype(F32).reshape(1, SSM_W),
            "wglu": w_glu.astype(BF16), "bglu": b_glu.astype(F32).reshape(1, SSM_W)}


def _s5_block_order(a, outer, inner):
    b, t, w = a.shape
    a = a.reshape(b, t // (outer * inner), outer, inner, w)
    return jnp.transpose(a, (0, 1, 3, 2, 4)).reshape(b, t, w)


def _lane_row(vals, offset):
    return jnp.zeros((1, 128), F32).at[0, offset:offset + vals.shape[0]].set(vals.astype(F32))


def kernel(x_prompt, x_sample, c_prompt, c_sample, cache_k, cache_v, page_table, state_conv, state_delta, state_ssm_re, state_ssm_im, norm_g, w_ada, b_ada, w_in, conv_w, dn_a_log, dn_dt_bias, dn_onorm, ssm_lam_re, ssm_lam_im, ssm_log_dt, ssm_b_re, ssm_b_im, ssm_c_re, ssm_c_im, ssm_d, ssm_w_glu, ssm_b_glu, da_lam_q1, da_lam_k1, da_lam_q2, da_lam_k2, da_subln, w_out, final_g):
    bp, tp, _ = x_prompt.shape
    bs, ts, _ = x_sample.shape
    n_pages = page_table.shape[1]
    past = n_pages * PAGE_SIZE
    n_pool = cache_k.shape[1]
    rs = bs * ts

    n_c = bp + bs
    c_rows = -(-n_c // 8) * 8
    c_all = jnp.concatenate([c_prompt, c_sample, jnp.zeros((c_rows - n_c, D_MODEL), F32)], axis=0)
    mods = _ada_call(c_all, w_ada, b_ada)

    w_perm = _permute_w_in(w_in)
    w_out_bf = w_out.astype(BF16)
    cos_p, sin_p = _rope_tables(jnp.arange(tp, dtype=jnp.int32))
    cos_s, sin_s = _rope_tables(past + (jnp.arange(rs, dtype=jnp.int32) % ts))
    cache_k4 = jnp.transpose(cache_k, (0, 1, 3, 4, 2)).reshape(DEPTH, n_pool, DA_W, PAGE_SIZE)
    cache_v4 = jnp.transpose(cache_v, (0, 1, 3, 4, 2)).reshape(DEPTH, n_pool, DA_W, PAGE_SIZE)
    conv_w8 = jnp.concatenate([conv_w, jnp.zeros((DEPTH, 8 - CONV_K, DN_CONV_CH), F32)], axis=1)
    zero_buf = jnp.zeros((bp, 8, DN_CONV_CH), F32)
    zero_delta = jnp.zeros((bp, DN_HEADS, DN_DK, DN_DV), F32)
    zero_h = jnp.zeros((bp, 1, SSM_N), F32)

    xp = x_prompt
    xs = x_sample.reshape(1, rs, D_MODEL)
    outs = {k: [] for k in ("kp", "vp", "ks", "vs", "cp", "cs", "dp", "ds", "hrp", "hip", "hrs", "his")}
    for l in range(DEPTH):
        lam_init = 0.8 - 0.6 * math.exp(-0.3 * l)
        lam = (jnp.exp(jnp.sum(da_lam_q1[l].astype(F32) * da_lam_k1[l].astype(F32)))
               - jnp.exp(jnp.sum(da_lam_q2[l].astype(F32) * da_lam_k2[l].astype(F32))) + lam_init).reshape(1, 1)
        g_row = norm_g[l].reshape(1, D_MODEL)
        alog_row = _lane_row(dn_a_log[l], DN_HEADS)
        dtb_row = _lane_row(dn_dt_bias[l], DN_HEADS)
        onorm_row = jnp.tile(dn_onorm[l].astype(F32), DN_HEADS).reshape(1, DN_W)
        subln_row = (jnp.tile(da_subln[l].astype(F32), DA_HEADS) * (1.0 - lam_init)).reshape(1, DA_W)
        sp = _s5_params(ssm_lam_re[l], ssm_lam_im[l], ssm_log_dt[l], ssm_b_re[l], ssm_b_im[l],
                        ssm_c_re[l], ssm_c_im[l], ssm_d[l], ssm_w_glu[l], ssm_b_glu[l])

        mp = mods[l, 0:bp].reshape(bp, 1, 3 * D_MODEL)
        shift, scale, gate = mp[..., 0:D_MODEL], mp[..., D_MODEL:2 * D_MODEL], mp[..., 2 * D_MODEL:]
        qkv, dng, su, sg, dag, sm, qh, kh, vh = _inproj_call(
            xp, scale, shift, g_row, cos_p, sin_p, w_perm, l, True)
        odn, s_fin = _delta_call(qkv, zero_buf, conv_w8[l], sm, alog_row, dtb_row,
                                 min(DN_ROWS, tp), DN_CHUNK, (0, DN_CHUNK), zero_delta)
        ossm, hr, hi = _s5_call(_s5_block_order(su, 8, S5_SUB), _s5_block_order(sg, 8, S5_SUB),
                                zero_h, zero_h, sp)
        ossm = _s5_block_order(ossm, S5_SUB, 8)
        oda = _attn_prompt_call(qh, kh, vh, lam)
        xp = _mix_call(xp, gate, odn, dng, ossm, oda, dag, onorm_row, subln_row, w_out_bf, l, True)
        outs["kp"].append(jnp.transpose(kh, (0, 3, 1, 2)))
        outs["vp"].append(jnp.transpose(vh, (0, 3, 1, 2)))
        outs["cp"].append(qkv[:, tp - (CONV_K - 1):, :])
        outs["dp"].append(s_fin)
        outs["hrp"].append(hr.reshape(bp, SSM_GROUPS, SSM_P))
        outs["hip"].append(hi.reshape(bp, SSM_GROUPS, SSM_P))

        ms_ = jnp.repeat(mods[l, bp:bp + bs], ts, axis=0).reshape(1, rs, 3 * D_MODEL)
        shift, scale, gate = ms_[..., 0:D_MODEL], ms_[..., D_MODEL:2 * D_MODEL], ms_[..., 2 * D_MODEL:]
        qkv, dng, su, sg, dag, sm, qrow, kf, vf = _inproj_call(
            xs, scale, shift, g_row, cos_s, sin_s, w_perm, l, False)
        qkv_seq = qkv.reshape(bs, ts, DN_CONV_CH)
        nb = CONV_K - 1
        pad_t = DN_CHUNK_S - nb - ts
        qkv_cat = jnp.pad(jnp.concatenate([state_conv[l], qkv_seq], axis=1), ((0, 0), (0, pad_t), (0, 0)))
        sm_cat = jnp.pad(sm.reshape(bs, ts, 128), ((0, 0), (nb, pad_t), (0, 0)))
        u, w, qd, kd, at, gc = _delta_call(
            qkv_cat.reshape(1, bs * DN_CHUNK_S, DN_CONV_CH), zero_buf[0:1], conv_w8[l],
            sm_cat.reshape(1, bs * DN_CHUNK_S, 128), alog_row, dtb_row,
            min(DN_ROWS, bs * DN_CHUNK_S), DN_CHUNK_S, (nb, nb + ts))
        odn, s_fin = _delta_step_call(u, w, qd, kd, at, gc, state_delta[l], DN_CHUNK_S)
        odn = odn.reshape(DN_HEADS, bs, DN_CHUNK_S, DN_DV)[:, :, nb:nb + ts, :]
        odn = jnp.transpose(odn, (1, 2, 0, 3)).reshape(1, rs, DN_W)
        su_t = jnp.transpose(su.reshape(bs, ts, SSM_W), (1, 0, 2))
        sg_t = jnp.transpose(sg.reshape(bs, ts, SSM_W), (1, 0, 2))
        ossm, hr, hi = _s5_step_call(su_t, sg_t, state_ssm_re[l].reshape(bs, SSM_N),
                                     state_ssm_im[l].reshape(bs, SSM_N), sp, bs, ts)
        ossm = jnp.transpose(ossm, (1, 0, 2))
        oda = _attn_sample_call(qrow.reshape(bs, ts, DA_W), kf.reshape(bs, ts, DA_W), vf.reshape(bs, ts, DA_W),
                                cache_k4, cache_v4, page_table, lam, l)
        xs = _mix_call(xs, gate, odn, dng, ossm.reshape(1, rs, SSM_W), oda.reshape(1, rs, DA_W), dag,
                       onorm_row, subln_row, w_out_bf, l, False)
        xp_conv = jnp.concatenate([state_conv[l], qkv_seq], axis=1)
        outs["ks"].append(kf.reshape(bs, ts, DA_HEADS, DA_VD))
        outs["vs"].append(vf.reshape(bs, ts, DA_HEADS, DA_VD))
        outs["cs"].append(xp_conv[:, xp_conv.shape[1] - (CONV_K - 1):, :])
        outs["ds"].append(s_fin)
        outs["hrs"].append(hr.reshape(bs, SSM_GROUPS, SSM_P))
        outs["his"].append(hi.reshape(bs, SSM_GROUPS, SSM_P))

    fg = final_g.reshape(1, D_MODEL)
    y_prompt = _final_norm_call(xp, fg)
    y_sample = _final_norm_call(xs, fg).reshape(bs, ts, D_MODEL)
    st = {k: jnp.stack(v) for k, v in outs.items()}
    return (y_prompt, y_sample, st["kp"], st["vp"], st["ks"], st["vs"], st["cp"], st["cs"],
            st["dp"], st["ds"], st["hrp"], st["hip"], st["hrs"], st["his"])
```

```python
import functools
import math

import jax
import jax.numpy as jnp
from jax import lax
from jax.experimental import pallas as pl
from jax.experimental.pallas import tpu as pltpu

F32 = jnp.float32
BF16 = jnp.bfloat16

D_MODEL = 1024
DEPTH = 4
PAGE_SIZE = 128
DN_HEADS = 6
DN_DK = 64
DN_DV = 64
DN_W = DN_HEADS * DN_DV
DN_CONV_CH = 2 * DN_HEADS * DN_DK + DN_W
CONV_K = 4
DN_CHUNK = 64
SSM_GROUPS = 16
SSM_GROUP_CH = 16
SSM_W = SSM_GROUPS * SSM_GROUP_CH
SSM_P = 64
SSM_N = SSM_GROUPS * SSM_P
DA_HEADS = 6
DA_HD = 32
DA_VD = 2 * DA_HD
DA_W = DA_HEADS * DA_VD
MIX_W = DN_W + SSM_W + DA_W
ROPE_THETA = 10000.0
NORM_EPS = 1e-6
NEG_BIG = -1e30

SEG_QKV = (0, 1152)
SEG_DNG = (1152, 1536)
SEG_SU = (1536, 1792)
SEG_SG = (1792, 2048)
SEG_Q = (2048, 2432)
SEG_K = (2432, 2816)
SEG_V = (2816, 3200)
SEG_DAG = (3200, 3584)
SEG_SM = (3584, 3712)
IN_W_PAD = 3712

TM_PROJ = 512
DN_ROWS = 256
DN_CHUNK_S = 16
S5_ROWS = 512
S5_SUB = 64
S5_SLABS = SSM_N // 128
TQ = 2048
TK_SUB = 512
PAGES_PER_STEP = 16
ATT_PAD = 16
VMEM_LIMIT = 56 * 1024 * 1024

Q_SCALE = (DA_HD ** -0.5) * math.log2(math.e)


def _mm(a, b):
    return jnp.dot(a.astype(BF16), b.astype(BF16), preferred_element_type=F32)


def _mm_nt(a, b):
    return lax.dot_general(a.astype(BF16), b.astype(BF16), (((1,), (1,)), ((), ())),
                           preferred_element_type=F32)


def _mm_tn(a, b):
    return lax.dot_general(a.astype(BF16), b.astype(BF16), (((0,), (0,)), ((), ())),
                           preferred_element_type=F32)


def _split(a):
    hi = a.astype(BF16)
    lo = (a - hi.astype(F32)).astype(BF16)
    return hi, lo


def _mm_split_lhs(a, b_bf16):
    hi, lo = _split(a)
    return (jnp.dot(hi, b_bf16, preferred_element_type=F32)
            + jnp.dot(lo, b_bf16, preferred_element_type=F32))


def _mm3(a, b):
    ah, al = _split(a)
    bh, bl = _split(b)
    return (jnp.dot(ah, bh, preferred_element_type=F32)
            + jnp.dot(ah, bl, preferred_element_type=F32)
            + jnp.dot(al, bh, preferred_element_type=F32))


def _silu(x):
    return x * jax.nn.sigmoid(x)


def _softplus(x):
    return jnp.maximum(x, 0.0) + jnp.log1p(jnp.exp(-jnp.abs(x)))


def _gelu_tanh(x):
    c = math.sqrt(2.0 / math.pi)
    return x * (0.5 * (1.0 + jnp.tanh(c * (x + 0.044715 * (x * x * x)))))


def _params(*sem):
    return pltpu.CompilerParams(dimension_semantics=sem, vmem_limit_bytes=VMEM_LIMIT)


def _ada_kernel(c_ref, w_ref, b_ref, o_ref):
    c = c_ref[...]
    o_ref[0] = _mm3(_silu(c), w_ref[0]) + b_ref[0]


def _ada_call(c_all, w_ada, b_ada):
    rows = c_all.shape[0]
    tn = 1024
    return pl.pallas_call(
        _ada_kernel,
        grid=(DEPTH, 3 * D_MODEL // tn),
        in_specs=[pl.BlockSpec((rows, D_MODEL), lambda l, n: (0, 0)),
                  pl.BlockSpec((1, D_MODEL, tn), lambda l, n: (l, 0, n)),
                  pl.BlockSpec((1, 1, tn), lambda l, n: (l, 0, n))],
        out_specs=pl.BlockSpec((1, rows, tn), lambda l, n: (l, 0, n)),
        out_shape=jax.ShapeDtypeStruct((DEPTH, rows, 3 * D_MODEL), F32),
        compiler_params=_params("parallel", "parallel"),
        name="adaln",
    )(c_all, w_ada, b_ada.reshape(DEPTH, 1, 3 * D_MODEL))


def _rope(x, cos, sin):
    lane = lax.broadcasted_iota(jnp.int32, cos.shape, 1)
    low = (lane & 16) == 0
    outs = []
    for c in range(x.shape[1] // 128):
        xc = x[:, c * 128:(c + 1) * 128]
        sw = jnp.where(low, pltpu.roll(xc, 112, 1), pltpu.roll(xc, 16, 1))
        outs.append(xc * cos + sw * sin)
    return jnp.concatenate(outs, axis=1)


def _inproj_kernel(x_ref, sc_ref, sh_ref, g_ref, cos_ref, sin_ref, w_ref, *outs, head_major):
    x = x_ref[0]
    ms = jnp.mean(x * x, axis=-1, keepdims=True)
    h = x * lax.rsqrt(ms + NORM_EPS) * g_ref[...] * (1.0 + sc_ref[0]) + sh_ref[0]
    hb = h.astype(BF16)

    def seg(ab):
        return jnp.dot(hb, w_ref[0, :, ab[0]:ab[1]], preferred_element_type=F32)

    qkv_o, dng_o, su_o, sg_o, dag_o, sm_o, q_o, k_o, v_o = outs
    qkv_o[0] = seg(SEG_QKV)
    dng_o[0] = seg(SEG_DNG)
    su_o[0] = seg(SEG_SU)
    sg_o[0] = seg(SEG_SG)
    dag_o[0] = seg(SEG_DAG)
    sm_o[0] = seg(SEG_SM)
    cos = cos_ref[...]
    sin = sin_ref[...]
    q = _rope(seg(SEG_Q), cos, sin) * Q_SCALE
    k = _rope(seg(SEG_K), cos, sin)
    v = seg(SEG_V)
    if head_major:
        for hd in range(DA_HEADS):
            sl = slice(hd * DA_VD, (hd + 1) * DA_VD)
            q_o[0, hd] = q[:, sl].T.astype(BF16)
            k_o[0, hd] = k[:, sl].T
            v_o[0, hd] = v[:, sl].T
    else:
        q_o[0] = q
        k_o[0] = k
        v_o[0] = v


def _inproj_call(x, scale, shift, norm_g, cos, sin, w_perm, layer, head_major):
    b, t, _ = x.shape
    tm = min(TM_PROJ, t)
    per_row = scale.shape[1] != 1
    tmm = tm if per_row else 1
    mod_map = (lambda bi, i: (bi, i, 0)) if per_row else (lambda bi, i: (bi, 0, 0))
    row_map = lambda bi, i: (bi, i, 0)

    def row_spec(w):
        return pl.BlockSpec((1, tm, w), row_map)

    def row_shape(w):
        return jax.ShapeDtypeStruct((b, t, w), F32)

    out_specs = [row_spec(1152), row_spec(384), row_spec(256), row_spec(256), row_spec(384), row_spec(128)]
    out_shape = [row_shape(1152), row_shape(384), row_shape(256), row_shape(256), row_shape(384), row_shape(128)]
    if head_major:
        for dt in (BF16, F32, F32):
            out_specs.append(pl.BlockSpec((1, DA_HEADS, DA_VD, tm), lambda bi, i: (bi, 0, 0, i)))
            out_shape.append(jax.ShapeDtypeStruct((b, DA_HEADS, DA_VD, t), dt))
    else:
        out_specs += [row_spec(384)] * 3
        out_shape += [row_shape(384)] * 3
    return pl.pallas_call(
        functools.partial(_inproj_kernel, head_major=head_major),
        grid=(b, t // tm),
        in_specs=[row_spec(D_MODEL),
                  pl.BlockSpec((1, tmm, D_MODEL), mod_map),
                  pl.BlockSpec((1, tmm, D_MODEL), mod_map),
                  pl.BlockSpec((1, D_MODEL), lambda bi, i: (0, 0)),
                  pl.BlockSpec((tm, 128), lambda bi, i: (i, 0)),
                  pl.BlockSpec((tm, 128), lambda bi, i: (i, 0)),
                  pl.BlockSpec((1, D_MODEL, IN_W_PAD), lambda bi, i: (layer, 0, 0))],
        out_specs=out_specs,
        out_shape=out_shape,
        compiler_params=_params("parallel", "parallel"),
        name="inproj",
    )(x, scale, shift, norm_g, cos, sin, w_perm)


def _delta_kernel(x_ref, prev_ref, buf_ref, cw_ref, sm_ref, alog_ref, dtb_ref, ltri_ref, last_ref,
                  fold_ref, bd_ref, *rest, rows, chunk, valid, scan):
    if scan:
        s0_ref, o_ref, sf_ref, xs_ref, s_ref = rest
    else:
        u_o, w_o, qd_o, kd_o, at_o, gc_o, xs_ref = rest
    i = pl.program_id(1)

    if scan:
        @pl.when(i == 0)
        def _():
            s_ref[...] = s0_ref[0]
    halo = jnp.where(i == 0, buf_ref[0], prev_ref[0])
    xs_ref[0:8, :] = halo
    xs_ref[8:8 + rows, :] = x_ref[0]
    cw = cw_ref[...]
    y = (xs_ref[pl.ds(5, rows), :] * cw[0:1] + xs_ref[pl.ds(6, rows), :] * cw[1:2]
         + xs_ref[pl.ds(7, rows), :] * cw[2:3] + xs_ref[pl.ds(8, rows), :] * cw[3:4])
    y = _silu(y)
    qk = y[:, 0:2 * DN_W]
    ss = _mm_split_lhs(qk * qk, bd_ref[...])
    qkn = qk * lax.rsqrt(ss + NORM_EPS)
    v_all = y[:, 2 * DN_W:]

    sm = sm_ref[0]
    pos = lax.broadcasted_iota(jnp.int32, sm.shape, 0) & (chunk - 1)
    real = (pos >= valid[0]) & (pos < valid[1])
    beta = jnp.where(real, jax.nn.sigmoid(sm), 0.0)
    g = jnp.where(real, -jnp.exp(alog_ref[...]) * _softplus(sm + dtb_ref[...]), 0.0)
    gc = _mm_split_lhs_rhs(ltri_ref[...], g)
    gc_last = _mm_split_lhs_rhs(last_ref[...], gc)
    if not scan:
        gc_o[0] = gc
    gc_t = gc.T

    shift = chunk.bit_length() - 1
    ri = lax.broadcasted_iota(jnp.int32, (rows, rows), 0)
    ci = lax.broadcasted_iota(jnp.int32, (rows, rows), 1)
    same_chunk = (ri >> shift) == (ci >> shift)
    causal = same_chunk & (ri >= ci)
    strict_b = jnp.where(same_chunk & (ri > ci), 1.0, 0.0).astype(BF16)
    eye_f = jnp.where(ri == ci, 1.0, 0.0)
    level_b = []
    s = 1
    while s < chunk:
        sh = s.bit_length()
        m = ((ri >> sh) == (ci >> sh)) & ((ri & s) != 0) & ((ci & s) == 0)
        level_b.append(jnp.where(m, 1.0, 0.0).astype(BF16))
        s *= 2
    scale = DN_DK ** -0.5
    per_head = []

    heads = []
    for hd in range(DN_HEADS):
        qh = qkn[:, hd * DN_DK:(hd + 1) * DN_DK] * scale
        kh = qkn[:, DN_W + hd * DN_DK:DN_W + (hd + 1) * DN_DK]
        beta_c = beta[:, hd:hd + 1]
        gcol = gc[:, DN_HEADS + hd:DN_HEADS + hd + 1]
        grow = gc_t[DN_HEADS + hd:DN_HEADS + hd + 1, :]
        decay = jnp.exp(jnp.where(causal, gcol - grow, NEG_BIG))
        kb = kh * beta_c
        kh_b = kh.astype(BF16)
        m_b = (_mm_nt(kb, kh_b) * decay).astype(BF16) * strict_b
        heads.append((qh, kh, kh_b, beta_c, gcol, decay, kb, m_b))
    x_invs = [eye_f - (h[7] * level_b[0]).astype(F32) for h in heads]
    for lb in level_b[1:]:
        for hd in range(DN_HEADS):
            x_b = x_invs[hd].astype(BF16)
            t_b = jnp.dot(x_b, heads[hd][7] * lb, preferred_element_type=F32).astype(BF16)
            x_invs[hd] = x_invs[hd] - jnp.dot(t_b, x_b, preferred_element_type=F32)

    for hd in range(DN_HEADS):
        qh, kh, kh_b, beta_c, gcol, decay, kb, _ = heads[hd]
        x_inv = x_invs[hd]
        vh = v_all[:, hd * DN_DV:(hd + 1) * DN_DV]
        e_g = jnp.exp(gcol)
        rhs = jnp.concatenate([vh * beta_c, kb * e_g], axis=1)
        sol = _mm(x_inv, rhs)
        attn_full = _mm_nt(qh, kh_b) * decay
        u = sol[:, 0:DN_DV]
        w = sol[:, DN_DV:]
        qd = qh * e_g
        gl_col = gc_last[:, DN_HEADS + hd:DN_HEADS + hd + 1]
        kd = kh * jnp.exp(gl_col - gcol)
        if scan:
            per_head.append((u, w.astype(BF16), qd.astype(BF16), kd.astype(BF16), attn_full.astype(BF16),
                             jnp.exp(gl_col)))
        else:
            u_o[0, hd] = u
            w_o[0, hd] = w
            qd_o[0, hd] = qd
            kd_o[0, hd] = kd
            at_o[0, hd] = _mm(attn_full, fold_ref[...])

    if scan:
        states = [s_ref[hd] for hd in range(DN_HEADS)]
        v_news = [[] for _ in range(DN_HEADS)]
        o_states = [[] for _ in range(DN_HEADS)]
        for c in range(rows // chunk):
            rc = slice(c * chunk, (c + 1) * chunk)
            for hd in range(DN_HEADS):
                u, w_b, qd_b, kd_b, _, e_last = per_head[hd]
                r = _mm(jnp.concatenate([w_b[rc], qd_b[rc]], axis=0), states[hd])
                v_new = u[rc] - r[0:chunk]
                o_states[hd].append(r[chunk:])
                states[hd] = states[hd] * e_last[c * chunk:c * chunk + 1] + _mm_tn(kd_b[rc], v_new)
                v_news[hd].append(v_new)
        for hd in range(DN_HEADS):
            s_ref[hd] = states[hd]
            o_ref[0, hd] = (jnp.concatenate(o_states[hd], axis=0)
                            + _mm(per_head[hd][4], jnp.concatenate(v_news[hd], axis=0)))

        @pl.when(i == pl.num_programs(1) - 1)
        def _():
            sf_ref[0] = s_ref[...]


def _mm_split_lhs_rhs(a_bf16, b):
    hi, lo = _split(b)
    return (jnp.dot(a_bf16, hi, preferred_element_type=F32)
            + jnp.dot(a_bf16, lo, preferred_element_type=F32))


def _delta_call(qkv, buf8, conv_w8, sm, alog_row, dtb_row, rows, chunk, valid, s0=None):
    b, t, _ = qkv.shape
    nblk = t // rows
    scan = s0 is not None
    r = jnp.arange(rows)
    same = (r[:, None] // chunk) == (r[None, :] // chunk)
    ltri = ((r[:, None] >= r[None, :]) & same).astype(BF16)
    last = (r[None, :] == (r[:, None] | (chunk - 1))).astype(BF16)
    fold = ((r[:, None] & (chunk - 1)) == jnp.arange(DN_CHUNK)[None, :]).astype(BF16)
    c = jnp.arange(2 * DN_W)
    bd = ((c[:, None] // DN_DK) == (c[None, :] // DN_DK)).astype(BF16)
    hm_spec = pl.BlockSpec((1, DN_HEADS, rows, DN_DV), lambda bi, i: (bi, 0, i, 0))
    hm_shape = jax.ShapeDtypeStruct((b, DN_HEADS, t, DN_DV), F32)
    st_spec = pl.BlockSpec((1, DN_HEADS, DN_DK, DN_DV), lambda bi, i: (bi, 0, 0, 0))
    in_specs = [pl.BlockSpec((1, rows, DN_CONV_CH), lambda bi, i: (bi, i, 0)),
                pl.BlockSpec((1, 8, DN_CONV_CH), lambda bi, i: (bi, jnp.maximum(i * (rows // 8) - 1, 0), 0)),
                pl.BlockSpec((1, 8, DN_CONV_CH), lambda bi, i: (bi, 0, 0)),
                pl.BlockSpec((8, DN_CONV_CH), lambda bi, i: (0, 0)),
                pl.BlockSpec((1, rows, 128), lambda bi, i: (bi, i, 0)),
                pl.BlockSpec((1, 128), lambda bi, i: (0, 0)),
                pl.BlockSpec((1, 128), lambda bi, i: (0, 0)),
                pl.BlockSpec((rows, rows), lambda bi, i: (0, 0)),
                pl.BlockSpec((rows, rows), lambda bi, i: (0, 0)),
                pl.BlockSpec((rows, DN_CHUNK), lambda bi, i: (0, 0)),
                pl.BlockSpec((2 * DN_W, 2 * DN_W), lambda bi, i: (0, 0))]
    args = [qkv, qkv, buf8, conv_w8, sm, alog_row, dtb_row, ltri, last, fold, bd]
    scratch = [pltpu.VMEM((rows + 8, DN_CONV_CH), F32)]
    if scan:
        in_specs.append(st_spec)
        args.append(s0)
        out_specs = [hm_spec, st_spec]
        out_shape = [hm_shape, jax.ShapeDtypeStruct((b, DN_HEADS, DN_DK, DN_DV), F32)]
        scratch.append(pltpu.VMEM((DN_HEADS, DN_DK, DN_DV), F32))
    else:
        out_specs = [hm_spec] * 5 + [pl.BlockSpec((1, rows, 128), lambda bi, i: (bi, i, 0))]
        out_shape = [hm_shape] * 5 + [jax.ShapeDtypeStruct((b, t, 128), F32)]
    return pl.pallas_call(
        functools.partial(_delta_kernel, rows=rows, chunk=chunk, valid=valid, scan=scan),
        grid=(b, nblk),
        in_specs=in_specs,
        out_specs=out_specs,
        out_shape=out_shape,
        scratch_shapes=scratch,
        compiler_params=_params("parallel", "arbitrary" if scan else "parallel"),
        name="delta_scan" if scan else "delta_local",
    )(*args)


def _delta_step_kernel(u_ref, w_ref, qd_ref, kd_ref, at_ref, gc_ref, s0_ref, o_ref, sf_ref, *, bb, chunk):
    for bi in range(bb):
        rows = slice(bi * chunk, (bi + 1) * chunk)
        for hd in range(DN_HEADS):
            s = s0_ref[bi, hd]
            wq = jnp.concatenate([w_ref[0, hd, rows, :], qd_ref[0, hd, rows, :]], axis=0)
            r = _mm(wq, s)
            v_new = u_ref[0, hd, rows, :] - r[0:chunk]
            o_ref[0, hd, rows, :] = r[chunk:] + _mm(at_ref[0, hd, rows, 0:chunk], v_new)
            last = (bi + 1) * chunk - 1
            g_last = jnp.exp(gc_ref[0, last:last + 1, DN_HEADS + hd:DN_HEADS + hd + 1])
            sf_ref[bi, hd] = s * g_last + _mm_tn(kd_ref[0, hd, rows, :], v_new)


def _delta_step_call(u, w, qd, kd, at, gc, s0, chunk):
    nseq = s0.shape[0]
    bb = 4
    hm_spec = pl.BlockSpec((1, DN_HEADS, bb * chunk, DN_DV), lambda i: (0, 0, i, 0))
    st_spec = pl.BlockSpec((bb, DN_HEADS, DN_DK, DN_DV), lambda i: (i, 0, 0, 0))
    return pl.pallas_call(
        functools.partial(_delta_step_kernel, bb=bb, chunk=chunk),
        grid=(nseq // bb,),
        in_specs=[hm_spec] * 5 + [pl.BlockSpec((1, bb * chunk, 128), lambda i: (0, i, 0)), st_spec],
        out_specs=[hm_spec, st_spec],
        out_shape=[jax.ShapeDtypeStruct(u.shape, F32),
                   jax.ShapeDtypeStruct((nseq, DN_HEADS, DN_DK, DN_DV), F32)],
        compiler_params=_params("parallel"),
        name="delta_step",
    )(u, w, qd, kd, at, gc, s0)


def _s5_epilogue(y, u, sg, d_ref, wglu_ref, bglu_ref):
    z = _gelu_tanh(y + d_ref[...] * u)
    gate = jax.nn.sigmoid(_mm(z, wglu_ref[...]) + bglu_ref[...])
    return z * gate * _silu(sg)


def _s5_kernel(u_ref, sg_ref, h0r_ref, h0i_ref, bblk_ref, ar_ref, ai_ref, apr_ref, api_ref, pw_ref,
               cblk_ref, d_ref, wglu_ref, bglu_ref, o_ref, hr_o, hi_o, hs_ref, cr_ref, ci_ref):
    i = pl.program_id(1)
    ns = S5_SLABS

    @pl.when(i == 0)
    def _():
        cr_ref[...] = h0r_ref[0]
        ci_ref[...] = h0i_ref[0]

    def slab(c):
        return slice(c * 128, (c + 1) * 128)

    u = u_ref[0]
    bu = _mm(u, bblk_ref[...])
    for c in range(2 * ns):
        hs_ref[c] = bu[:, slab(c)]
    ar = [jnp.broadcast_to(ar_ref[:, slab(c)], (8, 128)) for c in range(ns)]
    ai = [jnp.broadcast_to(ai_ref[:, slab(c)], (8, 128)) for c in range(ns)]

    def scan_body(j, carry):
        rows = pl.ds(pl.multiple_of(j * 8, 8), 8)
        new = []
        for c in range(ns):
            hr, hi = carry[2 * c], carry[2 * c + 1]
            nr = ar[c] * hr - ai[c] * hi + hs_ref[c, rows, :]
            ni = ar[c] * hi + ai[c] * hr + hs_ref[ns + c, rows, :]
            hs_ref[c, rows, :] = nr
            hs_ref[ns + c, rows, :] = ni
            new += [nr, ni]
        return tuple(new)

    zero = jnp.zeros((8, 128), F32)
    ends = lax.fori_loop(0, S5_SUB, scan_body, (zero,) * (2 * ns))

    h_in = []
    for c in range(ns):
        apr = apr_ref[:, slab(c)]
        api = api_ref[:, slab(c)]
        hr = cr_ref[:, slab(c)]
        hi = ci_ref[:, slab(c)]
        er, ei = ends[2 * c], ends[2 * c + 1]
        rows_r, rows_i = [], []
        for s in range(8):
            rows_r.append(hr)
            rows_i.append(hi)
            nr = apr * hr - api * hi + er[s:s + 1]
            ni = apr * hi + api * hr + ei[s:s + 1]
            hr, hi = nr, ni
        cr_ref[:, slab(c)] = hr
        ci_ref[:, slab(c)] = hi
        h_in += [jnp.concatenate(rows_r, axis=0), jnp.concatenate(rows_i, axis=0)]

    def fix_body(j, carry):
        rows = pl.ds(pl.multiple_of(j * 8, 8), 8)
        for c in range(ns):
            pr = pw_ref[rows, slab(c)]
            pi = pw_ref[rows, slab(ns + c)]
            hr, hi = h_in[2 * c], h_in[2 * c + 1]
            hs_ref[c, rows, :] = hs_ref[c, rows, :] + pr * hr - pi * hi
            hs_ref[ns + c, rows, :] = hs_ref[ns + c, rows, :] + pr * hi + pi * hr
        return carry

    lax.fori_loop(0, S5_SUB, fix_body, 0)

    y = jnp.zeros((S5_ROWS, SSM_W), F32)
    for c in range(2 * ns):
        y = y + _mm(hs_ref[c], cblk_ref[slab(c), :])

    o_ref[0] = _s5_epilogue(y, u, sg_ref[0], d_ref, wglu_ref, bglu_ref)

    @pl.when(i == pl.num_programs(1) - 1)
    def _():
        hr_o[0] = cr_ref[...]
        hi_o[0] = ci_ref[...]


def _s5_call(u, sg, h0r, h0i, sp):
    b, t, _ = u.shape
    n = SSM_N
    row_spec = pl.BlockSpec((1, S5_ROWS, SSM_W), lambda bi, i: (bi, i, 0))
    st_spec = pl.BlockSpec((1, 1, n), lambda bi, i: (bi, 0, 0))
    full = lambda shape: pl.BlockSpec(shape, lambda bi, i: (0,) * len(shape))
    return pl.pallas_call(
        _s5_kernel,
        grid=(b, t // S5_ROWS),
        in_specs=[row_spec, row_spec, st_spec, st_spec,
                  full((SSM_W, 2 * n)), full((1, n)), full((1, n)), full((1, n)), full((1, n)),
                  full((S5_ROWS, 2 * n)), full((2 * n, SSM_W)), full((1, SSM_W)),
                  full((SSM_W, SSM_W)), full((1, SSM_W))],
        out_specs=[row_spec, st_spec, st_spec],
        out_shape=[jax.ShapeDtypeStruct((b, t, SSM_W), F32),
                   jax.ShapeDtypeStruct((b, 1, n), F32), jax.ShapeDtypeStruct((b, 1, n), F32)],
        scratch_shapes=[pltpu.VMEM((2 * S5_SLABS, S5_ROWS, 128), F32), pltpu.VMEM((1, n), F32),
                        pltpu.VMEM((1, n), F32)],
        compiler_params=_params("parallel", "arbitrary"),
        name="s5",
    )(u, sg, h0r, h0i, sp["bblk"], sp["ar"], sp["ai"], sp["apr"], sp["api"], sp["pw"],
      sp["cblk"], sp["d"], sp["wglu"], sp["bglu"])


def _s5_step_kernel(u_ref, sg_ref, h0r_ref, h0i_ref, bblk_ref, ar_ref, ai_ref, cblk_ref, d_ref, wglu_ref,
                    bglu_ref, o_ref, hr_o, hi_o, *, nseq, t):
    hr = h0r_ref[...]
    hi = h0i_ref[...]
    ar = ar_ref[...]
    ai = ai_ref[...]
    n = SSM_N
    for step in range(t):
        u = u_ref[step]
        sg = sg_ref[step]
        bu = _mm(u, bblk_ref[...])
        nr = ar * hr - ai * hi + bu[:, 0:n]
        ni = ar * hi + ai * hr + bu[:, n:]
        hr, hi = nr, ni
        y = _mm(jnp.concatenate([hr, hi], axis=1), cblk_ref[...])
        o_ref[step] = _s5_epilogue(y, u, sg, d_ref, wglu_ref, bglu_ref)
    hr_o[...] = hr
    hi_o[...] = hi


def _s5_step_call(u, sg, h0r, h0i, sp, nseq, t):
    n = SSM_N
    return pl.pallas_call(
        functools.partial(_s5_step_kernel, nseq=nseq, t=t),
        out_shape=[jax.ShapeDtypeStruct((t, nseq, SSM_W), F32),
                   jax.ShapeDtypeStruct((nseq, n), F32), jax.ShapeDtypeStruct((nseq, n), F32)],
        compiler_params=pltpu.CompilerParams(vmem_limit_bytes=VMEM_LIMIT),
        name="s5_step",
    )(u, sg, h0r, h0i, sp["bblk"], sp["ar"], sp["ai"], sp["cblk"], sp["d"], sp["wglu"], sp["bglu"])


def _attn_prompt_kernel(qi_tab, ki_tab, lam_ref, q_ref, k_ref, v_ref, o_ref, qcat_ref, m_ref, acc_ref):
    p = pl.program_id(2)
    qi = qi_tab[p]
    ki = ki_tab[p]
    tq = q_ref.shape[3]

    sub = TK_SUB
    nsub = tq // sub

    @pl.when(ki == 0)
    def _():
        q = q_ref[0, 0]
        feat = lax.broadcasted_iota(jnp.int32, (DA_VD, sub), 0)
        zero = jnp.zeros((DA_VD, sub), q.dtype)
        for blk in range(nsub):
            qb = q[:, blk * sub:(blk + 1) * sub]
            qcat_ref[:, 2 * blk * sub:(2 * blk + 1) * sub] = jnp.where(feat < DA_HD, qb, zero)
            qcat_ref[:, (2 * blk + 1) * sub:(2 * blk + 2) * sub] = jnp.where(feat >= DA_HD, qb, zero)
        m_ref[...] = jnp.full(m_ref.shape, NEG_BIG, F32)
        acc_ref[...] = jnp.zeros(acc_ref.shape, F32)

    def step(diagonal):
        def col_range(c):
            return slice(2 * c * sub, 2 * tq) if diagonal else slice(0, 2 * tq)

        def scores(c):
            return _mm_tn(k_ref[0, 0, :, c * sub:(c + 1) * sub], qcat_ref[:, col_range(c)])

        s_next = scores(0)
        for c in range(nsub):
            keys = slice(c * sub, (c + 1) * sub)
            cols = col_range(c)
            s = s_next
            if c + 1 < nsub:
                s_next = scores(c + 1)
            if diagonal:
                kr = lax.broadcasted_iota(jnp.int32, (sub, 2 * sub), 0)
                qc = lax.broadcasted_iota(jnp.int32, (sub, 2 * sub), 1) & (sub - 1)
                own = jnp.where(kr <= qc, s[:, 0:2 * sub], NEG_BIG)
                s = own if c == nsub - 1 else jnp.concatenate([own, s[:, 2 * sub:]], axis=1)
            m = m_ref[:, cols]
            m_new = jnp.maximum(m, jnp.max(s, axis=0, keepdims=True))
            alpha = jnp.exp2(m - m_new)
            pm = jnp.exp2(s - m_new).astype(BF16)
            v = v_ref[0, 0, :, keys].astype(BF16)
            v_ext = jnp.concatenate([v, jnp.ones((ATT_PAD, sub), BF16)], axis=0)
            acc_ref[:, cols] = alpha * acc_ref[:, cols] + jnp.dot(v_ext, pm, preferred_element_type=F32)
            m_ref[:, cols] = m_new

    @pl.when(ki < qi)
    def _():
        step(False)

    @pl.when(ki == qi)
    def _():
        step(True)
        acc = acc_ref[...]
        outs = []
        for blk in range(nsub):
            a1 = acc[:, 2 * blk * sub:(2 * blk + 1) * sub]
            a2 = acc[:, (2 * blk + 1) * sub:(2 * blk + 2) * sub]
            o1 = a1[0:DA_VD] / a1[DA_VD:DA_VD + 1]
            o2 = a2[0:DA_VD] / a2[DA_VD:DA_VD + 1]
            outs.append(o1 - lam_ref[...] * o2)
        o_ref[0, 0] = jnp.concatenate(outs, axis=1).T


def _attn_prompt_call(qh, kh, vh, lam):
    b, h, _, t = qh.shape
    tq = min(TQ, t)
    nq = t // tq
    qi_tab = jnp.asarray([i for i in range(nq) for _ in range(i + 1)], jnp.int32)
    ki_tab = jnp.asarray([j for i in range(nq) for j in range(i + 1)], jnp.int32)
    grid_spec = pltpu.PrefetchScalarGridSpec(
        num_scalar_prefetch=2,
        grid=(b, h, int(qi_tab.shape[0])),
        in_specs=[pl.BlockSpec((1, 1), lambda bi, hi, p, qt, kt: (0, 0)),
                  pl.BlockSpec((1, 1, DA_VD, tq), lambda bi, hi, p, qt, kt: (bi, hi, 0, qt[p])),
                  pl.BlockSpec((1, 1, DA_VD, tq), lambda bi, hi, p, qt, kt: (bi, hi, 0, kt[p])),
                  pl.BlockSpec((1, 1, DA_VD, tq), lambda bi, hi, p, qt, kt: (bi, hi, 0, kt[p]))],
        out_specs=pl.BlockSpec((1, 1, tq, DA_VD), lambda bi, hi, p, qt, kt: (bi, hi, qt[p], 0)),
        scratch_shapes=[pltpu.VMEM((DA_VD, 2 * tq), BF16), pltpu.VMEM((1, 2 * tq), F32),
                        pltpu.VMEM((DA_VD + ATT_PAD, 2 * tq), F32)])
    return pl.pallas_call(
        _attn_prompt_kernel,
        grid_spec=grid_spec,
        out_shape=jax.ShapeDtypeStruct((b, h, t, DA_VD), F32),
        compiler_params=_params("parallel", "parallel", "arbitrary"),
        name="attn_prompt",
    )(qi_tab, ki_tab, lam, qh, kh, vh)


def _attn_sample_kernel(pt_ref, lam_ref, q_ref, kn_ref, vn_ref, *rest, t_new, n_pages):
    del pt_ref
    pp = PAGES_PER_STEP
    k_refs = rest[0:pp]
    v_refs = rest[pp:2 * pp]
    o_ref, qrows_ref, m_ref, l_ref, acc_ref = rest[2 * pp:]
    j = pl.program_id(1)
    nrow = 2 * t_new * 8

    def update(s, values, mm):
        m_old = m_ref[...]
        m_new = jnp.maximum(m_old, jnp.max(s, axis=1, keepdims=True))
        alpha = jnp.exp2(m_old - m_new)
        pm = jnp.exp2(s - m_new)
        l_ref[...] = alpha * l_ref[...] + jnp.sum(pm, axis=1, keepdims=True)
        acc = alpha * acc_ref[...]
        width = s.shape[1] // len(values)
        for idx, v in enumerate(values):
            acc = acc + mm(pm[:, idx * width:(idx + 1) * width], v)
        acc_ref[...] = acc
        m_ref[...] = m_new

    @pl.when(j == 0)
    def _():
        q = q_ref[0]
        sub = lax.broadcasted_iota(jnp.int32, (8, DA_W), 0)
        lane = lax.broadcasted_iota(jnp.int32, (8, DA_W), 1)
        for mp in range(2):
            keep = ((lane >> 6) == sub) & (((lane >> 5) & 1) == mp)
            for qi in range(t_new):
                r0 = mp * t_new * 8 + qi * 8
                qb = jnp.broadcast_to(q[qi:qi + 1, :], (8, DA_W))
                qrows_ref[r0:r0 + 8, :] = jnp.where(keep, qb, 0.0).astype(BF16)
        m_ref[...] = jnp.full(m_ref.shape, NEG_BIG, F32)
        l_ref[...] = jnp.zeros(l_ref.shape, F32)
        acc_ref[...] = jnp.zeros(acc_ref.shape, F32)
        pad = jnp.zeros((16 - t_new, DA_W), F32)
        k8 = jnp.concatenate([kn_ref[0], pad], axis=0)
        v8 = jnp.concatenate([vn_ref[0], pad], axis=0)
        s = _mm_nt(qrows_ref[...], k8)
        key = lax.broadcasted_iota(jnp.int32, s.shape, 1)
        qidx = (lax.broadcasted_iota(jnp.int32, s.shape, 0) >> 3) & (t_new - 1)
        s = jnp.where(key <= qidx, s, NEG_BIG)
        update(s, [v8], _mm)

    def pairs(refs):
        return [jnp.concatenate([refs[i][...].astype(BF16), refs[i + 1][...].astype(BF16)], axis=1)
                for i in range(0, pp, 2)]

    s_all = jnp.concatenate([_mm(qrows_ref[...], kp) for kp in pairs(k_refs)], axis=1)
    update(s_all, pairs(v_refs), _mm_nt)

    @pl.when(j == n_pages // pp - 1)
    def _():
        o = acc_ref[...] / l_ref[...]
        half = nrow // 2
        oc = o[0:half] - lam_ref[...] * o[half:]
        hd = lax.broadcasted_iota(jnp.int32, oc.shape, 0) & 7
        lane = lax.broadcasted_iota(jnp.int32, oc.shape, 1)
        oc = jnp.where((lane >> 6) == hd, oc, 0.0)
        o_ref[0] = jnp.sum(oc.reshape(t_new, 8, DA_W), axis=1)


def _attn_sample_call(q, k_new, v_new, cache_k, cache_v, page_table, lam, layer):
    nseq, t_new, _ = q.shape
    n_pages = page_table.shape[1]
    pp = PAGES_PER_STEP
    nrow = 2 * t_new * 8
    pt_flat = page_table.reshape(-1).astype(jnp.int32)

    def page_spec(idx):
        return pl.BlockSpec((None, None, DA_W, PAGE_SIZE),
                            lambda bi, j, pt: (layer, pt[bi * n_pages + j * pp + idx], 0, 0))

    tok_spec = pl.BlockSpec((1, t_new, DA_W), lambda bi, j, pt: (bi, 0, 0))
    grid_spec = pltpu.PrefetchScalarGridSpec(
        num_scalar_prefetch=1,
        grid=(nseq, n_pages // pp),
        in_specs=[pl.BlockSpec((1, 1), lambda bi, j, pt: (0, 0)), tok_spec, tok_spec, tok_spec]
        + [page_spec(i) for i in range(pp)] + [page_spec(i) for i in range(pp)],
        out_specs=tok_spec,
        scratch_shapes=[pltpu.VMEM((nrow, DA_W), BF16), pltpu.VMEM((nrow, 1), F32),
                        pltpu.VMEM((nrow, 1), F32), pltpu.VMEM((nrow, DA_W), F32)])
    return pl.pallas_call(
        functools.partial(_attn_sample_kernel, t_new=t_new, n_pages=n_pages),
        grid_spec=grid_spec,
        out_shape=jax.ShapeDtypeStruct((nseq, t_new, DA_W), F32),
        compiler_params=_params("parallel", "arbitrary"),
        name="attn_sample",
    )(pt_flat, lam, q, k_new, v_new, *([cache_k] * pp), *([cache_v] * pp))


def _mix_kernel(x_ref, gate_ref, odn_ref, dng_ref, ossm_ref, oda_ref, dag_ref, onorm_ref, subln_ref, bd_ref,
                w_ref, y_ref, *, head_major):
    if head_major:
        odn = jnp.concatenate([odn_ref[0, hd] for hd in range(DN_HEADS)], axis=1)
        oda = jnp.concatenate([oda_ref[0, hd] for hd in range(DA_HEADS)], axis=1)
    else:
        odn = odn_ref[0]
        oda = oda_ref[0]

    def head_norm(o, gain):
        ms = _mm_split_lhs(o * o, bd_ref[...]) * (1.0 / DN_DV)
        return o * lax.rsqrt(ms + NORM_EPS) * gain

    a = head_norm(odn, onorm_ref[...]) * _silu(dng_ref[0])
    c = head_norm(oda, subln_ref[...]) * _silu(dag_ref[0])
    mixed = (jnp.dot(a.astype(BF16), w_ref[0, 0:DN_W, :], preferred_element_type=F32)
             + jnp.dot(ossm_ref[0].astype(BF16), w_ref[0, DN_W:DN_W + SSM_W, :], preferred_element_type=F32)
             + jnp.dot(c.astype(BF16), w_ref[0, DN_W + SSM_W:, :], preferred_element_type=F32))
    y_ref[0] = x_ref[0] + gate_ref[0] * mixed


def _mix_call(x, gate, odn, dng, ossm, oda, dag, onorm_row, subln_row, w_out_bf16, layer, head_major):
    b, t, _ = x.shape
    tm = min(TM_PROJ, t)
    per_row = gate.shape[1] != 1
    tmm = tm if per_row else 1
    mod_map = (lambda bi, i: (bi, i, 0)) if per_row else (lambda bi, i: (bi, 0, 0))
    row_map = lambda bi, i: (bi, i, 0)
    if head_major:
        o_spec = pl.BlockSpec((1, DN_HEADS, tm, DN_DV), lambda bi, i: (bi, 0, i, 0))
    else:
        o_spec = pl.BlockSpec((1, tm, DN_W), row_map)
    c = jnp.arange(DN_W)
    bd = ((c[:, None] // DN_DV) == (c[None, :] // DN_DV)).astype(BF16)
    return pl.pallas_call(
        functools.partial(_mix_kernel, head_major=head_major),
        grid=(b, t // tm),
        in_specs=[pl.BlockSpec((1, tm, D_MODEL), row_map),
                  pl.BlockSpec((1, tmm, D_MODEL), mod_map),
                  o_spec,
                  pl.BlockSpec((1, tm, DN_W), row_map),
                  pl.BlockSpec((1, tm, SSM_W), row_map),
                  o_spec,
                  pl.BlockSpec((1, tm, DA_W), row_map),
                  pl.BlockSpec((1, DN_W), lambda bi, i: (0, 0)),
                  pl.BlockSpec((1, DA_W), lambda bi, i: (0, 0)),
                  pl.BlockSpec((DN_W, DN_W), lambda bi, i: (0, 0)),
                  pl.BlockSpec((1, MIX_W, D_MODEL), lambda bi, i: (layer, 0, 0))],
        out_specs=pl.BlockSpec((1, tm, D_MODEL), row_map),
        out_shape=jax.ShapeDtypeStruct((b, t, D_MODEL), F32),
        compiler_params=_params("parallel", "parallel"),
        name="mix",
    )(x, gate, odn, dng, ossm, oda, dag, onorm_row, subln_row, bd, w_out_bf16)


def _final_norm_kernel(x_ref, g_ref, o_ref):
    x = x_ref[0]
    ms = jnp.mean(x * x, axis=-1, keepdims=True)
    o_ref[0] = x * lax.rsqrt(ms + NORM_EPS) * g_ref[...]


def _final_norm_call(x, g):
    b, t, _ = x.shape
    tm = min(1024, t)
    return pl.pallas_call(
        _final_norm_kernel,
        grid=(b, t // tm),
        in_specs=[pl.BlockSpec((1, tm, D_MODEL), lambda bi, i: (bi, i, 0)),
                  pl.BlockSpec((1, D_MODEL), lambda bi, i: (0, 0))],
        out_specs=pl.BlockSpec((1, tm, D_MODEL), lambda bi, i: (bi, i, 0)),
        out_shape=jax.ShapeDtypeStruct((b, t, D_MODEL), F32),
        compiler_params=_params("parallel", "parallel"),
        name="final_norm",
    )(x, g)


def _permute_w_in(w_in):
    splits = (DN_CONV_CH, DN_HEADS, DN_HEADS, DN_W, SSM_W, SSM_W, DA_W, DA_W, DA_W, DA_W)
    offs = [0]
    for n in splits:
        offs.append(offs[-1] + n)
    qkv, dnb, dna, dng, su, sg, q, k, v, dag = [w_in[..., offs[i]:offs[i + 1]] for i in range(10)]
    pad = jnp.zeros(w_in.shape[:-1] + (128 - 2 * DN_HEADS,), w_in.dtype)
    return jnp.concatenate([qkv, dng, su, sg, q, k, v, dag, dnb, dna, pad], axis=-1).astype(BF16)


def _rope_tables(pos):
    half = DA_HD // 2
    inv = jnp.power(ROPE_THETA, -jnp.arange(half, dtype=F32) * 2.0 / DA_HD)
    ang = pos.astype(F32)[:, None] * inv[None, :]
    cos = jnp.tile(jnp.cos(ang), (1, 128 // half))
    sin = jnp.tile(jnp.sin(ang), (1, 128 // half))
    sign = jnp.where((jnp.arange(128) & half) == 0, -1.0, 1.0).astype(F32)
    return cos, sin * sign[None, :]


def _s5_params(lam_re, lam_im, log_dt, b_re, b_im, c_re, c_im, d_skip, w_glu, b_glu):
    g, p, cg = SSM_GROUPS, SSM_P, SSM_GROUP_CH
    lam = lax.complex(lam_re.astype(F32), lam_im.astype(F32))
    dt = jnp.exp(log_dt.astype(F32))[:, None]
    lam_bar = jnp.exp(lam * dt)
    b_bar = ((lam_bar - 1.0) / lam)[..., None] * lax.complex(b_re.astype(F32), b_im.astype(F32))
    eye = jnp.eye(g, dtype=F32)
    b_t = jnp.transpose(b_bar, (0, 2, 1))
    bb_re = jnp.einsum("gcp,gh->gchp", jnp.real(b_t), eye).reshape(g * cg, g * p)
    bb_im = jnp.einsum("gcp,gh->gchp", jnp.imag(b_t), eye).reshape(g * cg, g * p)
    bblk = jnp.concatenate([bb_re, bb_im], axis=1).astype(BF16)
    c_t_re = jnp.transpose(c_re.astype(F32), (0, 2, 1))
    c_t_im = jnp.transpose(c_im.astype(F32), (0, 2, 1))
    cc_re = jnp.einsum("gpc,gh->gphc", c_t_re, eye).reshape(g * p, g * cg)
    cc_im = jnp.einsum("gpc,gh->gphc", c_t_im, eye).reshape(g * p, g * cg)
    cblk = jnp.concatenate([cc_re, -cc_im], axis=0).astype(BF16)
    a = lam_bar.reshape(1, g * p)
    steps = jnp.arange(1, S5_SUB + 1, dtype=F32)[:, None]
    pw = jnp.exp((lam * dt).reshape(1, g * p) * steps)
    ap = pw[S5_SUB - 1:S5_SUB]
    return {"bblk": bblk, "cblk": cblk,
            "ar": jnp.real(a), "ai": jnp.imag(a),
            "apr": jnp.real(ap), "api": jnp.imag(ap),
            "pw": jnp.repeat(jnp.concatenate([jnp.real(pw), jnp.imag(pw)], axis=1), 8, axis=0),
            "d": d_skip.astype(F32).reshape(1, SSM_W),
            "wglu": w_glu.astype(BF16), "bglu": b_glu.astype(F32).reshape(1, SSM_W)}


def _s5_block_order(a, outer, inner):
    b, t, w = a.shape
    a = a.reshape(b, t // (outer * inner), outer, inner, w)
    return jnp.transpose(a, (0, 1, 3, 2, 4)).reshape(b, t, w)


def _lane_row(vals, offset):
    return jnp.zeros((1, 128), F32).at[0, offset:offset + vals.shape[0]].set(vals.astype(F32))


def kernel(x_prompt, x_sample, c_prompt, c_sample, cache_k, cache_v, page_table, state_conv, state_delta, state_ssm_re, state_ssm_im, norm_g, w_ada, b_ada, w_in, conv_w, dn_a_log, dn_dt_bias, dn_onorm, ssm_lam_re, ssm_lam_im, ssm_log_dt, ssm_b_re, ssm_b_im, ssm_c_re, ssm_c_im, ssm_d, ssm_w_glu, ssm_b_glu, da_lam_q1, da_lam_k1, da_lam_q2, da_lam_k2, da_subln, w_out, final_g):
    bp, tp, _ = x_prompt.shape
    bs, ts, _ = x_sample.shape
    n_pages = page_table.shape[1]
    past = n_pages * PAGE_SIZE
    n_pool = cache_k.shape[1]
    rs = bs * ts

    n_c = bp + bs
    c_rows = -(-n_c // 8) * 8
    c_all = jnp.concatenate([c_prompt, c_sample, jnp.zeros((c_rows - n_c, D_MODEL), F32)], axis=0)
    mods = _ada_call(c_all, w_ada, b_ada)

    w_perm = _permute_w_in(w_in)
    w_out_bf = w_out.astype(BF16)
    cos_p, sin_p = _rope_tables(jnp.arange(tp, dtype=jnp.int32))
    cos_s, sin_s = _rope_tables(past + (jnp.arange(rs, dtype=jnp.int32) % ts))
    cache_k4 = jnp.transpose(cache_k, (0, 1, 3, 4, 2)).reshape(DEPTH, n_pool, DA_W, PAGE_SIZE)
    cache_v4 = jnp.transpose(cache_v, (0, 1, 3, 4, 2)).reshape(DEPTH, n_pool, DA_W, PAGE_SIZE)
    conv_w8 = jnp.concatenate([conv_w, jnp.zeros((DEPTH, 8 - CONV_K, DN_CONV_CH), F32)], axis=1)
    zero_buf = jnp.zeros((bp, 8, DN_CONV_CH), F32)
    zero_delta = jnp.zeros((bp, DN_HEADS, DN_DK, DN_DV), F32)
    zero_h = jnp.zeros((bp, 1, SSM_N), F32)

    xp = x_prompt
    xs = x_sample.reshape(1, rs, D_MODEL)
    outs = {k: [] for k in ("kp", "vp", "ks", "vs", "cp", "cs", "dp", "ds", "hrp", "hip", "hrs", "his")}
    for l in range(DEPTH):
        lam_init = 0.8 - 0.6 * math.exp(-0.3 * l)
        lam = (jnp.exp(jnp.sum(da_lam_q1[l].astype(F32) * da_lam_k1[l].astype(F32)))
               - jnp.exp(jnp.sum(da_lam_q2[l].astype(F32) * da_lam_k2[l].astype(F32))) + lam_init).reshape(1, 1)
        g_row = norm_g[l].reshape(1, D_MODEL)
        alog_row = _lane_row(dn_a_log[l], DN_HEADS)
        dtb_row = _lane_row(dn_dt_bias[l], DN_HEADS)
        onorm_row = jnp.tile(dn_onorm[l].astype(F32), DN_HEADS).reshape(1, DN_W)
        subln_row = (jnp.tile(da_subln[l].astype(F32), DA_HEADS) * (1.0 - lam_init)).reshape(1, DA_W)
        sp = _s5_params(ssm_lam_re[l], ssm_lam_im[l], ssm_log_dt[l], ssm_b_re[l], ssm_b_im[l],
                        ssm_c_re[l], ssm_c_im[l], ssm_d[l], ssm_w_glu[l], ssm_b_glu[l])

        mp = mods[l, 0:bp].reshape(bp, 1, 3 * D_MODEL)
        shift, scale, gate = mp[..., 0:D_MODEL], mp[..., D_MODEL:2 * D_MODEL], mp[..., 2 * D_MODEL:]
        qkv, dng, su, sg, dag, sm, qh, kh, vh = _inproj_call(
            xp, scale, shift, g_row, cos_p, sin_p, w_perm, l, True)
        odn, s_fin = _delta_call(qkv, zero_buf, conv_w8[l], sm, alog_row, dtb_row,
                                 min(DN_ROWS, tp), DN_CHUNK, (0, DN_CHUNK), zero_delta)
        ossm, hr, hi = _s5_call(_s5_block_order(su, 8, S5_SUB), _s5_block_order(sg, 8, S5_SUB),
                                zero_h, zero_h, sp)
        ossm = _s5_block_order(ossm, S5_SUB, 8)
        oda = _attn_prompt_call(qh, kh, vh, lam)
        xp = _mix_call(xp, gate, odn, dng, ossm, oda, dag, onorm_row, subln_row, w_out_bf, l, True)
        outs["kp"].append(jnp.transpose(kh, (0, 3, 1, 2)))
        outs["vp"].append(jnp.transpose(vh, (0, 3, 1, 2)))
        outs["cp"].append(qkv[:, tp - (CONV_K - 1):, :])
        outs["dp"].append(s_fin)
        outs["hrp"].append(hr.reshape(bp, SSM_GROUPS, SSM_P))
        outs["hip"].append(hi.reshape(bp, SSM_GROUPS, SSM_P))

        ms_ = jnp.repeat(mods[l, bp:bp + bs], ts, axis=0).reshape(1, rs, 3 * D_MODEL)
        shift, scale, gate = ms_[..., 0:D_MODEL], ms_[..., D_MODEL:2 * D_MODEL], ms_[..., 2 * D_MODEL:]
        qkv, dng, su, sg, dag, sm, qrow, kf, vf = _inproj_call(
            xs, scale, shift, g_row, cos_s, sin_s, w_perm, l, False)
        qkv_seq = qkv.reshape(bs, ts, DN_CONV_CH)
        nb = CONV_K - 1
        pad_t = DN_CHUNK_S - nb - ts
        qkv_cat = jnp.pad(jnp.concatenate([state_conv[l], qkv_seq], axis=1), ((0, 0), (0, pad_t), (0, 0)))
        sm_cat = jnp.pad(sm.reshape(bs, ts, 128), ((0, 0), (nb, pad_t), (0, 0)))
        u, w, qd, kd, at, gc = _delta_call(
            qkv_cat.reshape(1, bs * DN_CHUNK_S, DN_CONV_CH), zero_buf[0:1], conv_w8[l],
            sm_cat.reshape(1, bs * DN_CHUNK_S, 128), alog_row, dtb_row,
            min(DN_ROWS, bs * DN_CHUNK_S), DN_CHUNK_S, (nb, nb + ts))
        odn, s_fin = _delta_step_call(u, w, qd, kd, at, gc, state_delta[l], DN_CHUNK_S)
        odn = odn.reshape(DN_HEADS, bs, DN_CHUNK_S, DN_DV)[:, :, nb:nb + ts, :]
        odn = jnp.transpose(odn, (1, 2, 0, 3)).reshape(1, rs, DN_W)
        su_t = jnp.transpose(su.reshape(bs, ts, SSM_W), (1, 0, 2))
        sg_t = jnp.transpose(sg.reshape(bs, ts, SSM_W), (1, 0, 2))
        ossm, hr, hi = _s5_step_call(su_t, sg_t, state_ssm_re[l].reshape(bs, SSM_N),
                                     state_ssm_im[l].reshape(bs, SSM_N), sp, bs, ts)
        ossm = jnp.transpose(ossm, (1, 0, 2))
        oda = _attn_sample_call(qrow.reshape(bs, ts, DA_W), kf.reshape(bs, ts, DA_W), vf.reshape(bs, ts, DA_W),
                                cache_k4, cache_v4, page_table, lam, l)
        xs = _mix_call(xs, gate, odn, dng, ossm.reshape(1, rs, SSM_W), oda.reshape(1, rs, DA_W), dag,
                       onorm_row, subln_row, w_out_bf, l, False)
        xp_conv = jnp.concatenate([state_conv[l], qkv_seq], axis=1)
        outs["ks"].append(kf.reshape(bs, ts, DA_HEADS, DA_VD))
        outs["vs"].append(vf.reshape(bs, ts, DA_HEADS, DA_VD))
        outs["cs"].append(xp_conv[:, xp_conv.shape[1] - (CONV_K - 1):, :])
        outs["ds"].append(s_fin)
        outs["hrs"].append(hr.reshape(bs, SSM_GROUPS, SSM_P))
        outs["his"].append(hi.reshape(bs, SSM_GROUPS, SSM_P))

    fg = final_g.reshape(1, D_MODEL)
    y_prompt = _final_norm_call(xp, fg)
    y_sample = _final_norm_call(xs, fg).reshape(bs, ts, D_MODEL)
    st = {k: jnp.stack(v) for k, v in outs.items()}
    return (y_prompt, y_sample, st["kp"], st["vp"], st["ks"], st["vs"], st["cp"], st["cs"],
            st["dp"], st["ds"], st["hrp"], st["hip"], st["hrs"], st["his"])
```

```python
import functools
import math

import jax
import jax.numpy as jnp
from jax import lax
from jax.experimental import pallas as pl
from jax.experimental.pallas import tpu as pltpu

F32 = jnp.float32
BF16 = jnp.bfloat16

D_MODEL = 1024
DEPTH = 4
PAGE_SIZE = 128
DN_HEADS = 6
DN_DK = 64
DN_DV = 64
DN_W = DN_HEADS * DN_DV
DN_CONV_CH = 2 * DN_HEADS * DN_DK + DN_W
CONV_K = 4
DN_CHUNK = 64
SSM_GROUPS = 16
SSM_GROUP_CH = 16
SSM_W = SSM_GROUPS * SSM_GROUP_CH
SSM_P = 64
SSM_N = SSM_GROUPS * SSM_P
DA_HEADS = 6
DA_HD = 32
DA_VD = 2 * DA_HD
DA_W = DA_HEADS * DA_VD
MIX_W = DN_W + SSM_W + DA_W
ROPE_THETA = 10000.0
NORM_EPS = 1e-6
NEG_BIG = -1e30

SEG_QKV = (0, 1152)
SEG_DNG = (1152, 1536)
SEG_SU = (1536, 1792)
SEG_SG = (1792, 2048)
SEG_Q = (2048, 2432)
SEG_K = (2432, 2816)
SEG_V = (2816, 3200)
SEG_DAG = (3200, 3584)
SEG_SM = (3584, 3712)
IN_W_PAD = 3712

TM_PROJ = 512
DN_ROWS = 256
DN_GROUP = 128
DN_CHUNK_S = 16
S5_ROWS = 512
S5_SUB = 64
S5_SLABS = SSM_N // 128
TQ = 2048
TK_SUB = 512
PAGES_PER_STEP = 16
ATT_PAD = 16
VMEM_LIMIT = 56 * 1024 * 1024

Q_SCALE = (DA_HD ** -0.5) * math.log2(math.e)


def _mm(a, b):
    return jnp.dot(a.astype(BF16), b.astype(BF16), preferred_element_type=F32)


def _mm_nt(a, b):
    return lax.dot_general(a.astype(BF16), b.astype(BF16), (((1,), (1,)), ((), ())),
                           preferred_element_type=F32)


def _mm_tn(a, b):
    return lax.dot_general(a.astype(BF16), b.astype(BF16), (((0,), (0,)), ((), ())),
                           preferred_element_type=F32)


def _split(a):
    hi = a.astype(BF16)
    lo = (a - hi.astype(F32)).astype(BF16)
    return hi, lo


def _mm_split_lhs(a, b_bf16):
    hi, lo = _split(a)
    return (jnp.dot(hi, b_bf16, preferred_element_type=F32)
            + jnp.dot(lo, b_bf16, preferred_element_type=F32))


def _mm3(a, b):
    ah, al = _split(a)
    bh, bl = _split(b)
    return (jnp.dot(ah, bh, preferred_element_type=F32)
            + jnp.dot(ah, bl, preferred_element_type=F32)
            + jnp.dot(al, bh, preferred_element_type=F32))


def _silu(x):
    return x * jax.nn.sigmoid(x)


def _softplus(x):
    return jnp.maximum(x, 0.0) + jnp.log1p(jnp.exp(-jnp.abs(x)))


def _gelu_tanh(x):
    c = math.sqrt(2.0 / math.pi)
    return x * (0.5 * (1.0 + jnp.tanh(c * (x + 0.044715 * (x * x * x)))))


def _params(*sem):
    return pltpu.CompilerParams(dimension_semantics=sem, vmem_limit_bytes=VMEM_LIMIT)


def _ada_kernel(c_ref, w_ref, b_ref, o_ref):
    c = c_ref[...]
    o_ref[0] = _mm3(_silu(c), w_ref[0]) + b_ref[0]


def _ada_call(c_all, w_ada, b_ada):
    rows = c_all.shape[0]
    tn = 1024
    return pl.pallas_call(
        _ada_kernel,
        grid=(DEPTH, 3 * D_MODEL // tn),
        in_specs=[pl.BlockSpec((rows, D_MODEL), lambda l, n: (0, 0)),
                  pl.BlockSpec((1, D_MODEL, tn), lambda l, n: (l, 0, n)),
                  pl.BlockSpec((1, 1, tn), lambda l, n: (l, 0, n))],
        out_specs=pl.BlockSpec((1, rows, tn), lambda l, n: (l, 0, n)),
        out_shape=jax.ShapeDtypeStruct((DEPTH, rows, 3 * D_MODEL), F32),
        compiler_params=_params("parallel", "parallel"),
        name="adaln",
    )(c_all, w_ada, b_ada.reshape(DEPTH, 1, 3 * D_MODEL))


def _rope(x, cos, sin):
    lane = lax.broadcasted_iota(jnp.int32, cos.shape, 1)
    low = (lane & 16) == 0
    outs = []
    for c in range(x.shape[1] // 128):
        xc = x[:, c * 128:(c + 1) * 128]
        sw = jnp.where(low, pltpu.roll(xc, 112, 1), pltpu.roll(xc, 16, 1))
        outs.append(xc * cos + sw * sin)
    return jnp.concatenate(outs, axis=1)


def _inproj_kernel(x_ref, sc_ref, sh_ref, g_ref, cos_ref, sin_ref, w_ref, *outs, head_major):
    x = x_ref[0]
    ms = jnp.mean(x * x, axis=-1, keepdims=True)
    h = x * lax.rsqrt(ms + NORM_EPS) * g_ref[...] * (1.0 + sc_ref[0]) + sh_ref[0]
    hb = h.astype(BF16)

    def seg(ab):
        return jnp.dot(hb, w_ref[0, :, ab[0]:ab[1]], preferred_element_type=F32)

    qkv_o, dng_o, su_o, sg_o, dag_o, sm_o, q_o, k_o, v_o = outs
    cos = cos_ref[...]
    sin = sin_ref[...]

    def emit(o_ref, val, dt):
        if head_major:
            for hd in range(DA_HEADS):
                o_ref[0, hd] = val[:, hd * DA_VD:(hd + 1) * DA_VD].T.astype(dt)
        else:
            o_ref[0] = val

    emit(q_o, _rope(seg(SEG_Q), cos, sin) * Q_SCALE, BF16)
    qkv_o[0] = seg(SEG_QKV)
    emit(k_o, _rope(seg(SEG_K), cos, sin), F32)
    dng_o[0] = seg(SEG_DNG)
    su_o[0] = seg(SEG_SU)
    emit(v_o, seg(SEG_V), F32)
    sg_o[0] = seg(SEG_SG)
    dag_o[0] = seg(SEG_DAG)
    sm_o[0] = seg(SEG_SM)


def _inproj_call(x, scale, shift, norm_g, cos, sin, w_perm, layer, head_major):
    b, t, _ = x.shape
    tm = min(TM_PROJ, t)
    per_row = scale.shape[1] != 1
    tmm = tm if per_row else 1
    mod_map = (lambda bi, i: (bi, i, 0)) if per_row else (lambda bi, i: (bi, 0, 0))
    row_map = lambda bi, i: (bi, i, 0)

    def row_spec(w):
        return pl.BlockSpec((1, tm, w), row_map)

    def row_shape(w):
        return jax.ShapeDtypeStruct((b, t, w), F32)

    out_specs = [row_spec(1152), row_spec(384), row_spec(256), row_spec(256), row_spec(384), row_spec(128)]
    out_shape = [row_shape(1152), row_shape(384), row_shape(256), row_shape(256), row_shape(384), row_shape(128)]
    if head_major:
        for dt in (BF16, F32, F32):
            out_specs.append(pl.BlockSpec((1, DA_HEADS, DA_VD, tm), lambda bi, i: (bi, 0, 0, i)))
            out_shape.append(jax.ShapeDtypeStruct((b, DA_HEADS, DA_VD, t), dt))
    else:
        out_specs += [row_spec(384)] * 3
        out_shape += [row_shape(384)] * 3
    return pl.pallas_call(
        functools.partial(_inproj_kernel, head_major=head_major),
        grid=(b, t // tm),
        in_specs=[row_spec(D_MODEL),
                  pl.BlockSpec((1, tmm, D_MODEL), mod_map),
                  pl.BlockSpec((1, tmm, D_MODEL), mod_map),
                  pl.BlockSpec((1, D_MODEL), lambda bi, i: (0, 0)),
                  pl.BlockSpec((tm, 128), lambda bi, i: (i, 0)),
                  pl.BlockSpec((tm, 128), lambda bi, i: (i, 0)),
                  pl.BlockSpec((1, D_MODEL, IN_W_PAD), lambda bi, i: (layer, 0, 0))],
        out_specs=out_specs,
        out_shape=out_shape,
        compiler_params=_params("parallel", "parallel"),
        name="inproj",
    )(x, scale, shift, norm_g, cos, sin, w_perm)


def _delta_kernel(x_ref, prev_ref, buf_ref, cw_ref, sm_ref, alog_ref, dtb_ref, ltri_ref, last_ref,
                  fold_ref, bd_ref, *rest, rows, chunk, valid, scan):
    if scan:
        s0_ref, o_ref, sf_ref, xs_ref, s_ref = rest
    else:
        u_o, w_o, qd_o, kd_o, at_o, gc_o, xs_ref = rest
    i = pl.program_id(1)

    if scan:
        @pl.when(i == 0)
        def _():
            s_ref[...] = s0_ref[0]
    halo = jnp.where(i == 0, buf_ref[0], prev_ref[0])
    xs_ref[0:8, :] = halo
    xs_ref[8:8 + rows, :] = x_ref[0]
    cw = cw_ref[...]
    y = (xs_ref[pl.ds(5, rows), :] * cw[0:1] + xs_ref[pl.ds(6, rows), :] * cw[1:2]
         + xs_ref[pl.ds(7, rows), :] * cw[2:3] + xs_ref[pl.ds(8, rows), :] * cw[3:4])
    y = _silu(y)
    qk = y[:, 0:2 * DN_W]
    ss = _mm_split_lhs(qk * qk, bd_ref[...])
    qkn = qk * lax.rsqrt(ss + NORM_EPS)
    v_all = y[:, 2 * DN_W:]

    sm = sm_ref[0]
    pos = lax.broadcasted_iota(jnp.int32, sm.shape, 0) & (chunk - 1)
    real = (pos >= valid[0]) & (pos < valid[1])
    beta = jnp.where(real, jax.nn.sigmoid(sm), 0.0)
    g = jnp.where(real, -jnp.exp(alog_ref[...]) * _softplus(sm + dtb_ref[...]), 0.0)
    gc = _mm_split_lhs_rhs(ltri_ref[...], g)
    gc_last = _mm_split_lhs_rhs(last_ref[...], gc)
    if not scan:
        gc_o[0] = gc
    gc_t = gc.T

    grp = min(rows, DN_GROUP)
    n_grp = rows // grp
    per_grp = grp // chunk
    shift = chunk.bit_length() - 1
    ri = lax.broadcasted_iota(jnp.int32, (grp, grp), 0)
    ci = lax.broadcasted_iota(jnp.int32, (grp, grp), 1)
    same_chunk = (ri >> shift) == (ci >> shift)
    causal = same_chunk & (ri >= ci)
    strict_b = jnp.where(same_chunk & (ri > ci), 1.0, 0.0).astype(BF16)
    eye_f = jnp.where(ri == ci, 1.0, 0.0)
    level_b = []
    s = 1
    while s < chunk:
        sh = s.bit_length()
        m = ((ri >> sh) == (ci >> sh)) & ((ri & s) != 0) & ((ci & s) == 0)
        level_b.append(jnp.where(m, 1.0, 0.0).astype(BF16))
        s *= 2
    scale = DN_DK ** -0.5
    per_head = []

    probs = []
    for g in range(n_grp):
        gr = slice(g * grp, (g + 1) * grp)
        for hd in range(DN_HEADS):
            qh = qkn[gr, hd * DN_DK:(hd + 1) * DN_DK] * scale
            kh = qkn[gr, DN_W + hd * DN_DK:DN_W + (hd + 1) * DN_DK]
            beta_c = beta[gr, hd:hd + 1]
            gcol = gc[gr, DN_HEADS + hd:DN_HEADS + hd + 1]
            grow = gc_t[DN_HEADS + hd:DN_HEADS + hd + 1, gr]
            decay = jnp.exp(jnp.where(causal, gcol - grow, NEG_BIG))
            kb = kh * beta_c
            kh_b = kh.astype(BF16)
            m_b = (_mm_nt(kb, kh_b) * decay).astype(BF16) * strict_b
            probs.append((g, hd, qh, kh, kh_b, beta_c, gcol, decay, kb, m_b))
    x_invs = [eye_f - (pr[9] * level_b[0]).astype(F32) for pr in probs]
    for lb in level_b[1:]:
        for n, pr in enumerate(probs):
            x_b = x_invs[n].astype(BF16)
            t_b = jnp.dot(x_b, pr[9] * lb, preferred_element_type=F32).astype(BF16)
            x_invs[n] = x_invs[n] - jnp.dot(t_b, x_b, preferred_element_type=F32)

    for n, (g, hd, qh, kh, kh_b, beta_c, gcol, decay, kb, _) in enumerate(probs):
        gr = slice(g * grp, (g + 1) * grp)
        vh = v_all[gr, hd * DN_DV:(hd + 1) * DN_DV]
        e_g = jnp.exp(gcol)
        rhs = jnp.concatenate([vh * beta_c, kb * e_g], axis=1)
        sol = _mm(x_invs[n], rhs)
        attn_full = _mm_nt(qh, kh_b) * decay
        u = sol[:, 0:DN_DV]
        w = sol[:, DN_DV:]
        qd = qh * e_g
        gl_col = gc_last[gr, DN_HEADS + hd:DN_HEADS + hd + 1]
        kd = kh * jnp.exp(gl_col - gcol)
        if scan:
            per_head.append((u, w.astype(BF16), qd.astype(BF16), kd.astype(BF16), attn_full.astype(BF16),
                             jnp.exp(gl_col)))
        else:
            u_o[0, hd, gr, :] = u
            w_o[0, hd, gr, :] = w
            qd_o[0, hd, gr, :] = qd
            kd_o[0, hd, gr, :] = kd
            at_o[0, hd, gr, :] = _mm(attn_full, fold_ref[0:grp, :])

    if scan:
        states = [s_ref[hd] for hd in range(DN_HEADS)]
        for g in range(n_grp):
            v_news = [[] for _ in range(DN_HEADS)]
            o_states = [[] for _ in range(DN_HEADS)]
            for c in range(per_grp):
                rc = slice(c * chunk, (c + 1) * chunk)
                for hd in range(DN_HEADS):
                    u, w_b, qd_b, kd_b, _, e_last = per_head[g * DN_HEADS + hd]
                    r = _mm(jnp.concatenate([w_b[rc], qd_b[rc]], axis=0), states[hd])
                    v_new = u[rc] - r[0:chunk]
                    o_states[hd].append(r[chunk:])
                    states[hd] = states[hd] * e_last[c * chunk:c * chunk + 1] + _mm_tn(kd_b[rc], v_new)
                    v_news[hd].append(v_new)
            for hd in range(DN_HEADS):
                o_ref[0, hd, g * grp:(g + 1) * grp, :] = (
                    jnp.concatenate(o_states[hd], axis=0)
                    + _mm(per_head[g * DN_HEADS + hd][4], jnp.concatenate(v_news[hd], axis=0)))
        for hd in range(DN_HEADS):
            s_ref[hd] = states[hd]

        @pl.when(i == pl.num_programs(1) - 1)
        def _():
            sf_ref[0] = s_ref[...]


def _mm_split_lhs_rhs(a_bf16, b):
    hi, lo = _split(b)
    return (jnp.dot(a_bf16, hi, preferred_element_type=F32)
            + jnp.dot(a_bf16, lo, preferred_element_type=F32))


def _delta_call(qkv, buf8, conv_w8, sm, alog_row, dtb_row, rows, chunk, valid, s0=None):
    b, t, _ = qkv.shape
    nblk = t // rows
    scan = s0 is not None
    r = jnp.arange(rows)
    same = (r[:, None] // chunk) == (r[None, :] // chunk)
    ltri = ((r[:, None] >= r[None, :]) & same).astype(BF16)
    last = (r[None, :] == (r[:, None] | (chunk - 1))).astype(BF16)
    fold = ((r[:, None] & (chunk - 1)) == jnp.arange(DN_CHUNK)[None, :]).astype(BF16)
    c = jnp.arange(2 * DN_W)
    bd = ((c[:, None] // DN_DK) == (c[None, :] // DN_DK)).astype(BF16)
    hm_spec = pl.BlockSpec((1, DN_HEADS, rows, DN_DV), lambda bi, i: (bi, 0, i, 0))
    hm_shape = jax.ShapeDtypeStruct((b, DN_HEADS, t, DN_DV), F32)
    st_spec = pl.BlockSpec((1, DN_HEADS, DN_DK, DN_DV), lambda bi, i: (bi, 0, 0, 0))
    in_specs = [pl.BlockSpec((1, rows, DN_CONV_CH), lambda bi, i: (bi, i, 0)),
                pl.BlockSpec((1, 8, DN_CONV_CH), lambda bi, i: (bi, jnp.maximum(i * (rows // 8) - 1, 0), 0)),
                pl.BlockSpec((1, 8, DN_CONV_CH), lambda bi, i: (bi, 0, 0)),
                pl.BlockSpec((8, DN_CONV_CH), lambda bi, i: (0, 0)),
                pl.BlockSpec((1, rows, 128), lambda bi, i: (bi, i, 0)),
                pl.BlockSpec((1, 128), lambda bi, i: (0, 0)),
                pl.BlockSpec((1, 128), lambda bi, i: (0, 0)),
                pl.BlockSpec((rows, rows), lambda bi, i: (0, 0)),
                pl.BlockSpec((rows, rows), lambda bi, i: (0, 0)),
                pl.BlockSpec((rows, DN_CHUNK), lambda bi, i: (0, 0)),
                pl.BlockSpec((2 * DN_W, 2 * DN_W), lambda bi, i: (0, 0))]
    args = [qkv, qkv, buf8, conv_w8, sm, alog_row, dtb_row, ltri, last, fold, bd]
    scratch = [pltpu.VMEM((rows + 8, DN_CONV_CH), F32)]
    if scan:
        in_specs.append(st_spec)
        args.append(s0)
        out_specs = [hm_spec, st_spec]
        out_shape = [hm_shape, jax.ShapeDtypeStruct((b, DN_HEADS, DN_DK, DN_DV), F32)]
        scratch.append(pltpu.VMEM((DN_HEADS, DN_DK, DN_DV), F32))
    else:
        out_specs = [hm_spec] * 5 + [pl.BlockSpec((1, rows, 128), lambda bi, i: (bi, i, 0))]
        out_shape = [hm_shape] * 5 + [jax.ShapeDtypeStruct((b, t, 128), F32)]
    return pl.pallas_call(
        functools.partial(_delta_kernel, rows=rows, chunk=chunk, valid=valid, scan=scan),
        grid=(b, nblk),
        in_specs=in_specs,
        out_specs=out_specs,
        out_shape=out_shape,
        scratch_shapes=scratch,
        compiler_params=_params("parallel", "arbitrary" if scan else "parallel"),
        name="delta_scan" if scan else "delta_local",
    )(*args)


def _delta_step_kernel(u_ref, w_ref, qd_ref, kd_ref, at_ref, gc_ref, s0_ref, o_ref, sf_ref, *, bb, chunk):
    for bi in range(bb):
        rows = slice(bi * chunk, (bi + 1) * chunk)
        for hd in range(DN_HEADS):
            s = s0_ref[bi, hd]
            wq = jnp.concatenate([w_ref[0, hd, rows, :], qd_ref[0, hd, rows, :]], axis=0)
            r = _mm(wq, s)
            v_new = u_ref[0, hd, rows, :] - r[0:chunk]
            o_ref[0, hd, rows, :] = r[chunk:] + _mm(at_ref[0, hd, rows, 0:chunk], v_new)
            last = (bi + 1) * chunk - 1
            g_last = jnp.exp(gc_ref[0, last:last + 1, DN_HEADS + hd:DN_HEADS + hd + 1])
            sf_ref[bi, hd] = s * g_last + _mm_tn(kd_ref[0, hd, rows, :], v_new)


def _delta_step_call(u, w, qd, kd, at, gc, s0, chunk):
    nseq = s0.shape[0]
    bb = 4
    hm_spec = pl.BlockSpec((1, DN_HEADS, bb * chunk, DN_DV), lambda i: (0, 0, i, 0))
    st_spec = pl.BlockSpec((bb, DN_HEADS, DN_DK, DN_DV), lambda i: (i, 0, 0, 0))
    return pl.pallas_call(
        functools.partial(_delta_step_kernel, bb=bb, chunk=chunk),
        grid=(nseq // bb,),
        in_specs=[hm_spec] * 5 + [pl.BlockSpec((1, bb * chunk, 128), lambda i: (0, i, 0)), st_spec],
        out_specs=[hm_spec, st_spec],
        out_shape=[jax.ShapeDtypeStruct(u.shape, F32),
                   jax.ShapeDtypeStruct((nseq, DN_HEADS, DN_DK, DN_DV), F32)],
        compiler_params=_params("parallel"),
        name="delta_step",
    )(u, w, qd, kd, at, gc, s0)


def _s5_epilogue(y, u, sg, d_ref, wglu_ref, bglu_ref):
    z = _gelu_tanh(y + d_ref[...] * u)
    gate = jax.nn.sigmoid(_mm(z, wglu_ref[...]) + bglu_ref[...])
    return z * gate * _silu(sg)


def _s5_kernel(u_ref, sg_ref, h0r_ref, h0i_ref, bblk_ref, ar_ref, ai_ref, apr_ref, api_ref, pw_ref,
               cblk_ref, d_ref, wglu_ref, bglu_ref, o_ref, hr_o, hi_o, hs_ref, cr_ref, ci_ref):
    i = pl.program_id(1)
    ns = S5_SLABS

    @pl.when(i == 0)
    def _():
        cr_ref[...] = h0r_ref[0]
        ci_ref[...] = h0i_ref[0]

    def slab(c):
        return slice(c * 128, (c + 1) * 128)

    u = u_ref[0]
    bu = _mm(u, bblk_ref[...])
    for c in range(2 * ns):
        hs_ref[c] = bu[:, slab(c)]
    ar = [jnp.broadcast_to(ar_ref[:, slab(c)], (8, 128)) for c in range(ns)]
    ai = [jnp.broadcast_to(ai_ref[:, slab(c)], (8, 128)) for c in range(ns)]

    def scan_body(j, carry):
        rows = pl.ds(pl.multiple_of(j * 8, 8), 8)
        new = []
        for c in range(ns):
            hr, hi = carry[2 * c], carry[2 * c + 1]
            nr = ar[c] * hr - ai[c] * hi + hs_ref[c, rows, :]
            ni = ar[c] * hi + ai[c] * hr + hs_ref[ns + c, rows, :]
            hs_ref[c, rows, :] = nr
            hs_ref[ns + c, rows, :] = ni
            new += [nr, ni]
        return tuple(new)

    zero = jnp.zeros((8, 128), F32)
    ends = lax.fori_loop(0, S5_SUB, scan_body, (zero,) * (2 * ns))

    h_in = []
    for c in range(ns):
        apr = apr_ref[:, slab(c)]
        api = api_ref[:, slab(c)]
        hr = cr_ref[:, slab(c)]
        hi = ci_ref[:, slab(c)]
        er, ei = ends[2 * c], ends[2 * c + 1]
        rows_r, rows_i = [], []
        for s in range(8):
            rows_r.append(hr)
            rows_i.append(hi)
            nr = apr * hr - api * hi + er[s:s + 1]
            ni = apr * hi + api * hr + ei[s:s + 1]
            hr, hi = nr, ni
        cr_ref[:, slab(c)] = hr
        ci_ref[:, slab(c)] = hi
        h_in += [jnp.concatenate(rows_r, axis=0), jnp.concatenate(rows_i, axis=0)]

    def fix_body(j, carry):
        rows = pl.ds(pl.multiple_of(j * 8, 8), 8)
        for c in range(ns):
            pr = pw_ref[rows, slab(c)]
            pi = pw_ref[rows, slab(ns + c)]
            hr, hi = h_in[2 * c], h_in[2 * c + 1]
            hs_ref[c, rows, :] = hs_ref[c, rows, :] + pr * hr - pi * hi
            hs_ref[ns + c, rows, :] = hs_ref[ns + c, rows, :] + pr * hi + pi * hr
        return carry

    lax.fori_loop(0, S5_SUB, fix_body, 0)

    y = jnp.zeros((S5_ROWS, SSM_W), F32)
    for c in range(2 * ns):
        y = y + _mm(hs_ref[c], cblk_ref[slab(c), :])

    o_ref[0] = _s5_epilogue(y, u, sg_ref[0], d_ref, wglu_ref, bglu_ref)

    @pl.when(i == pl.num_programs(1) - 1)
    def _():
        hr_o[0] = cr_ref[...]
        hi_o[0] = ci_ref[...]


def _s5_call(u, sg, h0r, h0i, sp):
    b, t, _ = u.shape
    n = SSM_N
    row_spec = pl.BlockSpec((1, S5_ROWS, SSM_W), lambda bi, i: (bi, i, 0))
    st_spec = pl.BlockSpec((1, 1, n), lambda bi, i: (bi, 0, 0))
    full = lambda shape: pl.BlockSpec(shape, lambda bi, i: (0,) * len(shape))
    return pl.pallas_call(
        _s5_kernel,
        grid=(b, t // S5_ROWS),
        in_specs=[row_spec, row_spec, st_spec, st_spec,
                  full((SSM_W, 2 * n)), full((1, n)), full((1, n)), full((1, n)), full((1, n)),
                  full((S5_ROWS, 2 * n)), full((2 * n, SSM_W)), full((1, SSM_W)),
                  full((SSM_W, SSM_W)), full((1, SSM_W))],
        out_specs=[row_spec, st_spec, st_spec],
        out_shape=[jax.ShapeDtypeStruct((b, t, SSM_W), F32),
                   jax.ShapeDtypeStruct((b, 1, n), F32), jax.ShapeDtypeStruct((b, 1, n), F32)],
        scratch_shapes=[pltpu.VMEM((2 * S5_SLABS, S5_ROWS, 128), F32), pltpu.VMEM((1, n), F32),
                        pltpu.VMEM((1, n), F32)],
        compiler_params=_params("parallel", "arbitrary"),
        name="s5",
    )(u, sg, h0r, h0i, sp["bblk"], sp["ar"], sp["ai"], sp["apr"], sp["api"], sp["pw"],
      sp["cblk"], sp["d"], sp["wglu"], sp["bglu"])


def _s5_step_kernel(u_ref, sg_ref, h0r_ref, h0i_ref, bblk_ref, ar_ref, ai_ref, cblk_ref, d_ref, wglu_ref,
                    bglu_ref, o_ref, hr_o, hi_o, *, nseq, t):
    hr = h0r_ref[...]
    hi = h0i_ref[...]
    ar = ar_ref[...]
    ai = ai_ref[...]
    n = SSM_N
    for step in range(t):
        u = u_ref[step]
        sg = sg_ref[step]
        bu = _mm(u, bblk_ref[...])
        nr = ar * hr - ai * hi + bu[:, 0:n]
        ni = ar * hi + ai * hr + bu[:, n:]
        hr, hi = nr, ni
        y = _mm(jnp.concatenate([hr, hi], axis=1), cblk_ref[...])
        o_ref[step] = _s5_epilogue(y, u, sg, d_ref, wglu_ref, bglu_ref)
    hr_o[...] = hr
    hi_o[...] = hi


def _s5_step_call(u, sg, h0r, h0i, sp, nseq, t):
    n = SSM_N
    return pl.pallas_call(
        functools.partial(_s5_step_kernel, nseq=nseq, t=t),
        out_shape=[jax.ShapeDtypeStruct((t, nseq, SSM_W), F32),
                   jax.ShapeDtypeStruct((nseq, n), F32), jax.ShapeDtypeStruct((nseq, n), F32)],
        compiler_params=pltpu.CompilerParams(vmem_limit_bytes=VMEM_LIMIT),
        name="s5_step",
    )(u, sg, h0r, h0i, sp["bblk"], sp["ar"], sp["ai"], sp["cblk"], sp["d"], sp["wglu"], sp["bglu"])


def _attn_prompt_kernel(qi_tab, ki_tab, lam_ref, q_ref, k_ref, v_ref, o_ref, qcat_ref, m_ref, acc_ref):
    p = pl.program_id(2)
    qi = qi_tab[p]
    ki = ki_tab[p]
    tq = q_ref.shape[3]

    sub = TK_SUB
    nsub = tq // sub

    @pl.when(ki == 0)
    def _():
        q = q_ref[0, 0]
        feat = lax.broadcasted_iota(jnp.int32, (DA_VD, sub), 0)
        zero = jnp.zeros((DA_VD, sub), q.dtype)
        for blk in range(nsub):
            qb = q[:, blk * sub:(blk + 1) * sub]
            qcat_ref[:, 2 * blk * sub:(2 * blk + 1) * sub] = jnp.where(feat < DA_HD, qb, zero)
            qcat_ref[:, (2 * blk + 1) * sub:(2 * blk + 2) * sub] = jnp.where(feat >= DA_HD, qb, zero)
        m_ref[...] = jnp.full(m_ref.shape, NEG_BIG, F32)
        acc_ref[...] = jnp.zeros(acc_ref.shape, F32)

    def step(diagonal):
        def col_range(c):
            return slice(2 * c * sub, 2 * tq) if diagonal else slice(0, 2 * tq)

        def scores(c):
            return _mm_tn(k_ref[0, 0, :, c * sub:(c + 1) * sub], qcat_ref[:, col_range(c)])

        s_next = scores(0)
        for c in range(nsub):
            keys = slice(c * sub, (c + 1) * sub)
            cols = col_range(c)
            s = s_next
            if c + 1 < nsub:
                s_next = scores(c + 1)
            if diagonal:
                kr = lax.broadcasted_iota(jnp.int32, (sub, 2 * sub), 0)
                qc = lax.broadcasted_iota(jnp.int32, (sub, 2 * sub), 1) & (sub - 1)
                own = jnp.where(kr <= qc, s[:, 0:2 * sub], NEG_BIG)
                s = own if c == nsub - 1 else jnp.concatenate([own, s[:, 2 * sub:]], axis=1)
            m = m_ref[:, cols]
            m_new = jnp.maximum(m, jnp.max(s, axis=0, keepdims=True))
            alpha = jnp.exp2(m - m_new)
            pm = jnp.exp2(s - m_new).astype(BF16)
            v = v_ref[0, 0, :, keys].astype(BF16)
            v_ext = jnp.concatenate([v, jnp.ones((ATT_PAD, sub), BF16)], axis=0)
            acc_ref[:, cols] = alpha * acc_ref[:, cols] + jnp.dot(v_ext, pm, preferred_element_type=F32)
            m_ref[:, cols] = m_new

    @pl.when(ki < qi)
    def _():
        step(False)

    @pl.when(ki == qi)
    def _():
        step(True)
        acc = acc_ref[...]
        outs = []
        for blk in range(nsub):
            a1 = acc[:, 2 * blk * sub:(2 * blk + 1) * sub]
            a2 = acc[:, (2 * blk + 1) * sub:(2 * blk + 2) * sub]
            o1 = a1[0:DA_VD] / a1[DA_VD:DA_VD + 1]
            o2 = a2[0:DA_VD] / a2[DA_VD:DA_VD + 1]
            outs.append(o1 - lam_ref[...] * o2)
        o_ref[0, 0] = jnp.concatenate(outs, axis=1).T


def _attn_prompt_call(qh, kh, vh, lam):
    b, h, _, t = qh.shape
    tq = min(TQ, t)
    nq = t // tq
    qi_tab = jnp.asarray([i for i in range(nq) for _ in range(i + 1)], jnp.int32)
    ki_tab = jnp.asarray([j for i in range(nq) for j in range(i + 1)], jnp.int32)
    grid_spec = pltpu.PrefetchScalarGridSpec(
        num_scalar_prefetch=2,
        grid=(b, h, int(qi_tab.shape[0])),
        in_specs=[pl.BlockSpec((1, 1), lambda bi, hi, p, qt, kt: (0, 0)),
                  pl.BlockSpec((1, 1, DA_VD, tq), lambda bi, hi, p, qt, kt: (bi, hi, 0, qt[p])),
                  pl.BlockSpec((1, 1, DA_VD, tq), lambda bi, hi, p, qt, kt: (bi, hi, 0, kt[p])),
                  pl.BlockSpec((1, 1, DA_VD, tq), lambda bi, hi, p, qt, kt: (bi, hi, 0, kt[p]))],
        out_specs=pl.BlockSpec((1, 1, tq, DA_VD), lambda bi, hi, p, qt, kt: (bi, hi, qt[p], 0)),
        scratch_shapes=[pltpu.VMEM((DA_VD, 2 * tq), BF16), pltpu.VMEM((1, 2 * tq), F32),
                        pltpu.VMEM((DA_VD + ATT_PAD, 2 * tq), F32)])
    return pl.pallas_call(
        _attn_prompt_kernel,
        grid_spec=grid_spec,
        out_shape=jax.ShapeDtypeStruct((b, h, t, DA_VD), F32),
        compiler_params=_params("parallel", "parallel", "arbitrary"),
        name="attn_prompt",
    )(qi_tab, ki_tab, lam, qh, kh, vh)


def _attn_sample_kernel(pt_ref, lam_ref, q_ref, kn_ref, vn_ref, *rest, t_new, n_pages):
    del pt_ref
    pp = PAGES_PER_STEP
    k_refs = rest[0:pp]
    v_refs = rest[pp:2 * pp]
    o_ref, qrows_ref, m_ref, l_ref, acc_ref = rest[2 * pp:]
    j = pl.program_id(1)
    nrow = 2 * t_new * 8

    def update(s, values, mm):
        m_old = m_ref[...]
        m_new = jnp.maximum(m_old, jnp.max(s, axis=1, keepdims=True))
        alpha = jnp.exp2(m_old - m_new)
        pm = jnp.exp2(s - m_new)
        l_ref[...] = alpha * l_ref[...] + jnp.sum(pm, axis=1, keepdims=True)
        acc = alpha * acc_ref[...]
        width = s.shape[1] // len(values)
        for idx, v in enumerate(values):
            acc = acc + mm(pm[:, idx * width:(idx + 1) * width], v)
        acc_ref[...] = acc
        m_ref[...] = m_new

    @pl.when(j == 0)
    def _():
        q = q_ref[0]
        sub = lax.broadcasted_iota(jnp.int32, (8, DA_W), 0)
        lane = lax.broadcasted_iota(jnp.int32, (8, DA_W), 1)
        for mp in range(2):
            keep = ((lane >> 6) == sub) & (((lane >> 5) & 1) == mp)
            for qi in range(t_new):
                r0 = mp * t_new * 8 + qi * 8
                qb = jnp.broadcast_to(q[qi:qi + 1, :], (8, DA_W))
                qrows_ref[r0:r0 + 8, :] = jnp.where(keep, qb, 0.0).astype(BF16)
        m_ref[...] = jnp.full(m_ref.shape, NEG_BIG, F32)
        l_ref[...] = jnp.zeros(l_ref.shape, F32)
        acc_ref[...] = jnp.zeros(acc_ref.shape, F32)
        pad = jnp.zeros((16 - t_new, DA_W), F32)
        k8 = jnp.concatenate([kn_ref[0], pad], axis=0)
        v8 = jnp.concatenate([vn_ref[0], pad], axis=0)
        s = _mm_nt(qrows_ref[...], k8)
        key = lax.broadcasted_iota(jnp.int32, s.shape, 1)
        qidx = (lax.broadcasted_iota(jnp.int32, s.shape, 0) >> 3) & (t_new - 1)
        s = jnp.where(key <= qidx, s, NEG_BIG)
        update(s, [v8], _mm)

    def pairs(refs):
        return [jnp.concatenate([refs[i][...].astype(BF16), refs[i + 1][...].astype(BF16)], axis=1)
                for i in range(0, pp, 2)]

    s_all = jnp.concatenate([_mm(qrows_ref[...], kp) for kp in pairs(k_refs)], axis=1)
    update(s_all, pairs(v_refs), _mm_nt)

    @pl.when(j == n_pages // pp - 1)
    def _():
        o = acc_ref[...] / l_ref[...]
        half = nrow // 2
        oc = o[0:half] - lam_ref[...] * o[half:]
        hd = lax.broadcasted_iota(jnp.int32, oc.shape, 0) & 7
        lane = lax.broadcasted_iota(jnp.int32, oc.shape, 1)
        oc = jnp.where((lane >> 6) == hd, oc, 0.0)
        o_ref[0] = jnp.sum(oc.reshape(t_new, 8, DA_W), axis=1)


def _attn_sample_call(q, k_new, v_new, cache_k, cache_v, page_table, lam, layer):
    nseq, t_new, _ = q.shape
    n_pages = page_table.shape[1]
    pp = PAGES_PER_STEP
    nrow = 2 * t_new * 8
    pt_flat = page_table.reshape(-1).astype(jnp.int32)

    def page_spec(idx):
        return pl.BlockSpec((None, None, DA_W, PAGE_SIZE),
                            lambda bi, j, pt: (layer, pt[bi * n_pages + j * pp + idx], 0, 0))

    tok_spec = pl.BlockSpec((1, t_new, DA_W), lambda bi, j, pt: (bi, 0, 0))
    grid_spec = pltpu.PrefetchScalarGridSpec(
        num_scalar_prefetch=1,
        grid=(nseq, n_pages // pp),
        in_specs=[pl.BlockSpec((1, 1), lambda bi, j, pt: (0, 0)), tok_spec, tok_spec, tok_spec]
        + [page_spec(i) for i in range(pp)] + [page_spec(i) for i in range(pp)],
        out_specs=tok_spec,
        scratch_shapes=[pltpu.VMEM((nrow, DA_W), BF16), pltpu.VMEM((nrow, 1), F32),
                        pltpu.VMEM((nrow, 1), F32), pltpu.VMEM((nrow, DA_W), F32)])
    return pl.pallas_call(
        functools.partial(_attn_sample_kernel, t_new=t_new, n_pages=n_pages),
        grid_spec=grid_spec,
        out_shape=jax.ShapeDtypeStruct((nseq, t_new, DA_W), F32),
        compiler_params=_params("parallel", "arbitrary"),
        name="attn_sample",
    )(pt_flat, lam, q, k_new, v_new, *([cache_k] * pp), *([cache_v] * pp))


def _mix_kernel(x_ref, gate_ref, odn_ref, dng_ref, ossm_ref, oda_ref, dag_ref, onorm_ref, subln_ref, bd_ref,
                w_ref, y_ref, *, head_major):
    if head_major:
        odn = jnp.concatenate([odn_ref[0, hd] for hd in range(DN_HEADS)], axis=1)
        oda = jnp.concatenate([oda_ref[0, hd] for hd in range(DA_HEADS)], axis=1)
    else:
        odn = odn_ref[0]
        oda = oda_ref[0]

    def head_norm(o, gain):
        ms = _mm_split_lhs(o * o, bd_ref[...]) * (1.0 / DN_DV)
        return o * lax.rsqrt(ms + NORM_EPS) * gain

    a = head_norm(odn, onorm_ref[...]) * _silu(dng_ref[0])
    c = head_norm(oda, subln_ref[...]) * _silu(dag_ref[0])
    mixed = (jnp.dot(a.astype(BF16), w_ref[0, 0:DN_W, :], preferred_element_type=F32)
             + jnp.dot(ossm_ref[0].astype(BF16), w_ref[0, DN_W:DN_W + SSM_W, :], preferred_element_type=F32)
             + jnp.dot(c.astype(BF16), w_ref[0, DN_W + SSM_W:, :], preferred_element_type=F32))
    y_ref[0] = x_ref[0] + gate_ref[0] * mixed


def _mix_call(x, gate, odn, dng, ossm, oda, dag, onorm_row, subln_row, w_out_bf16, layer, head_major):
    b, t, _ = x.shape
    tm = min(TM_PROJ, t)
    per_row = gate.shape[1] != 1
    tmm = tm if per_row else 1
    mod_map = (lambda bi, i: (bi, i, 0)) if per_row else (lambda bi, i: (bi, 0, 0))
    row_map = lambda bi, i: (bi, i, 0)
    if head_major:
        o_spec = pl.BlockSpec((1, DN_HEADS, tm, DN_DV), lambda bi, i: (bi, 0, i, 0))
    else:
        o_spec = pl.BlockSpec((1, tm, DN_W), row_map)
    c = jnp.arange(DN_W)
    bd = ((c[:, None] // DN_DV) == (c[None, :] // DN_DV)).astype(BF16)
    return pl.pallas_call(
        functools.partial(_mix_kernel, head_major=head_major),
        grid=(b, t // tm),
        in_specs=[pl.BlockSpec((1, tm, D_MODEL), row_map),
                  pl.BlockSpec((1, tmm, D_MODEL), mod_map),
                  o_spec,
                  pl.BlockSpec((1, tm, DN_W), row_map),
                  pl.BlockSpec((1, tm, SSM_W), row_map),
                  o_spec,
                  pl.BlockSpec((1, tm, DA_W), row_map),
                  pl.BlockSpec((1, DN_W), lambda bi, i: (0, 0)),
                  pl.BlockSpec((1, DA_W), lambda bi, i: (0, 0)),
                  pl.BlockSpec((DN_W, DN_W), lambda bi, i: (0, 0)),
                  pl.BlockSpec((1, MIX_W, D_MODEL), lambda bi, i: (layer, 0, 0))],
        out_specs=pl.BlockSpec((1, tm, D_MODEL), row_map),
        out_shape=jax.ShapeDtypeStruct((b, t, D_MODEL), F32),
        compiler_params=_params("parallel", "parallel"),
        name="mix",
    )(x, gate, odn, dng, ossm, oda, dag, onorm_row, subln_row, bd, w_out_bf16)


def _final_norm_kernel(x_ref, g_ref, o_ref):
    x = x_ref[0]
    ms = jnp.mean(x * x, axis=-1, keepdims=True)
    o_ref[0] = x * lax.rsqrt(ms + NORM_EPS) * g_ref[...]


def _final_norm_call(x, g):
    b, t, _ = x.shape
    tm = min(1024, t)
    return pl.pallas_call(
        _final_norm_kernel,
        grid=(b, t // tm),
        in_specs=[pl.BlockSpec((1, tm, D_MODEL), lambda bi, i: (bi, i, 0)),
                  pl.BlockSpec((1, D_MODEL), lambda bi, i: (0, 0))],
        out_specs=pl.BlockSpec((1, tm, D_MODEL), lambda bi, i: (bi, i, 0)),
        out_shape=jax.ShapeDtypeStruct((b, t, D_MODEL), F32),
        compiler_params=_params("parallel", "parallel"),
        name="final_norm",
    )(x, g)


def _permute_w_in(w_in):
    splits = (DN_CONV_CH, DN_HEADS, DN_HEADS, DN_W, SSM_W, SSM_W, DA_W, DA_W, DA_W, DA_W)
    offs = [0]
    for n in splits:
        offs.append(offs[-1] + n)
    qkv, dnb, dna, dng, su, sg, q, k, v, dag = [w_in[..., offs[i]:offs[i + 1]] for i in range(10)]
    pad = jnp.zeros(w_in.shape[:-1] + (128 - 2 * DN_HEADS,), w_in.dtype)
    return jnp.concatenate([qkv, dng, su, sg, q, k, v, dag, dnb, dna, pad], axis=-1).astype(BF16)


def _rope_tables(pos):
    half = DA_HD // 2
    inv = jnp.power(ROPE_THETA, -jnp.arange(half, dtype=F32) * 2.0 / DA_HD)
    ang = pos.astype(F32)[:, None] * inv[None, :]
    cos = jnp.tile(jnp.cos(ang), (1, 128 // half))
    sin = jnp.tile(jnp.sin(ang), (1, 128 // half))
    sign = jnp.where((jnp.arange(128) & half) == 0, -1.0, 1.0).astype(F32)
    return cos, sin * sign[None, :]


def _s5_params(lam_re, lam_im, log_dt, b_re, b_im, c_re, c_im, d_skip, w_glu, b_glu):
    g, p, cg = SSM_GROUPS, SSM_P, SSM_GROUP_CH
    lam = lax.complex(lam_re.astype(F32), lam_im.astype(F32))
    dt = jnp.exp(log_dt.astype(F32))[:, None]
    lam_bar = jnp.exp(lam * dt)
    b_bar = ((lam_bar - 1.0) / lam)[..., None] * lax.complex(b_re.astype(F32), b_im.astype(F32))
    eye = jnp.eye(g, dtype=F32)
    b_t = jnp.transpose(b_bar, (0, 2, 1))
    bb_re = jnp.einsum("gcp,gh->gchp", jnp.real(b_t), eye).reshape(g * cg, g * p)
    bb_im = jnp.einsum("gcp,gh->gchp", jnp.imag(b_t), eye).reshape(g * cg, g * p)
    bblk = jnp.concatenate([bb_re, bb_im], axis=1).astype(BF16)
    c_t_re = jnp.transpose(c_re.astype(F32), (0, 2, 1))
    c_t_im = jnp.transpose(c_im.astype(F32), (0, 2, 1))
    cc_re = jnp.einsum("gpc,gh->gphc", c_t_re, eye).reshape(g * p, g * cg)
    cc_im = jnp.einsum("gpc,gh->gphc", c_t_im, eye).reshape(g * p, g * cg)
    cblk = jnp.concatenate([cc_re, -cc_im], axis=0).astype(BF16)
    a = lam_bar.reshape(1, g * p)
    steps = jnp.arange(1, S5_SUB + 1, dtype=F32)[:, None]
    pw = jnp.exp((lam * dt).reshape(1, g * p) * steps)
    ap = pw[S5_SUB - 1:S5_SUB]
    return {"bblk": bblk, "cblk": cblk,
            "ar": jnp.real(a), "ai": jnp.imag(a),
            "apr": jnp.real(ap), "api": jnp.imag(ap),
            "pw": jnp.repeat(jnp.concatenate([jnp.real(pw), jnp.imag(pw)], axis=1), 8, axis=0),
            "d": d_skip.astype(F32).reshape(1, SSM_W),
            "wglu": w_glu.astype(BF16), "bglu": b_glu.astype(F32).reshape(1, SSM_W)}


def _s5_block_order(a, outer, inner):
    b, t, w = a.shape
    a = a.reshape(b, t // (outer * inner), outer, inner, w)
    return jnp.transpose(a, (0, 1, 3, 2, 4)).reshape(b, t, w)


def _lane_row(vals, offset):
    return jnp.zeros((1, 128), F32).at[0, offset:offset + vals.shape[0]].set(vals.astype(F32))


def kernel(x_prompt, x_sample, c_prompt, c_sample, cache_k, cache_v, page_table, state_conv, state_delta, state_ssm_re, state_ssm_im, norm_g, w_ada, b_ada, w_in, conv_w, dn_a_log, dn_dt_bias, dn_onorm, ssm_lam_re, ssm_lam_im, ssm_log_dt, ssm_b_re, ssm_b_im, ssm_c_re, ssm_c_im, ssm_d, ssm_w_glu, ssm_b_glu, da_lam_q1, da_lam_k1, da_lam_q2, da_lam_k2, da_subln, w_out, final_g):
    bp, tp, _ = x_prompt.shape
    bs, ts, _ = x_sample.shape
    n_pages = page_table.shape[1]
    past = n_pages * PAGE_SIZE
    n_pool = cache_k.shape[1]
    rs = bs * ts

    n_c = bp + bs
    c_rows = -(-n_c // 8) * 8
    c_all = jnp.concatenate([c_prompt, c_sample, jnp.zeros((c_rows - n_c, D_MODEL), F32)], axis=0)
    mods = _ada_call(c_all, w_ada, b_ada)

    w_perm = _permute_w_in(w_in)
    w_out_bf = w_out.astype(BF16)
    cos_p, sin_p = _rope_tables(jnp.arange(tp, dtype=jnp.int32))
    cos_s, sin_s = _rope_tables(past + (jnp.arange(rs, dtype=jnp.int32) % ts))
    cache_k4 = jnp.transpose(cache_k, (0, 1, 3, 4, 2)).reshape(DEPTH, n_pool, DA_W, PAGE_SIZE)
    cache_v4 = jnp.transpose(cache_v, (0, 1, 3, 4, 2)).reshape(DEPTH, n_pool, DA_W, PAGE_SIZE)
    conv_w8 = jnp.concatenate([conv_w, jnp.zeros((DEPTH, 8 - CONV_K, DN_CONV_CH), F32)], axis=1)
    zero_buf = jnp.zeros((bp, 8, DN_CONV_CH), F32)
    zero_delta = jnp.zeros((bp, DN_HEADS, DN_DK, DN_DV), F32)
    zero_h = jnp.zeros((bp, 1, SSM_N), F32)

    xp = x_prompt
    xs = x_sample.reshape(1, rs, D_MODEL)
    outs = {k: [] for k in ("kp", "vp", "ks", "vs", "cp", "cs", "dp", "ds", "hrp", "hip", "hrs", "his")}
    for l in range(DEPTH):
        lam_init = 0.8 - 0.6 * math.exp(-0.3 * l)
        lam = (jnp.exp(jnp.sum(da_lam_q1[l].astype(F32) * da_lam_k1[l].astype(F32)))
               - jnp.exp(jnp.sum(da_lam_q2[l].astype(F32) * da_lam_k2[l].astype(F32))) + lam_init).reshape(1, 1)
        g_row = norm_g[l].reshape(1, D_MODEL)
        alog_row = _lane_row(dn_a_log[l], DN_HEADS)
        dtb_row = _lane_row(dn_dt_bias[l], DN_HEADS)
        onorm_row = jnp.tile(dn_onorm[l].astype(F32), DN_HEADS).reshape(1, DN_W)
        subln_row = (jnp.tile(da_subln[l].astype(F32), DA_HEADS) * (1.0 - lam_init)).reshape(1, DA_W)
        sp = _s5_params(ssm_lam_re[l], ssm_lam_im[l], ssm_log_dt[l], ssm_b_re[l], ssm_b_im[l],
                        ssm_c_re[l], ssm_c_im[l], ssm_d[l], ssm_w_glu[l], ssm_b_glu[l])

        mp = mods[l, 0:bp].reshape(bp, 1, 3 * D_MODEL)
        shift, scale, gate = mp[..., 0:D_MODEL], mp[..., D_MODEL:2 * D_MODEL], mp[..., 2 * D_MODEL:]
        qkv, dng, su, sg, dag, sm, qh, kh, vh = _inproj_call(
            xp, scale, shift, g_row, cos_p, sin_p, w_perm, l, True)
        odn, s_fin = _delta_call(qkv, zero_buf, conv_w8[l], sm, alog_row, dtb_row,
                                 min(DN_ROWS, tp), DN_CHUNK, (0, DN_CHUNK), zero_delta)
        ossm, hr, hi = _s5_call(_s5_block_order(su, 8, S5_SUB), _s5_block_order(sg, 8, S5_SUB),
                                zero_h, zero_h, sp)
        ossm = _s5_block_order(ossm, S5_SUB, 8)
        oda = _attn_prompt_call(qh, kh, vh, lam)
        xp = _mix_call(xp, gate, odn, dng, ossm, oda, dag, onorm_row, subln_row, w_out_bf, l, True)
        outs["kp"].append(jnp.transpose(kh, (0, 3, 1, 2)))
        outs["vp"].append(jnp.transpose(vh, (0, 3, 1, 2)))
        outs["cp"].append(qkv[:, tp - (CONV_K - 1):, :])
        outs["dp"].append(s_fin)
        outs["hrp"].append(hr.reshape(bp, SSM_GROUPS, SSM_P))
        outs["hip"].append(hi.reshape(bp, SSM_GROUPS, SSM_P))

        ms_ = jnp.repeat(mods[l, bp:bp + bs], ts, axis=0).reshape(1, rs, 3 * D_MODEL)
        shift, scale, gate = ms_[..., 0:D_MODEL], ms_[..., D_MODEL:2 * D_MODEL], ms_[..., 2 * D_MODEL:]
        qkv, dng, su, sg, dag, sm, qrow, kf, vf = _inproj_call(
            xs, scale, shift, g_row, cos_s, sin_s, w_perm, l, False)
        qkv_seq = qkv.reshape(bs, ts, DN_CONV_CH)
        nb = CONV_K - 1
        pad_t = DN_CHUNK_S - nb - ts
        qkv_cat = jnp.pad(jnp.concatenate([state_conv[l], qkv_seq], axis=1), ((0, 0), (0, pad_t), (0, 0)))
        sm_cat = jnp.pad(sm.reshape(bs, ts, 128), ((0, 0), (nb, pad_t), (0, 0)))
        u, w, qd, kd, at, gc = _delta_call(
            qkv_cat.reshape(1, bs * DN_CHUNK_S, DN_CONV_CH), zero_buf[0:1], conv_w8[l],
            sm_cat.reshape(1, bs * DN_CHUNK_S, 128), alog_row, dtb_row,
            min(DN_ROWS, bs * DN_CHUNK_S), DN_CHUNK_S, (nb, nb + ts))
        odn, s_fin = _delta_step_call(u, w, qd, kd, at, gc, state_delta[l], DN_CHUNK_S)
        odn = odn.reshape(DN_HEADS, bs, DN_CHUNK_S, DN_DV)[:, :, nb:nb + ts, :]
        odn = jnp.transpose(odn, (1, 2, 0, 3)).reshape(1, rs, DN_W)
        su_t = jnp.transpose(su.reshape(bs, ts, SSM_W), (1, 0, 2))
        sg_t = jnp.transpose(sg.reshape(bs, ts, SSM_W), (1, 0, 2))
        ossm, hr, hi = _s5_step_call(su_t, sg_t, state_ssm_re[l].reshape(bs, SSM_N),
                                     state_ssm_im[l].reshape(bs, SSM_N), sp, bs, ts)
        ossm = jnp.transpose(ossm, (1, 0, 2))
        oda = _attn_sample_call(qrow.reshape(bs, ts, DA_W), kf.reshape(bs, ts, DA_W), vf.reshape(bs, ts, DA_W),
                                cache_k4, cache_v4, page_table, lam, l)
        xs = _mix_call(xs, gate, odn, dng, ossm.reshape(1, rs, SSM_W), oda.reshape(1, rs, DA_W), dag,
                       onorm_row, subln_row, w_out_bf, l, False)
        xp_conv = jnp.concatenate([state_conv[l], qkv_seq], axis=1)
        outs["ks"].append(kf.reshape(bs, ts, DA_HEADS, DA_VD))
        outs["vs"].append(vf.reshape(bs, ts, DA_HEADS, DA_VD))
        outs["cs"].append(xp_conv[:, xp_conv.shape[1] - (CONV_K - 1):, :])
        outs["ds"].append(s_fin)
        outs["hrs"].append(hr.reshape(bs, SSM_GROUPS, SSM_P))
        outs["his"].append(hi.reshape(bs, SSM_GROUPS, SSM_P))

    fg = final_g.reshape(1, D_MODEL)
    y_prompt = _final_norm_call(xp, fg)
    y_sample = _final_norm_call(xs, fg).reshape(bs, ts, D_MODEL)
    st = {k: jnp.stack(v) for k, v in outs.items()}
    return (y_prompt, y_sample, st["kp"], st["vp"], st["ks"], st["vs"], st["cp"], st["cs"],
            st["dp"], st["ds"], st["hrp"], st["hip"], st["hrs"], st["his"])
```

```python
import functools
import math

import jax
import jax.numpy as jnp
from jax import lax
from jax.experimental import pallas as pl
from jax.experimental.pallas import tpu as pltpu

F32 = jnp.float32
BF16 = jnp.bfloat16

D_MODEL = 1024
DEPTH = 4
PAGE_SIZE = 128
DN_HEADS = 6
DN_DK = 64
DN_DV = 64
DN_W = DN_HEADS * DN_DV
DN_CONV_CH = 2 * DN_HEADS * DN_DK + DN_W
CONV_K = 4
DN_CHUNK = 64
SSM_GROUPS = 16
SSM_GROUP_CH = 16
SSM_W = SSM_GROUPS * SSM_GROUP_CH
SSM_P = 64
SSM_N = SSM_GROUPS * SSM_P
DA_HEADS = 6
DA_HD = 32
DA_VD = 2 * DA_HD
DA_W = DA_HEADS * DA_VD
MIX_W = DN_W + SSM_W + DA_W
ROPE_THETA = 10000.0
NORM_EPS = 1e-6
NEG_BIG = -1e30

SEG_QKV = (0, 1152)
SEG_DNG = (1152, 1536)
SEG_SU = (1536, 1792)
SEG_SG = (1792, 2048)
SEG_Q = (2048, 2432)
SEG_K = (2432, 2816)
SEG_V = (2816, 3200)
SEG_DAG = (3200, 3584)
SEG_SM = (3584, 3712)
IN_W_PAD = 3712

TM_PROJ = 512
DN_ROWS = 256
DN_GROUP = 128
DN_CHUNK_S = 16
S5_ROWS = 512
S5_SUB = 64
S5_SLABS = SSM_N // 128
TQ = 2048
TK_SUB = 512
PAGES_PER_STEP = 16
ATT_PAD = 16
VMEM_LIMIT = 56 * 1024 * 1024

Q_SCALE = (DA_HD ** -0.5) * math.log2(math.e)


def _mm(a, b):
    return jnp.dot(a.astype(BF16), b.astype(BF16), preferred_element_type=F32)


def _mm_nt(a, b):
    return lax.dot_general(a.astype(BF16), b.astype(BF16), (((1,), (1,)), ((), ())),
                           preferred_element_type=F32)


def _mm_tn(a, b):
    return lax.dot_general(a.astype(BF16), b.astype(BF16), (((0,), (0,)), ((), ())),
                           preferred_element_type=F32)


def _split(a):
    hi = a.astype(BF16)
    lo = (a - hi.astype(F32)).astype(BF16)
    return hi, lo


def _mm_split_lhs(a, b_bf16):
    hi, lo = _split(a)
    return (jnp.dot(hi, b_bf16, preferred_element_type=F32)
            + jnp.dot(lo, b_bf16, preferred_element_type=F32))


def _mm3(a, b):
    ah, al = _split(a)
    bh, bl = _split(b)
    return (jnp.dot(ah, bh, preferred_element_type=F32)
            + jnp.dot(ah, bl, preferred_element_type=F32)
            + jnp.dot(al, bh, preferred_element_type=F32))


def _silu(x):
    return x * jax.nn.sigmoid(x)


def _softplus(x):
    return jnp.maximum(x, 0.0) + jnp.log1p(jnp.exp(-jnp.abs(x)))


def _gelu_tanh(x):
    c = math.sqrt(2.0 / math.pi)
    return x * (0.5 * (1.0 + jnp.tanh(c * (x + 0.044715 * (x * x * x)))))


def _params(*sem):
    return pltpu.CompilerParams(dimension_semantics=sem, vmem_limit_bytes=VMEM_LIMIT)


def _ada_kernel(c_ref, w_ref, b_ref, o_ref):
    c = c_ref[...]
    o_ref[0] = _mm3(_silu(c), w_ref[0]) + b_ref[0]


def _ada_call(c_all, w_ada, b_ada):
    rows = c_all.shape[0]
    tn = 1024
    return pl.pallas_call(
        _ada_kernel,
        grid=(DEPTH, 3 * D_MODEL // tn),
        in_specs=[pl.BlockSpec((rows, D_MODEL), lambda l, n: (0, 0)),
                  pl.BlockSpec((1, D_MODEL, tn), lambda l, n: (l, 0, n)),
                  pl.BlockSpec((1, 1, tn), lambda l, n: (l, 0, n))],
        out_specs=pl.BlockSpec((1, rows, tn), lambda l, n: (l, 0, n)),
        out_shape=jax.ShapeDtypeStruct((DEPTH, rows, 3 * D_MODEL), F32),
        compiler_params=_params("parallel", "parallel"),
        name="adaln",
    )(c_all, w_ada, b_ada.reshape(DEPTH, 1, 3 * D_MODEL))


def _rope(x, cos, sin):
    lane = lax.broadcasted_iota(jnp.int32, cos.shape, 1)
    low = (lane & 16) == 0
    outs = []
    for c in range(x.shape[1] // 128):
        xc = x[:, c * 128:(c + 1) * 128]
        sw = jnp.where(low, pltpu.roll(xc, 112, 1), pltpu.roll(xc, 16, 1))
        outs.append(xc * cos + sw * sin)
    return jnp.concatenate(outs, axis=1)


def _inproj_kernel(x_ref, sc_ref, sh_ref, g_ref, cos_ref, sin_ref, w_ref, *outs, head_major):
    x = x_ref[0]
    ms = jnp.mean(x * x, axis=-1, keepdims=True)
    h = x * lax.rsqrt(ms + NORM_EPS) * g_ref[...] * (1.0 + sc_ref[0]) + sh_ref[0]
    hb = h.astype(BF16)

    def seg(ab):
        return jnp.dot(hb, w_ref[0, :, ab[0]:ab[1]], preferred_element_type=F32)

    qkv_o, dng_o, su_o, sg_o, dag_o, sm_o, q_o, k_o, v_o = outs
    cos = cos_ref[...]
    sin = sin_ref[...]

    def emit(o_ref, val, dt):
        if head_major:
            for hd in range(DA_HEADS):
                o_ref[0, hd] = val[:, hd * DA_VD:(hd + 1) * DA_VD].T.astype(dt)
        else:
            o_ref[0] = val

    emit(q_o, _rope(seg(SEG_Q), cos, sin) * Q_SCALE, BF16)
    qkv_o[0] = seg(SEG_QKV)
    emit(k_o, _rope(seg(SEG_K), cos, sin), F32)
    dng_o[0] = seg(SEG_DNG)
    su_o[0] = seg(SEG_SU)
    emit(v_o, seg(SEG_V), F32)
    sg_o[0] = seg(SEG_SG)
    dag_o[0] = seg(SEG_DAG)
    sm_o[0] = seg(SEG_SM)


def _inproj_call(x, scale, shift, norm_g, cos, sin, w_perm, layer, head_major):
    b, t, _ = x.shape
    tm = min(TM_PROJ, t)
    per_row = scale.shape[1] != 1
    tmm = tm if per_row else 1
    mod_map = (lambda bi, i: (bi, i, 0)) if per_row else (lambda bi, i: (bi, 0, 0))
    row_map = lambda bi, i: (bi, i, 0)

    def row_spec(w):
        return pl.BlockSpec((1, tm, w), row_map)

    def row_shape(w):
        return jax.ShapeDtypeStruct((b, t, w), F32)

    out_specs = [row_spec(1152), row_spec(384), row_spec(256), row_spec(256), row_spec(384), row_spec(128)]
    out_shape = [row_shape(1152), row_shape(384), row_shape(256), row_shape(256), row_shape(384), row_shape(128)]
    if head_major:
        for dt in (BF16, F32, F32):
            out_specs.append(pl.BlockSpec((1, DA_HEADS, DA_VD, tm), lambda bi, i: (bi, 0, 0, i)))
            out_shape.append(jax.ShapeDtypeStruct((b, DA_HEADS, DA_VD, t), dt))
    else:
        out_specs += [row_spec(384)] * 3
        out_shape += [row_shape(384)] * 3
    return pl.pallas_call(
        functools.partial(_inproj_kernel, head_major=head_major),
        grid=(b, t // tm),
        in_specs=[row_spec(D_MODEL),
                  pl.BlockSpec((1, tmm, D_MODEL), mod_map),
                  pl.BlockSpec((1, tmm, D_MODEL), mod_map),
                  pl.BlockSpec((1, D_MODEL), lambda bi, i: (0, 0)),
                  pl.BlockSpec((tm, 128), lambda bi, i: (i, 0)),
                  pl.BlockSpec((tm, 128), lambda bi, i: (i, 0)),
                  pl.BlockSpec((1, D_MODEL, IN_W_PAD), lambda bi, i: (layer, 0, 0))],
        out_specs=out_specs,
        out_shape=out_shape,
        compiler_params=_params("parallel", "parallel"),
        name="inproj",
    )(x, scale, shift, norm_g, cos, sin, w_perm)


def _delta_kernel(x_ref, prev_ref, buf_ref, cw_ref, sm_ref, alog_ref, dtb_ref, ltri_ref, last_ref,
                  fold_ref, bd_ref, *rest, rows, chunk, valid, scan):
    if scan:
        s0_ref, o_ref, sf_ref, xs_ref, s_ref = rest
    else:
        u_o, w_o, qd_o, kd_o, at_o, gc_o, xs_ref = rest
    i = pl.program_id(1)

    if scan:
        @pl.when(i == 0)
        def _():
            zero = jnp.zeros((DN_DK, DN_DV), F32)
            for p in range(DN_HEADS // 2):
                s_ref[p] = jnp.concatenate(
                    [jnp.concatenate([s0_ref[0, 2 * p], zero], axis=1),
                     jnp.concatenate([zero, s0_ref[0, 2 * p + 1]], axis=1)], axis=0)
    halo = jnp.where(i == 0, buf_ref[0], prev_ref[0])
    xs_ref[0:8, :] = halo
    xs_ref[8:8 + rows, :] = x_ref[0]
    cw = cw_ref[...]
    y = (xs_ref[pl.ds(5, rows), :] * cw[0:1] + xs_ref[pl.ds(6, rows), :] * cw[1:2]
         + xs_ref[pl.ds(7, rows), :] * cw[2:3] + xs_ref[pl.ds(8, rows), :] * cw[3:4])
    y = _silu(y)
    qk = y[:, 0:2 * DN_W]
    ss = _mm_split_lhs(qk * qk, bd_ref[...])
    qkn = qk * lax.rsqrt(ss + NORM_EPS)
    v_all = y[:, 2 * DN_W:]

    sm = sm_ref[0]
    pos = lax.broadcasted_iota(jnp.int32, sm.shape, 0) & (chunk - 1)
    real = (pos >= valid[0]) & (pos < valid[1])
    beta = jnp.where(real, jax.nn.sigmoid(sm), 0.0)
    g = jnp.where(real, -jnp.exp(alog_ref[...]) * _softplus(sm + dtb_ref[...]), 0.0)
    gc = _mm_split_lhs_rhs(ltri_ref[...], g)
    gc_last = _mm_split_lhs_rhs(last_ref[...], gc)
    if not scan:
        gc_o[0] = gc
    gc_t = gc.T

    grp = min(rows, DN_GROUP)
    n_grp = rows // grp
    per_grp = grp // chunk
    shift = chunk.bit_length() - 1
    ri = lax.broadcasted_iota(jnp.int32, (grp, grp), 0)
    ci = lax.broadcasted_iota(jnp.int32, (grp, grp), 1)
    same_chunk = (ri >> shift) == (ci >> shift)
    causal = same_chunk & (ri >= ci)
    strict_b = jnp.where(same_chunk & (ri > ci), 1.0, 0.0).astype(BF16)
    eye_f = jnp.where(ri == ci, 1.0, 0.0)
    level_b = []
    s = 1
    while s < chunk:
        sh = s.bit_length()
        m = ((ri >> sh) == (ci >> sh)) & ((ri & s) != 0) & ((ci & s) == 0)
        level_b.append(jnp.where(m, 1.0, 0.0).astype(BF16))
        s *= 2
    scale = DN_DK ** -0.5
    n_pair = DN_HEADS // 2
    first_half = lax.broadcasted_iota(jnp.int32, (grp, 128), 1) < DN_DK

    probs = []
    for g in range(n_grp):
        gr = slice(g * grp, (g + 1) * grp)
        for p in range(n_pair):
            q_pair = qkn[gr, p * 128:(p + 1) * 128] * scale
            k_pair = qkn[gr, DN_W + p * 128:DN_W + (p + 1) * 128]
            v_pair = v_all[gr, p * 128:(p + 1) * 128]
            k_pair_b = k_pair.astype(BF16)
            for hh in range(2):
                hd = 2 * p + hh
                own = first_half if hh == 0 else jnp.logical_not(first_half)
                qh = jnp.where(own, q_pair, 0.0)
                kh = jnp.where(own, k_pair, 0.0)
                vh = jnp.where(own, v_pair, 0.0)
                beta_c = beta[gr, hd:hd + 1]
                gcol = gc[gr, DN_HEADS + hd:DN_HEADS + hd + 1]
                grow = gc_t[DN_HEADS + hd:DN_HEADS + hd + 1, gr]
                decay = jnp.exp(jnp.where(causal, gcol - grow, NEG_BIG))
                kb = kh * beta_c
                m_b = (_mm_nt(kb, k_pair_b) * decay).astype(BF16) * strict_b
                probs.append((g, p, hh, qh, kh, vh, k_pair_b, beta_c, gcol, decay, kb, m_b))
    x_invs = [eye_f - (pr[11] * level_b[0]).astype(F32) for pr in probs]
    for lb in level_b[1:]:
        for n, pr in enumerate(probs):
            x_b = x_invs[n].astype(BF16)
            t_b = jnp.dot(x_b, pr[11] * lb, preferred_element_type=F32).astype(BF16)
            x_invs[n] = x_invs[n] - jnp.dot(t_b, x_b, preferred_element_type=F32)

    pairs = {}
    for n, (g, p, hh, qh, kh, vh, k_pair_b, beta_c, gcol, decay, kb, _) in enumerate(probs):
        gr = slice(g * grp, (g + 1) * grp)
        hd = 2 * p + hh
        e_g = jnp.exp(gcol)
        sol = _mm(x_invs[n], jnp.concatenate([vh * beta_c, kb * e_g], axis=1))
        attn_full = _mm_nt(qh, k_pair_b) * decay
        u = sol[:, 0:128]
        w = sol[:, 128:256]
        qd = qh * e_g
        gl_col = gc_last[gr, DN_HEADS + hd:DN_HEADS + hd + 1]
        kd = kh * jnp.exp(gl_col - gcol)
        if scan:
            ent = pairs.setdefault((g, p), {"sum": None, "attn": [], "e_last": []})
            ent["sum"] = (u, w, qd, kd) if hh == 0 else tuple(a + b for a, b in zip(ent["sum"], (u, w, qd, kd)))
            ent["attn"].append(attn_full.astype(BF16))
            ent["e_last"].append(jnp.exp(gl_col))
        else:
            hs = slice(hh * DN_DK, (hh + 1) * DN_DK)
            u_o[0, hd, gr, :] = u[:, hs]
            w_o[0, hd, gr, :] = w[:, hs]
            qd_o[0, hd, gr, :] = qd[:, hs]
            kd_o[0, hd, gr, :] = kd[:, hs]
            at_o[0, hd, gr, :] = _mm(attn_full, fold_ref[0:grp, :])

    if scan:
        r2 = lax.broadcasted_iota(jnp.int32, (128, 128), 0) < DN_DK
        c2 = lax.broadcasted_iota(jnp.int32, (128, 128), 1) < DN_DK
        on_diag = r2 == c2
        lane_first = lax.broadcasted_iota(jnp.int32, (1, 128), 1) < DN_DK
        states = [s_ref[p] for p in range(n_pair)]
        for g in range(n_grp):
            v_news = [[] for _ in range(n_pair)]
            o_states = [[] for _ in range(n_pair)]
            for c in range(per_grp):
                rc = slice(c * chunk, (c + 1) * chunk)
                for p in range(n_pair):
                    ent = pairs[(g, p)]
                    u, w, qd, kd = ent["sum"]
                    r = _mm(jnp.concatenate([w[rc], qd[rc]], axis=0), states[p])
                    v_new = u[rc] - r[0:chunk]
                    o_states[p].append(r[chunk:])
                    e_last = jnp.where(lane_first, ent["e_last"][0][c * chunk:c * chunk + 1],
                                       ent["e_last"][1][c * chunk:c * chunk + 1])
                    states[p] = jnp.where(on_diag, states[p] * e_last + _mm_tn(kd[rc], v_new), 0.0)
                    v_news[p].append(v_new)
            for p in range(n_pair):
                vn_b = jnp.concatenate(v_news[p], axis=0).astype(BF16)
                oa = jnp.dot(pairs[(g, p)]["attn"][0], vn_b, preferred_element_type=F32)
                ob = jnp.dot(pairs[(g, p)]["attn"][1], vn_b, preferred_element_type=F32)
                o_ref[0, g * grp:(g + 1) * grp, p * 128:(p + 1) * 128] = (
                    jnp.concatenate(o_states[p], axis=0) + jnp.where(first_half, oa, ob))
        for p in range(n_pair):
            s_ref[p] = states[p]

        @pl.when(i == pl.num_programs(1) - 1)
        def _():
            for p in range(DN_HEADS // 2):
                sf_ref[0, 2 * p] = s_ref[p, 0:DN_DK, 0:DN_DV]
                sf_ref[0, 2 * p + 1] = s_ref[p, DN_DK:, DN_DV:]


def _mm_split_lhs_rhs(a_bf16, b):
    hi, lo = _split(b)
    return (jnp.dot(a_bf16, hi, preferred_element_type=F32)
            + jnp.dot(a_bf16, lo, preferred_element_type=F32))


def _delta_call(qkv, buf8, conv_w8, sm, alog_row, dtb_row, rows, chunk, valid, s0=None):
    b, t, _ = qkv.shape
    nblk = t // rows
    scan = s0 is not None
    r = jnp.arange(rows)
    same = (r[:, None] // chunk) == (r[None, :] // chunk)
    ltri = ((r[:, None] >= r[None, :]) & same).astype(BF16)
    last = (r[None, :] == (r[:, None] | (chunk - 1))).astype(BF16)
    fold = ((r[:, None] & (chunk - 1)) == jnp.arange(DN_CHUNK)[None, :]).astype(BF16)
    c = jnp.arange(2 * DN_W)
    bd = ((c[:, None] // DN_DK) == (c[None, :] // DN_DK)).astype(BF16)
    hm_spec = pl.BlockSpec((1, DN_HEADS, rows, DN_DV), lambda bi, i: (bi, 0, i, 0))
    hm_shape = jax.ShapeDtypeStruct((b, DN_HEADS, t, DN_DV), F32)
    st_spec = pl.BlockSpec((1, DN_HEADS, DN_DK, DN_DV), lambda bi, i: (bi, 0, 0, 0))
    in_specs = [pl.BlockSpec((1, rows, DN_CONV_CH), lambda bi, i: (bi, i, 0)),
                pl.BlockSpec((1, 8, DN_CONV_CH), lambda bi, i: (bi, jnp.maximum(i * (rows // 8) - 1, 0), 0)),
                pl.BlockSpec((1, 8, DN_CONV_CH), lambda bi, i: (bi, 0, 0)),
                pl.BlockSpec((8, DN_CONV_CH), lambda bi, i: (0, 0)),
                pl.BlockSpec((1, rows, 128), lambda bi, i: (bi, i, 0)),
                pl.BlockSpec((1, 128), lambda bi, i: (0, 0)),
                pl.BlockSpec((1, 128), lambda bi, i: (0, 0)),
                pl.BlockSpec((rows, rows), lambda bi, i: (0, 0)),
                pl.BlockSpec((rows, rows), lambda bi, i: (0, 0)),
                pl.BlockSpec((rows, DN_CHUNK), lambda bi, i: (0, 0)),
                pl.BlockSpec((2 * DN_W, 2 * DN_W), lambda bi, i: (0, 0))]
    args = [qkv, qkv, buf8, conv_w8, sm, alog_row, dtb_row, ltri, last, fold, bd]
    scratch = [pltpu.VMEM((rows + 8, DN_CONV_CH), F32)]
    if scan:
        in_specs.append(st_spec)
        args.append(s0)
        out_specs = [pl.BlockSpec((1, rows, DN_W), lambda bi, i: (bi, i, 0)), st_spec]
        out_shape = [jax.ShapeDtypeStruct((b, t, DN_W), F32),
                     jax.ShapeDtypeStruct((b, DN_HEADS, DN_DK, DN_DV), F32)]
        scratch.append(pltpu.VMEM((DN_HEADS // 2, 2 * DN_DK, 2 * DN_DV), F32))
    else:
        out_specs = [hm_spec] * 5 + [pl.BlockSpec((1, rows, 128), lambda bi, i: (bi, i, 0))]
        out_shape = [hm_shape] * 5 + [jax.ShapeDtypeStruct((b, t, 128), F32)]
    return pl.pallas_call(
        functools.partial(_delta_kernel, rows=rows, chunk=chunk, valid=valid, scan=scan),
        grid=(b, nblk),
        in_specs=in_specs,
        out_specs=out_specs,
        out_shape=out_shape,
        scratch_shapes=scratch,
        compiler_params=_params("parallel", "arbitrary" if scan else "parallel"),
        name="delta_scan" if scan else "delta_local",
    )(*args)


def _delta_step_kernel(u_ref, w_ref, qd_ref, kd_ref, at_ref, gc_ref, s0_ref, o_ref, sf_ref, *, bb, chunk):
    for bi in range(bb):
        rows = slice(bi * chunk, (bi + 1) * chunk)
        for hd in range(DN_HEADS):
            s = s0_ref[bi, hd]
            wq = jnp.concatenate([w_ref[0, hd, rows, :], qd_ref[0, hd, rows, :]], axis=0)
            r = _mm(wq, s)
            v_new = u_ref[0, hd, rows, :] - r[0:chunk]
            o_ref[0, hd, rows, :] = r[chunk:] + _mm(at_ref[0, hd, rows, 0:chunk], v_new)
            last = (bi + 1) * chunk - 1
            g_last = jnp.exp(gc_ref[0, last:last + 1, DN_HEADS + hd:DN_HEADS + hd + 1])
            sf_ref[bi, hd] = s * g_last + _mm_tn(kd_ref[0, hd, rows, :], v_new)


def _delta_step_call(u, w, qd, kd, at, gc, s0, chunk):
    nseq = s0.shape[0]
    bb = 4
    hm_spec = pl.BlockSpec((1, DN_HEADS, bb * chunk, DN_DV), lambda i: (0, 0, i, 0))
    st_spec = pl.BlockSpec((bb, DN_HEADS, DN_DK, DN_DV), lambda i: (i, 0, 0, 0))
    return pl.pallas_call(
        functools.partial(_delta_step_kernel, bb=bb, chunk=chunk),
        grid=(nseq // bb,),
        in_specs=[hm_spec] * 5 + [pl.BlockSpec((1, bb * chunk, 128), lambda i: (0, i, 0)), st_spec],
        out_specs=[hm_spec, st_spec],
        out_shape=[jax.ShapeDtypeStruct(u.shape, F32),
                   jax.ShapeDtypeStruct((nseq, DN_HEADS, DN_DK, DN_DV), F32)],
        compiler_params=_params("parallel"),
        name="delta_step",
    )(u, w, qd, kd, at, gc, s0)


def _s5_epilogue(y, u, sg, d_ref, wglu_ref, bglu_ref):
    z = _gelu_tanh(y + d_ref[...] * u)
    gate = jax.nn.sigmoid(_mm(z, wglu_ref[...]) + bglu_ref[...])
    return z * gate * _silu(sg)


def _s5_kernel(u_ref, sg_ref, h0r_ref, h0i_ref, bblk_ref, ar_ref, ai_ref, apr_ref, api_ref, pw_ref,
               cblk_ref, d_ref, wglu_ref, bglu_ref, o_ref, hr_o, hi_o, hs_ref, cr_ref, ci_ref):
    i = pl.program_id(1)
    ns = S5_SLABS

    @pl.when(i == 0)
    def _():
        cr_ref[...] = h0r_ref[0]
        ci_ref[...] = h0i_ref[0]

    def slab(c):
        return slice(c * 128, (c + 1) * 128)

    u = u_ref[0]
    bu = _mm(u, bblk_ref[...])
    for c in range(2 * ns):
        hs_ref[c] = bu[:, slab(c)]
    ar = [jnp.broadcast_to(ar_ref[:, slab(c)], (8, 128)) for c in range(ns)]
    ai = [jnp.broadcast_to(ai_ref[:, slab(c)], (8, 128)) for c in range(ns)]

    def scan_body(j, carry):
        rows = pl.ds(pl.multiple_of(j * 8, 8), 8)
        new = []
        for c in range(ns):
            hr, hi = carry[2 * c], carry[2 * c + 1]
            nr = ar[c] * hr - ai[c] * hi + hs_ref[c, rows, :]
            ni = ar[c] * hi + ai[c] * hr + hs_ref[ns + c, rows, :]
            hs_ref[c, rows, :] = nr
            hs_ref[ns + c, rows, :] = ni
            new += [nr, ni]
        return tuple(new)

    zero = jnp.zeros((8, 128), F32)
    ends = lax.fori_loop(0, S5_SUB, scan_body, (zero,) * (2 * ns))

    h_in = []
    for c in range(ns):
        apr = apr_ref[:, slab(c)]
        api = api_ref[:, slab(c)]
        hr = cr_ref[:, slab(c)]
        hi = ci_ref[:, slab(c)]
        er, ei = ends[2 * c], ends[2 * c + 1]
        rows_r, rows_i = [], []
        for s in range(8):
            rows_r.append(hr)
            rows_i.append(hi)
            nr = apr * hr - api * hi + er[s:s + 1]
            ni = apr * hi + api * hr + ei[s:s + 1]
            hr, hi = nr, ni
        cr_ref[:, slab(c)] = hr
        ci_ref[:, slab(c)] = hi
        h_in += [jnp.concatenate(rows_r, axis=0), jnp.concatenate(rows_i, axis=0)]

    def fix_body(j, carry):
        rows = pl.ds(pl.multiple_of(j * 8, 8), 8)
        for c in range(ns):
            pr = pw_ref[rows, slab(c)]
            pi = pw_ref[rows, slab(ns + c)]
            hr, hi = h_in[2 * c], h_in[2 * c + 1]
            hs_ref[c, rows, :] = hs_ref[c, rows, :] + pr * hr - pi * hi
            hs_ref[ns + c, rows, :] = hs_ref[ns + c, rows, :] + pr * hi + pi * hr
        return carry

    lax.fori_loop(0, S5_SUB, fix_body, 0)

    y = jnp.zeros((S5_ROWS, SSM_W), F32)
    for c in range(2 * ns):
        y = y + _mm(hs_ref[c], cblk_ref[slab(c), :])

    o_ref[0] = _s5_epilogue(y, u, sg_ref[0], d_ref, wglu_ref, bglu_ref)

    @pl.when(i == pl.num_programs(1) - 1)
    def _():
        hr_o[0] = cr_ref[...]
        hi_o[0] = ci_ref[...]


def _s5_call(u, sg, h0r, h0i, sp):
    b, t, _ = u.shape
    n = SSM_N
    row_spec = pl.BlockSpec((1, S5_ROWS, SSM_W), lambda bi, i: (bi, i, 0))
    st_spec = pl.BlockSpec((1, 1, n), lambda bi, i: (bi, 0, 0))
    full = lambda shape: pl.BlockSpec(shape, lambda bi, i: (0,) * len(shape))
    return pl.pallas_call(
        _s5_kernel,
        grid=(b, t // S5_ROWS),
        in_specs=[row_spec, row_spec, st_spec, st_spec,
                  full((SSM_W, 2 * n)), full((1, n)), full((1, n)), full((1, n)), full((1, n)),
                  full((S5_ROWS, 2 * n)), full((2 * n, SSM_W)), full((1, SSM_W)),
                  full((SSM_W, SSM_W)), full((1, SSM_W))],
        out_specs=[row_spec, st_spec, st_spec],
        out_shape=[jax.ShapeDtypeStruct((b, t, SSM_W), F32),
                   jax.ShapeDtypeStruct((b, 1, n), F32), jax.ShapeDtypeStruct((b, 1, n), F32)],
        scratch_shapes=[pltpu.VMEM((2 * S5_SLABS, S5_ROWS, 128), F32), pltpu.VMEM((1, n), F32),
                        pltpu.VMEM((1, n), F32)],
        compiler_params=_params("parallel", "arbitrary"),
        name="s5",
    )(u, sg, h0r, h0i, sp["bblk"], sp["ar"], sp["ai"], sp["apr"], sp["api"], sp["pw"],
      sp["cblk"], sp["d"], sp["wglu"], sp["bglu"])


def _s5_step_kernel(u_ref, sg_ref, h0r_ref, h0i_ref, bblk_ref, ar_ref, ai_ref, cblk_ref, d_ref, wglu_ref,
                    bglu_ref, o_ref, hr_o, hi_o, *, nseq, t):
    hr = h0r_ref[...]
    hi = h0i_ref[...]
    ar = ar_ref[...]
    ai = ai_ref[...]
    n = SSM_N
    for step in range(t):
        u = u_ref[step]
        sg = sg_ref[step]
        bu = _mm(u, bblk_ref[...])
        nr = ar * hr - ai * hi + bu[:, 0:n]
        ni = ar * hi + ai * hr + bu[:, n:]
        hr, hi = nr, ni
        y = _mm(jnp.concatenate([hr, hi], axis=1), cblk_ref[...])
        o_ref[step] = _s5_epilogue(y, u, sg, d_ref, wglu_ref, bglu_ref)
    hr_o[...] = hr
    hi_o[...] = hi


def _s5_step_call(u, sg, h0r, h0i, sp, nseq, t):
    n = SSM_N
    return pl.pallas_call(
        functools.partial(_s5_step_kernel, nseq=nseq, t=t),
        out_shape=[jax.ShapeDtypeStruct((t, nseq, SSM_W), F32),
                   jax.ShapeDtypeStruct((nseq, n), F32), jax.ShapeDtypeStruct((nseq, n), F32)],
        compiler_params=pltpu.CompilerParams(vmem_limit_bytes=VMEM_LIMIT),
        name="s5_step",
    )(u, sg, h0r, h0i, sp["bblk"], sp["ar"], sp["ai"], sp["cblk"], sp["d"], sp["wglu"], sp["bglu"])


def _attn_prompt_kernel(qi_tab, ki_tab, lam_ref, q_ref, k_ref, v_ref, o_ref, qcat_ref, m_ref, acc_ref):
    p = pl.program_id(2)
    qi = qi_tab[p]
    ki = ki_tab[p]
    tq = q_ref.shape[3]

    sub = TK_SUB
    nsub = tq // sub

    @pl.when(ki == 0)
    def _():
        q = q_ref[0, 0]
        feat = lax.broadcasted_iota(jnp.int32, (DA_VD, sub), 0)
        zero = jnp.zeros((DA_VD, sub), q.dtype)
        for blk in range(nsub):
            qb = q[:, blk * sub:(blk + 1) * sub]
            qcat_ref[:, 2 * blk * sub:(2 * blk + 1) * sub] = jnp.where(feat < DA_HD, qb, zero)
            qcat_ref[:, (2 * blk + 1) * sub:(2 * blk + 2) * sub] = jnp.where(feat >= DA_HD, qb, zero)
        m_ref[...] = jnp.full(m_ref.shape, NEG_BIG, F32)
        acc_ref[...] = jnp.zeros(acc_ref.shape, F32)

    def step(diagonal):
        def col_range(c):
            return slice(2 * c * sub, 2 * tq) if diagonal else slice(0, 2 * tq)

        def scores(c):
            return _mm_tn(k_ref[0, 0, :, c * sub:(c + 1) * sub], qcat_ref[:, col_range(c)])

        s_next = scores(0)
        for c in range(nsub):
            keys = slice(c * sub, (c + 1) * sub)
            cols = col_range(c)
            s = s_next
            if c + 1 < nsub:
                s_next = scores(c + 1)
            if diagonal:
                kr = lax.broadcasted_iota(jnp.int32, (sub, 2 * sub), 0)
                qc = lax.broadcasted_iota(jnp.int32, (sub, 2 * sub), 1) & (sub - 1)
                own = jnp.where(kr <= qc, s[:, 0:2 * sub], NEG_BIG)
                s = own if c == nsub - 1 else jnp.concatenate([own, s[:, 2 * sub:]], axis=1)
            m = m_ref[:, cols]
            m_new = jnp.maximum(m, jnp.max(s, axis=0, keepdims=True))
            alpha = jnp.exp2(m - m_new)
            pm = jnp.exp2(s - m_new).astype(BF16)
            v = v_ref[0, 0, :, keys].astype(BF16)
            v_ext = jnp.concatenate([v, jnp.ones((ATT_PAD, sub), BF16)], axis=0)
            acc_ref[:, cols] = alpha * acc_ref[:, cols] + jnp.dot(v_ext, pm, preferred_element_type=F32)
            m_ref[:, cols] = m_new

    @pl.when(ki < qi)
    def _():
        step(False)

    @pl.when(ki == qi)
    def _():
        step(True)
        acc = acc_ref[...]
        outs = []
        for blk in range(nsub):
            a1 = acc[:, 2 * blk * sub:(2 * blk + 1) * sub]
            a2 = acc[:, (2 * blk + 1) * sub:(2 * blk + 2) * sub]
            o1 = a1[0:DA_VD] / a1[DA_VD:DA_VD + 1]
            o2 = a2[0:DA_VD] / a2[DA_VD:DA_VD + 1]
            outs.append(o1 - lam_ref[...] * o2)
        o_ref[0, 0] = jnp.concatenate(outs, axis=1).T


def _attn_prompt_call(qh, kh, vh, lam):
    b, h, _, t = qh.shape
    tq = min(TQ, t)
    nq = t // tq
    qi_tab = jnp.asarray([i for i in range(nq) for _ in range(i + 1)], jnp.int32)
    ki_tab = jnp.asarray([j for i in range(nq) for j in range(i + 1)], jnp.int32)
    grid_spec = pltpu.PrefetchScalarGridSpec(
        num_scalar_prefetch=2,
        grid=(b, h, int(qi_tab.shape[0])),
        in_specs=[pl.BlockSpec((1, 1), lambda bi, hi, p, qt, kt: (0, 0)),
                  pl.BlockSpec((1, 1, DA_VD, tq), lambda bi, hi, p, qt, kt: (bi, hi, 0, qt[p])),
                  pl.BlockSpec((1, 1, DA_VD, tq), lambda bi, hi, p, qt, kt: (bi, hi, 0, kt[p])),
                  pl.BlockSpec((1, 1, DA_VD, tq), lambda bi, hi, p, qt, kt: (bi, hi, 0, kt[p]))],
        out_specs=pl.BlockSpec((1, 1, tq, DA_VD), lambda bi, hi, p, qt, kt: (bi, hi, qt[p], 0)),
        scratch_shapes=[pltpu.VMEM((DA_VD, 2 * tq), BF16), pltpu.VMEM((1, 2 * tq), F32),
                        pltpu.VMEM((DA_VD + ATT_PAD, 2 * tq), F32)])
    return pl.pallas_call(
        _attn_prompt_kernel,
        grid_spec=grid_spec,
        out_shape=jax.ShapeDtypeStruct((b, h, t, DA_VD), F32),
        compiler_params=_params("parallel", "parallel", "arbitrary"),
        name="attn_prompt",
    )(qi_tab, ki_tab, lam, qh, kh, vh)


def _attn_sample_kernel(pt_ref, lam_ref, q_ref, kn_ref, vn_ref, *rest, t_new, n_pages):
    del pt_ref
    pp = PAGES_PER_STEP
    k_refs = rest[0:pp]
    v_refs = rest[pp:2 * pp]
    o_ref, qrows_ref, m_ref, l_ref, acc_ref = rest[2 * pp:]
    j = pl.program_id(1)
    nrow = 2 * t_new * 8

    def update(s, values, mm):
        m_old = m_ref[...]
        m_new = jnp.maximum(m_old, jnp.max(s, axis=1, keepdims=True))
        alpha = jnp.exp2(m_old - m_new)
        pm = jnp.exp2(s - m_new)
        l_ref[...] = alpha * l_ref[...] + jnp.sum(pm, axis=1, keepdims=True)
        acc = alpha * acc_ref[...]
        width = s.shape[1] // len(values)
        for idx, v in enumerate(values):
            acc = acc + mm(pm[:, idx * width:(idx + 1) * width], v)
        acc_ref[...] = acc
        m_ref[...] = m_new

    @pl.when(j == 0)
    def _():
        q = q_ref[0]
        sub = lax.broadcasted_iota(jnp.int32, (8, DA_W), 0)
        lane = lax.broadcasted_iota(jnp.int32, (8, DA_W), 1)
        for mp in range(2):
            keep = ((lane >> 6) == sub) & (((lane >> 5) & 1) == mp)
            for qi in range(t_new):
                r0 = mp * t_new * 8 + qi * 8
                qb = jnp.broadcast_to(q[qi:qi + 1, :], (8, DA_W))
                qrows_ref[r0:r0 + 8, :] = jnp.where(keep, qb, 0.0).astype(BF16)
        m_ref[...] = jnp.full(m_ref.shape, NEG_BIG, F32)
        l_ref[...] = jnp.zeros(l_ref.shape, F32)
        acc_ref[...] = jnp.zeros(acc_ref.shape, F32)
        pad = jnp.zeros((16 - t_new, DA_W), F32)
        k8 = jnp.concatenate([kn_ref[0], pad], axis=0)
        v8 = jnp.concatenate([vn_ref[0], pad], axis=0)
        s = _mm_nt(qrows_ref[...], k8)
        key = lax.broadcasted_iota(jnp.int32, s.shape, 1)
        qidx = (lax.broadcasted_iota(jnp.int32, s.shape, 0) >> 3) & (t_new - 1)
        s = jnp.where(key <= qidx, s, NEG_BIG)
        update(s, [v8], _mm)

    def pairs(refs):
        return [jnp.concatenate([refs[i][...].astype(BF16), refs[i + 1][...].astype(BF16)], axis=1)
                for i in range(0, pp, 2)]

    s_all = jnp.concatenate([_mm(qrows_ref[...], kp) for kp in pairs(k_refs)], axis=1)
    update(s_all, pairs(v_refs), _mm_nt)

    @pl.when(j == n_pages // pp - 1)
    def _():
        o = acc_ref[...] / l_ref[...]
        half = nrow // 2
        oc = o[0:half] - lam_ref[...] * o[half:]
        hd = lax.broadcasted_iota(jnp.int32, oc.shape, 0) & 7
        lane = lax.broadcasted_iota(jnp.int32, oc.shape, 1)
        oc = jnp.where((lane >> 6) == hd, oc, 0.0)
        o_ref[0] = jnp.sum(oc.reshape(t_new, 8, DA_W), axis=1)


def _attn_sample_call(q, k_new, v_new, cache_k, cache_v, page_table, lam, layer):
    nseq, t_new, _ = q.shape
    n_pages = page_table.shape[1]
    pp = PAGES_PER_STEP
    nrow = 2 * t_new * 8
    pt_flat = page_table.reshape(-1).astype(jnp.int32)

    def page_spec(idx):
        return pl.BlockSpec((None, None, DA_W, PAGE_SIZE),
                            lambda bi, j, pt: (layer, pt[bi * n_pages + j * pp + idx], 0, 0))

    tok_spec = pl.BlockSpec((1, t_new, DA_W), lambda bi, j, pt: (bi, 0, 0))
    grid_spec = pltpu.PrefetchScalarGridSpec(
        num_scalar_prefetch=1,
        grid=(nseq, n_pages // pp),
        in_specs=[pl.BlockSpec((1, 1), lambda bi, j, pt: (0, 0)), tok_spec, tok_spec, tok_spec]
        + [page_spec(i) for i in range(pp)] + [page_spec(i) for i in range(pp)],
        out_specs=tok_spec,
        scratch_shapes=[pltpu.VMEM((nrow, DA_W), BF16), pltpu.VMEM((nrow, 1), F32),
                        pltpu.VMEM((nrow, 1), F32), pltpu.VMEM((nrow, DA_W), F32)])
    return pl.pallas_call(
        functools.partial(_attn_sample_kernel, t_new=t_new, n_pages=n_pages),
        grid_spec=grid_spec,
        out_shape=jax.ShapeDtypeStruct((nseq, t_new, DA_W), F32),
        compiler_params=_params("parallel", "arbitrary"),
        name="attn_sample",
    )(pt_flat, lam, q, k_new, v_new, *([cache_k] * pp), *([cache_v] * pp))


def _mix_kernel(x_ref, gate_ref, odn_ref, dng_ref, ossm_ref, oda_ref, dag_ref, onorm_ref, subln_ref, bd_ref,
                w_ref, y_ref, *, head_major):
    odn = odn_ref[0]
    if head_major:
        oda = jnp.concatenate([oda_ref[0, hd] for hd in range(DA_HEADS)], axis=1)
    else:
        oda = oda_ref[0]

    def head_norm(o, gain):
        ms = _mm_split_lhs(o * o, bd_ref[...]) * (1.0 / DN_DV)
        return o * lax.rsqrt(ms + NORM_EPS) * gain

    a = head_norm(odn, onorm_ref[...]) * _silu(dng_ref[0])
    c = head_norm(oda, subln_ref[...]) * _silu(dag_ref[0])
    mixed = (jnp.dot(a.astype(BF16), w_ref[0, 0:DN_W, :], preferred_element_type=F32)
             + jnp.dot(ossm_ref[0].astype(BF16), w_ref[0, DN_W:DN_W + SSM_W, :], preferred_element_type=F32)
             + jnp.dot(c.astype(BF16), w_ref[0, DN_W + SSM_W:, :], preferred_element_type=F32))
    y_ref[0] = x_ref[0] + gate_ref[0] * mixed


def _mix_call(x, gate, odn, dng, ossm, oda, dag, onorm_row, subln_row, w_out_bf16, layer, head_major):
    b, t, _ = x.shape
    tm = min(TM_PROJ, t)
    per_row = gate.shape[1] != 1
    tmm = tm if per_row else 1
    mod_map = (lambda bi, i: (bi, i, 0)) if per_row else (lambda bi, i: (bi, 0, 0))
    row_map = lambda bi, i: (bi, i, 0)
    rows_spec = pl.BlockSpec((1, tm, DN_W), row_map)
    oda_spec = pl.BlockSpec((1, DA_HEADS, tm, DA_VD), lambda bi, i: (bi, 0, i, 0)) if head_major else rows_spec
    c = jnp.arange(DN_W)
    bd = ((c[:, None] // DN_DV) == (c[None, :] // DN_DV)).astype(BF16)
    return pl.pallas_call(
        functools.partial(_mix_kernel, head_major=head_major),
        grid=(b, t // tm),
        in_specs=[pl.BlockSpec((1, tm, D_MODEL), row_map),
                  pl.BlockSpec((1, tmm, D_MODEL), mod_map),
                  rows_spec,
                  pl.BlockSpec((1, tm, DN_W), row_map),
                  pl.BlockSpec((1, tm, SSM_W), row_map),
                  oda_spec,
                  pl.BlockSpec((1, tm, DA_W), row_map),
                  pl.BlockSpec((1, DN_W), lambda bi, i: (0, 0)),
                  pl.BlockSpec((1, DA_W), lambda bi, i: (0, 0)),
                  pl.BlockSpec((DN_W, DN_W), lambda bi, i: (0, 0)),
                  pl.BlockSpec((1, MIX_W, D_MODEL), lambda bi, i: (layer, 0, 0))],
        out_specs=pl.BlockSpec((1, tm, D_MODEL), row_map),
        out_shape=jax.ShapeDtypeStruct((b, t, D_MODEL), F32),
        compiler_params=_params("parallel", "parallel"),
        name="mix",
    )(x, gate, odn, dng, ossm, oda, dag, onorm_row, subln_row, bd, w_out_bf16)


def _final_norm_kernel(x_ref, g_ref, o_ref):
    x = x_ref[0]
    ms = jnp.mean(x * x, axis=-1, keepdims=True)
    o_ref[0] = x * lax.rsqrt(ms + NORM_EPS) * g_ref[...]


def _final_norm_call(x, g):
    b, t, _ = x.shape
    tm = min(1024, t)
    return pl.pallas_call(
        _final_norm_kernel,
        grid=(b, t // tm),
        in_specs=[pl.BlockSpec((1, tm, D_MODEL), lambda bi, i: (bi, i, 0)),
                  pl.BlockSpec((1, D_MODEL), lambda bi, i: (0, 0))],
        out_specs=pl.BlockSpec((1, tm, D_MODEL), lambda bi, i: (bi, i, 0)),
        out_shape=jax.ShapeDtypeStruct((b, t, D_MODEL), F32),
        compiler_params=_params("parallel", "parallel"),
        name="final_norm",
    )(x, g)


def _permute_w_in(w_in):
    splits = (DN_CONV_CH, DN_HEADS, DN_HEADS, DN_W, SSM_W, SSM_W, DA_W, DA_W, DA_W, DA_W)
    offs = [0]
    for n in splits:
        offs.append(offs[-1] + n)
    qkv, dnb, dna, dng, su, sg, q, k, v, dag = [w_in[..., offs[i]:offs[i + 1]] for i in range(10)]
    pad = jnp.zeros(w_in.shape[:-1] + (128 - 2 * DN_HEADS,), w_in.dtype)
    return jnp.concatenate([qkv, dng, su, sg, q, k, v, dag, dnb, dna, pad], axis=-1).astype(BF16)


def _rope_tables(pos):
    half = DA_HD // 2
    inv = jnp.power(ROPE_THETA, -jnp.arange(half, dtype=F32) * 2.0 / DA_HD)
    ang = pos.astype(F32)[:, None] * inv[None, :]
    cos = jnp.tile(jnp.cos(ang), (1, 128 // half))
    sin = jnp.tile(jnp.sin(ang), (1, 128 // half))
    sign = jnp.where((jnp.arange(128) & half) == 0, -1.0, 1.0).astype(F32)
    return cos, sin * sign[None, :]


def _s5_params(lam_re, lam_im, log_dt, b_re, b_im, c_re, c_im, d_skip, w_glu, b_glu):
    g, p, cg = SSM_GROUPS, SSM_P, SSM_GROUP_CH
    lam = lax.complex(lam_re.astype(F32), lam_im.astype(F32))
    dt = jnp.exp(log_dt.astype(F32))[:, None]
    lam_bar = jnp.exp(lam * dt)
    b_bar = ((lam_bar - 1.0) / lam)[..., None] * lax.complex(b_re.astype(F32), b_im.astype(F32))
    eye = jnp.eye(g, dtype=F32)
    b_t = jnp.transpose(b_bar, (0, 2, 1))
    bb_re = jnp.einsum("gcp,gh->gchp", jnp.real(b_t), eye).reshape(g * cg, g * p)
    bb_im = jnp.einsum("gcp,gh->gchp", jnp.imag(b_t), eye).reshape(g * cg, g * p)
    bblk = jnp.concatenate([bb_re, bb_im], axis=1).astype(BF16)
    c_t_re = jnp.transpose(c_re.astype(F32), (0, 2, 1))
    c_t_im = jnp.transpose(c_im.astype(F32), (0, 2, 1))
    cc_re = jnp.einsum("gpc,gh->gphc", c_t_re, eye).reshape(g * p, g * cg)
    cc_im = jnp.einsum("gpc,gh->gphc", c_t_im, eye).reshape(g * p, g * cg)
    cblk = jnp.concatenate([cc_re, -cc_im], axis=0).astype(BF16)
    a = lam_bar.reshape(1, g * p)
    steps = jnp.arange(1, S5_SUB + 1, dtype=F32)[:, None]
    pw = jnp.exp((lam * dt).reshape(1, g * p) * steps)
    ap = pw[S5_SUB - 1:S5_SUB]
    return {"bblk": bblk, "cblk": cblk,
            "ar": jnp.real(a), "ai": jnp.imag(a),
            "apr": jnp.real(ap), "api": jnp.imag(ap),
            "pw": jnp.repeat(jnp.concatenate([jnp.real(pw), jnp.imag(pw)], axis=1), 8, axis=0),
            "d": d_skip.astype(F32).reshape(1, SSM_W),
            "wglu": w_glu.astype(BF16), "bglu": b_glu.astype(F32).reshape(1, SSM_W)}


def _s5_block_order(a, outer, inner):
    b, t, w = a.shape
    a = a.reshape(b, t // (outer * inner), outer, inner, w)
    return jnp.transpose(a, (0, 1, 3, 2, 4)).reshape(b, t, w)


def _lane_row(vals, offset):
    return jnp.zeros((1, 128), F32).at[0, offset:offset + vals.shape[0]].set(vals.astype(F32))


def kernel(x_prompt, x_sample, c_prompt, c_sample, cache_k, cache_v, page_table, state_conv, state_delta, state_ssm_re, state_ssm_im, norm_g, w_ada, b_ada, w_in, conv_w, dn_a_log, dn_dt_bias, dn_onorm, ssm_lam_re, ssm_lam_im, ssm_log_dt, ssm_b_re, ssm_b_im, ssm_c_re, ssm_c_im, ssm_d, ssm_w_glu, ssm_b_glu, da_lam_q1, da_lam_k1, da_lam_q2, da_lam_k2, da_subln, w_out, final_g):
    bp, tp, _ = x_prompt.shape
    bs, ts, _ = x_sample.shape
    n_pages = page_table.shape[1]
    past = n_pages * PAGE_SIZE
    n_pool = cache_k.shape[1]
    rs = bs * ts

    n_c = bp + bs
    c_rows = -(-n_c // 8) * 8
    c_all = jnp.concatenate([c_prompt, c_sample, jnp.zeros((c_rows - n_c, D_MODEL), F32)], axis=0)
    mods = _ada_call(c_all, w_ada, b_ada)

    w_perm = _permute_w_in(w_in)
    w_out_bf = w_out.astype(BF16)
    cos_p, sin_p = _rope_tables(jnp.arange(tp, dtype=jnp.int32))
    cos_s, sin_s = _rope_tables(past + (jnp.arange(rs, dtype=jnp.int32) % ts))
    cache_k4 = jnp.transpose(cache_k, (0, 1, 3, 4, 2)).reshape(DEPTH, n_pool, DA_W, PAGE_SIZE)
    cache_v4 = jnp.transpose(cache_v, (0, 1, 3, 4, 2)).reshape(DEPTH, n_pool, DA_W, PAGE_SIZE)
    conv_w8 = jnp.concatenate([conv_w, jnp.zeros((DEPTH, 8 - CONV_K, DN_CONV_CH), F32)], axis=1)
    zero_buf = jnp.zeros((bp, 8, DN_CONV_CH), F32)
    zero_delta = jnp.zeros((bp, DN_HEADS, DN_DK, DN_DV), F32)
    zero_h = jnp.zeros((bp, 1, SSM_N), F32)

    xp = x_prompt
    xs = x_sample.reshape(1, rs, D_MODEL)
    outs = {k: [] for k in ("kp", "vp", "ks", "vs", "cp", "cs", "dp", "ds", "hrp", "hip", "hrs", "his")}
    for l in range(DEPTH):
        lam_init = 0.8 - 0.6 * math.exp(-0.3 * l)
        lam = (jnp.exp(jnp.sum(da_lam_q1[l].astype(F32) * da_lam_k1[l].astype(F32)))
               - jnp.exp(jnp.sum(da_lam_q2[l].astype(F32) * da_lam_k2[l].astype(F32))) + lam_init).reshape(1, 1)
        g_row = norm_g[l].reshape(1, D_MODEL)
        alog_row = _lane_row(dn_a_log[l], DN_HEADS)
        dtb_row = _lane_row(dn_dt_bias[l], DN_HEADS)
        onorm_row = jnp.tile(dn_onorm[l].astype(F32), DN_HEADS).reshape(1, DN_W)
        subln_row = (jnp.tile(da_subln[l].astype(F32), DA_HEADS) * (1.0 - lam_init)).reshape(1, DA_W)
        sp = _s5_params(ssm_lam_re[l], ssm_lam_im[l], ssm_log_dt[l], ssm_b_re[l], ssm_b_im[l],
                        ssm_c_re[l], ssm_c_im[l], ssm_d[l], ssm_w_glu[l], ssm_b_glu[l])

        mp = mods[l, 0:bp].reshape(bp, 1, 3 * D_MODEL)
        shift, scale, gate = mp[..., 0:D_MODEL], mp[..., D_MODEL:2 * D_MODEL], mp[..., 2 * D_MODEL:]
        qkv, dng, su, sg, dag, sm, qh, kh, vh = _inproj_call(
            xp, scale, shift, g_row, cos_p, sin_p, w_perm, l, True)
        odn, s_fin = _delta_call(qkv, zero_buf, conv_w8[l], sm, alog_row, dtb_row,
                                 min(DN_ROWS, tp), DN_CHUNK, (0, DN_CHUNK), zero_delta)
        ossm, hr, hi = _s5_call(_s5_block_order(su, 8, S5_SUB), _s5_block_order(sg, 8, S5_SUB),
                                zero_h, zero_h, sp)
        ossm = _s5_block_order(ossm, S5_SUB, 8)
        oda = _attn_prompt_call(qh, kh, vh, lam)
        xp = _mix_call(xp, gate, odn, dng, ossm, oda, dag, onorm_row, subln_row, w_out_bf, l, True)
        outs["kp"].append(jnp.transpose(kh, (0, 3, 1, 2)))
        outs["vp"].append(jnp.transpose(vh, (0, 3, 1, 2)))
        outs["cp"].append(qkv[:, tp - (CONV_K - 1):, :])
        outs["dp"].append(s_fin)
        outs["hrp"].append(hr.reshape(bp, SSM_GROUPS, SSM_P))
        outs["hip"].append(hi.reshape(bp, SSM_GROUPS, SSM_P))

        ms_ = jnp.repeat(mods[l, bp:bp + bs], ts, axis=0).reshape(1, rs, 3 * D_MODEL)
        shift, scale, gate = ms_[..., 0:D_MODEL], ms_[..., D_MODEL:2 * D_MODEL], ms_[..., 2 * D_MODEL:]
        qkv, dng, su, sg, dag, sm, qrow, kf, vf = _inproj_call(
            xs, scale, shift, g_row, cos_s, sin_s, w_perm, l, False)
        qkv_seq = qkv.reshape(bs, ts, DN_CONV_CH)
        nb = CONV_K - 1
        pad_t = DN_CHUNK_S - nb - ts
        qkv_cat = jnp.pad(jnp.concatenate([state_conv[l], qkv_seq], axis=1), ((0, 0), (0, pad_t), (0, 0)))
        sm_cat = jnp.pad(sm.reshape(bs, ts, 128), ((0, 0), (nb, pad_t), (0, 0)))
        u, w, qd, kd, at, gc = _delta_call(
            qkv_cat.reshape(1, bs * DN_CHUNK_S, DN_CONV_CH), zero_buf[0:1], conv_w8[l],
            sm_cat.reshape(1, bs * DN_CHUNK_S, 128), alog_row, dtb_row,
            min(DN_ROWS, bs * DN_CHUNK_S), DN_CHUNK_S, (nb, nb + ts))
        odn, s_fin = _delta_step_call(u, w, qd, kd, at, gc, state_delta[l], DN_CHUNK_S)
        odn = odn.reshape(DN_HEADS, bs, DN_CHUNK_S, DN_DV)[:, :, nb:nb + ts, :]
        odn = jnp.transpose(odn, (1, 2, 0, 3)).reshape(1, rs, DN_W)
        su_t = jnp.transpose(su.reshape(bs, ts, SSM_W), (1, 0, 2))
        sg_t = jnp.transpose(sg.reshape(bs, ts, SSM_W), (1, 0, 2))
        ossm, hr, hi = _s5_step_call(su_t, sg_t, state_ssm_re[l].reshape(bs, SSM_N),
                                     state_ssm_im[l].reshape(bs, SSM_N), sp, bs, ts)
        ossm = jnp.transpose(ossm, (1, 0, 2))
        oda = _attn_sample_call(qrow.reshape(bs, ts, DA_W), kf.reshape(bs, ts, DA_W), vf.reshape(bs, ts, DA_W),
                                cache_k4, cache_v4, page_table, lam, l)
        xs = _mix_call(xs, gate, odn, dng, ossm.reshape(1, rs, SSM_W), oda.reshape(1, rs, DA_W), dag,
                       onorm_row, subln_row, w_out_bf, l, False)
        xp_conv = jnp.concatenate([state_conv[l], qkv_seq], axis=1)
        outs["ks"].append(kf.reshape(bs, ts, DA_HEADS, DA_VD))
        outs["vs"].append(vf.reshape(bs, ts, DA_HEADS, DA_VD))
        outs["cs"].append(xp_conv[:, xp_conv.shape[1] - (CONV_K - 1):, :])
        outs["ds"].append(s_fin)
        outs["hrs"].append(hr.reshape(bs, SSM_GROUPS, SSM_P))
        outs["his"].append(hi.reshape(bs, SSM_GROUPS, SSM_P))

    fg = final_g.reshape(1, D_MODEL)
    y_prompt = _final_norm_call(xp, fg)
    y_sample = _final_norm_call(xs, fg).reshape(bs, ts, D_MODEL)
    st = {k: jnp.stack(v) for k, v in outs.items()}
    return (y_prompt, y_sample, st["kp"], st["vp"], st["ks"], st["vs"], st["cp"], st["cs"],
            st["dp"], st["ds"], st["hrp"], st["hip"], st["hrs"], st["his"])
```

```python
import functools
import math

import jax
import jax.numpy as jnp
from jax import lax
from jax.experimental import pallas as pl
from jax.experimental.pallas import tpu as pltpu

F32 = jnp.float32
BF16 = jnp.bfloat16

D_MODEL = 1024
DEPTH = 4
PAGE_SIZE = 128
DN_HEADS = 6
DN_DK = 64
DN_DV = 64
DN_W = DN_HEADS * DN_DV
DN_CONV_CH = 2 * DN_HEADS * DN_DK + DN_W
CONV_K = 4
DN_CHUNK = 64
SSM_GROUPS = 16
SSM_GROUP_CH = 16
SSM_W = SSM_GROUPS * SSM_GROUP_CH
SSM_P = 64
SSM_N = SSM_GROUPS * SSM_P
DA_HEADS = 6
DA_HD = 32
DA_VD = 2 * DA_HD
DA_W = DA_HEADS * DA_VD
MIX_W = DN_W + SSM_W + DA_W
ROPE_THETA = 10000.0
NORM_EPS = 1e-6
NEG_BIG = -1e30

SEG_QKV = (0, 1152)
SEG_DNG = (1152, 1536)
SEG_SU = (1536, 1792)
SEG_SG = (1792, 2048)
SEG_Q = (2048, 2432)
SEG_K = (2432, 2816)
SEG_V = (2816, 3200)
SEG_DAG = (3200, 3584)
SEG_SM = (3584, 3712)
IN_W_PAD = 3712

TM_PROJ = 512
DN_ROWS = 256
DN_GROUP = 128
DN_CHUNK_S = 16
S5_ROWS = 512
S5_SUB = 64
S5_SLABS = SSM_N // 128
TQ = 2048
TK_SUB = 512
PAGES_PER_STEP = 16
ATT_PAD = 16
VMEM_LIMIT = 56 * 1024 * 1024

Q_SCALE = (DA_HD ** -0.5) * math.log2(math.e)


def _mm(a, b):
    return jnp.dot(a.astype(BF16), b.astype(BF16), preferred_element_type=F32)


def _mm_nt(a, b):
    return lax.dot_general(a.astype(BF16), b.astype(BF16), (((1,), (1,)), ((), ())),
                           preferred_element_type=F32)


def _mm_tn(a, b):
    return lax.dot_general(a.astype(BF16), b.astype(BF16), (((0,), (0,)), ((), ())),
                           preferred_element_type=F32)


def _split(a):
    hi = a.astype(BF16)
    lo = (a - hi.astype(F32)).astype(BF16)
    return hi, lo


def _mm_split_lhs(a, b_bf16):
    hi, lo = _split(a)
    return (jnp.dot(hi, b_bf16, preferred_element_type=F32)
            + jnp.dot(lo, b_bf16, preferred_element_type=F32))


def _mm3(a, b):
    ah, al = _split(a)
    bh, bl = _split(b)
    return (jnp.dot(ah, bh, preferred_element_type=F32)
            + jnp.dot(ah, bl, preferred_element_type=F32)
            + jnp.dot(al, bh, preferred_element_type=F32))


def _silu(x):
    return x * jax.nn.sigmoid(x)


def _softplus(x):
    return jnp.maximum(x, 0.0) + jnp.log1p(jnp.exp(-jnp.abs(x)))


def _gelu_tanh(x):
    c = math.sqrt(2.0 / math.pi)
    return x * (0.5 * (1.0 + jnp.tanh(c * (x + 0.044715 * (x * x * x)))))


def _params(*sem):
    return pltpu.CompilerParams(dimension_semantics=sem, vmem_limit_bytes=VMEM_LIMIT)


def _ada_kernel(c_ref, w_ref, b_ref, o_ref):
    c = c_ref[...]
    o_ref[0] = _mm3(_silu(c), w_ref[0]) + b_ref[0]


def _ada_call(c_all, w_ada, b_ada):
    rows = c_all.shape[0]
    tn = 1024
    return pl.pallas_call(
        _ada_kernel,
        grid=(DEPTH, 3 * D_MODEL // tn),
        in_specs=[pl.BlockSpec((rows, D_MODEL), lambda l, n: (0, 0)),
                  pl.BlockSpec((1, D_MODEL, tn), lambda l, n: (l, 0, n)),
                  pl.BlockSpec((1, 1, tn), lambda l, n: (l, 0, n))],
        out_specs=pl.BlockSpec((1, rows, tn), lambda l, n: (l, 0, n)),
        out_shape=jax.ShapeDtypeStruct((DEPTH, rows, 3 * D_MODEL), F32),
        compiler_params=_params("parallel", "parallel"),
        name="adaln",
    )(c_all, w_ada, b_ada.reshape(DEPTH, 1, 3 * D_MODEL))


def _rope(x, cos, sin):
    lane = lax.broadcasted_iota(jnp.int32, cos.shape, 1)
    low = (lane & 16) == 0
    outs = []
    for c in range(x.shape[1] // 128):
        xc = x[:, c * 128:(c + 1) * 128]
        sw = jnp.where(low, pltpu.roll(xc, 112, 1), pltpu.roll(xc, 16, 1))
        outs.append(xc * cos + sw * sin)
    return jnp.concatenate(outs, axis=1)


def _inproj_kernel(x_ref, sc_ref, sh_ref, g_ref, cos_ref, sin_ref, w_ref, *rest, head_major, n_alias):
    outs = rest[n_alias:]
    x = x_ref[0]
    ms = jnp.mean(x * x, axis=-1, keepdims=True)
    h = x * lax.rsqrt(ms + NORM_EPS) * g_ref[...] * (1.0 + sc_ref[0]) + sh_ref[0]
    hb = h.astype(BF16)

    def seg(ab):
        return jnp.dot(hb, w_ref[0, :, ab[0]:ab[1]], preferred_element_type=F32)

    qkv_o, dng_o, su_o, sg_o, dag_o, sm_o, q_o, k_o, v_o = outs
    cos = cos_ref[...]
    sin = sin_ref[...]

    def emit(o_ref, val, dt):
        if head_major:
            for hd in range(DA_HEADS):
                o_ref[0, hd] = val[:, hd * DA_VD:(hd + 1) * DA_VD].T.astype(dt)
        else:
            o_ref[0] = val

    emit(q_o, _rope(seg(SEG_Q), cos, sin) * Q_SCALE, BF16)
    qkv_o[0] = seg(SEG_QKV)
    emit(k_o, _rope(seg(SEG_K), cos, sin), F32)
    dng_o[0] = seg(SEG_DNG)
    su_o[0] = seg(SEG_SU)
    emit(v_o, seg(SEG_V), F32)
    sg_o[0] = seg(SEG_SG)
    dag_o[0] = seg(SEG_DAG)
    sm_o[0] = seg(SEG_SM)


def _inproj_call(x, scale, shift, norm_g, cos, sin, w_perm, layer, head_major, kv_all=None):
    b, t, _ = x.shape
    tm = min(TM_PROJ, t)
    per_row = scale.shape[1] != 1
    tmm = tm if per_row else 1
    mod_map = (lambda bi, i: (bi, i, 0)) if per_row else (lambda bi, i: (bi, 0, 0))
    row_map = lambda bi, i: (bi, i, 0)

    def row_spec(w):
        return pl.BlockSpec((1, tm, w), row_map)

    def row_shape(w):
        return jax.ShapeDtypeStruct((b, t, w), F32)

    out_specs = [row_spec(1152), row_spec(384), row_spec(256), row_spec(256), row_spec(384), row_spec(128)]
    out_shape = [row_shape(1152), row_shape(384), row_shape(256), row_shape(256), row_shape(384), row_shape(128)]
    in_specs = [row_spec(D_MODEL),
                pl.BlockSpec((1, tmm, D_MODEL), mod_map),
                pl.BlockSpec((1, tmm, D_MODEL), mod_map),
                pl.BlockSpec((1, D_MODEL), lambda bi, i: (0, 0)),
                pl.BlockSpec((tm, 128), lambda bi, i: (i, 0)),
                pl.BlockSpec((tm, 128), lambda bi, i: (i, 0)),
                pl.BlockSpec((1, D_MODEL, IN_W_PAD), lambda bi, i: (layer, 0, 0))]
    args = [x, scale, shift, norm_g, cos, sin, w_perm]
    aliases = {}
    if head_major:
        out_specs.append(pl.BlockSpec((1, DA_HEADS, DA_VD, tm), lambda bi, i: (bi, 0, 0, i)))
        out_shape.append(jax.ShapeDtypeStruct((b, DA_HEADS, DA_VD, t), BF16))
        for _ in range(2):
            out_specs.append(pl.BlockSpec((None, 1, DA_HEADS, DA_VD, tm), lambda bi, i: (layer, bi, 0, 0, i)))
            out_shape.append(jax.ShapeDtypeStruct((DEPTH, b, DA_HEADS, DA_VD, t), F32))
        if kv_all is not None:
            for arr in kv_all:
                aliases[len(args)] = len(out_shape) - 2 + len(aliases)
                in_specs.append(pl.BlockSpec(memory_space=pl.ANY))
                args.append(arr)
    else:
        out_specs += [row_spec(384)] * 3
        out_shape += [row_shape(384)] * 3
    return pl.pallas_call(
        functools.partial(_inproj_kernel, head_major=head_major, n_alias=len(aliases)),
        grid=(b, t // tm),
        in_specs=in_specs,
        out_specs=out_specs,
        out_shape=out_shape,
        input_output_aliases=aliases,
        compiler_params=_params("parallel", "parallel"),
        name="inproj",
    )(*args)


def _delta_kernel(x_ref, prev_ref, buf_ref, cw_ref, sm_ref, alog_ref, dtb_ref, ltri_ref, last_ref,
                  fold_ref, bd_ref, *rest, rows, chunk, valid, scan):
    if scan:
        s0_ref, o_ref, sf_ref, xs_ref, s_ref = rest
    else:
        u_o, w_o, qd_o, kd_o, at_o, gc_o, xs_ref = rest
    i = pl.program_id(1)

    if scan:
        @pl.when(i == 0)
        def _():
            zero = jnp.zeros((DN_DK, DN_DV), F32)
            for p in range(DN_HEADS // 2):
                s_ref[p] = jnp.concatenate(
                    [jnp.concatenate([s0_ref[0, 2 * p], zero], axis=1),
                     jnp.concatenate([zero, s0_ref[0, 2 * p + 1]], axis=1)], axis=0)
    halo = jnp.where(i == 0, buf_ref[0], prev_ref[0])
    xs_ref[0:8, :] = halo
    xs_ref[8:8 + rows, :] = x_ref[0]
    cw = cw_ref[...]
    y = (xs_ref[pl.ds(5, rows), :] * cw[0:1] + xs_ref[pl.ds(6, rows), :] * cw[1:2]
         + xs_ref[pl.ds(7, rows), :] * cw[2:3] + xs_ref[pl.ds(8, rows), :] * cw[3:4])
    y = _silu(y)
    qk = y[:, 0:2 * DN_W]
    ss = _mm_split_lhs(qk * qk, bd_ref[...])
    qkn = qk * lax.rsqrt(ss + NORM_EPS)
    v_all = y[:, 2 * DN_W:]

    sm = sm_ref[0]
    pos = lax.broadcasted_iota(jnp.int32, sm.shape, 0) & (chunk - 1)
    real = (pos >= valid[0]) & (pos < valid[1])
    beta = jnp.where(real, jax.nn.sigmoid(sm), 0.0)
    g = jnp.where(real, -jnp.exp(alog_ref[...]) * _softplus(sm + dtb_ref[...]), 0.0)
    gc = _mm_split_lhs_rhs(ltri_ref[...], g)
    gc_last = _mm_split_lhs_rhs(last_ref[...], gc)
    if not scan:
        gc_o[0] = gc
    gc_t = gc.T

    grp = min(rows, DN_GROUP)
    n_grp = rows // grp
    per_grp = grp // chunk
    shift = chunk.bit_length() - 1
    ri = lax.broadcasted_iota(jnp.int32, (grp, grp), 0)
    ci = lax.broadcasted_iota(jnp.int32, (grp, grp), 1)
    same_chunk = (ri >> shift) == (ci >> shift)
    causal = same_chunk & (ri >= ci)
    strict_b = jnp.where(same_chunk & (ri > ci), 1.0, 0.0).astype(BF16)
    eye_f = jnp.where(ri == ci, 1.0, 0.0)
    level_b = []
    s = 1
    while s < chunk:
        sh = s.bit_length()
        m = ((ri >> sh) == (ci >> sh)) & ((ri & s) != 0) & ((ci & s) == 0)
        level_b.append(jnp.where(m, 1.0, 0.0).astype(BF16))
        s *= 2
    scale = DN_DK ** -0.5
    n_pair = DN_HEADS // 2
    first_half = lax.broadcasted_iota(jnp.int32, (grp, 128), 1) < DN_DK

    probs = []
    for g in range(n_grp):
        gr = slice(g * grp, (g + 1) * grp)
        for p in range(n_pair):
            q_pair = qkn[gr, p * 128:(p + 1) * 128] * scale
            k_pair = qkn[gr, DN_W + p * 128:DN_W + (p + 1) * 128]
            v_pair = v_all[gr, p * 128:(p + 1) * 128]
            k_pair_b = k_pair.astype(BF16)
            for hh in range(2):
                hd = 2 * p + hh
                own = first_half if hh == 0 else jnp.logical_not(first_half)
                qh = jnp.where(own, q_pair, 0.0)
                kh = jnp.where(own, k_pair, 0.0)
                vh = jnp.where(own, v_pair, 0.0)
                beta_c = beta[gr, hd:hd + 1]
                gcol = gc[gr, DN_HEADS + hd:DN_HEADS + hd + 1]
                grow = gc_t[DN_HEADS + hd:DN_HEADS + hd + 1, gr]
                decay = jnp.exp(jnp.where(causal, gcol - grow, NEG_BIG))
                kb = kh * beta_c
                m_b = (_mm_nt(kb, k_pair_b) * decay).astype(BF16) * strict_b
                probs.append((g, p, hh, qh, kh, vh, k_pair_b, beta_c, gcol, decay, kb, m_b))
    x_invs = [eye_f - (pr[11] * level_b[0]).astype(F32) for pr in probs]
    for lb in level_b[1:]:
        for n, pr in enumerate(probs):
            x_b = x_invs[n].astype(BF16)
            t_b = jnp.dot(x_b, pr[11] * lb, preferred_element_type=F32).astype(BF16)
            x_invs[n] = x_invs[n] - jnp.dot(t_b, x_b, preferred_element_type=F32)

    pairs = {}
    for n, (g, p, hh, qh, kh, vh, k_pair_b, beta_c, gcol, decay, kb, _) in enumerate(probs):
        gr = slice(g * grp, (g + 1) * grp)
        hd = 2 * p + hh
        e_g = jnp.exp(gcol)
        sol = _mm(x_invs[n], jnp.concatenate([vh * beta_c, kb * e_g], axis=1))
        attn_full = _mm_nt(qh, k_pair_b) * decay
        u = sol[:, 0:128]
        w = sol[:, 128:256]
        qd = qh * e_g
        gl_col = gc_last[gr, DN_HEADS + hd:DN_HEADS + hd + 1]
        kd = kh * jnp.exp(gl_col - gcol)
        if scan:
            ent = pairs.setdefault((g, p), {"sum": None, "attn": [], "e_last": []})
            ent["sum"] = (u, w, qd, kd) if hh == 0 else tuple(a + b for a, b in zip(ent["sum"], (u, w, qd, kd)))
            ent["attn"].append(attn_full.astype(BF16))
            ent["e_last"].append(jnp.exp(gl_col))
        else:
            hs = slice(hh * DN_DK, (hh + 1) * DN_DK)
            u_o[0, hd, gr, :] = u[:, hs]
            w_o[0, hd, gr, :] = w[:, hs]
            qd_o[0, hd, gr, :] = qd[:, hs]
            kd_o[0, hd, gr, :] = kd[:, hs]
            at_o[0, hd, gr, :] = _mm(attn_full, fold_ref[0:grp, :])

    if scan:
        r2 = lax.broadcasted_iota(jnp.int32, (128, 128), 0) < DN_DK
        c2 = lax.broadcasted_iota(jnp.int32, (128, 128), 1) < DN_DK
        on_diag = r2 == c2
        lane_first = lax.broadcasted_iota(jnp.int32, (1, 128), 1) < DN_DK
        states = [s_ref[p] for p in range(n_pair)]
        for g in range(n_grp):
            v_news = [[] for _ in range(n_pair)]
            o_states = [[] for _ in range(n_pair)]
            for c in range(per_grp):
                rc = slice(c * chunk, (c + 1) * chunk)
                for p in range(n_pair):
                    ent = pairs[(g, p)]
                    u, w, qd, kd = ent["sum"]
                    r = _mm(jnp.concatenate([w[rc], qd[rc]], axis=0), states[p])
                    v_new = u[rc] - r[0:chunk]
                    o_states[p].append(r[chunk:])
                    e_last = jnp.where(lane_first, ent["e_last"][0][c * chunk:c * chunk + 1],
                                       ent["e_last"][1][c * chunk:c * chunk + 1])
                    states[p] = jnp.where(on_diag, states[p] * e_last + _mm_tn(kd[rc], v_new), 0.0)
                    v_news[p].append(v_new)
            for p in range(n_pair):
                vn_b = jnp.concatenate(v_news[p], axis=0).astype(BF16)
                oa = jnp.dot(pairs[(g, p)]["attn"][0], vn_b, preferred_element_type=F32)
                ob = jnp.dot(pairs[(g, p)]["attn"][1], vn_b, preferred_element_type=F32)
                o_ref[0, g * grp:(g + 1) * grp, p * 128:(p + 1) * 128] = (
                    jnp.concatenate(o_states[p], axis=0) + jnp.where(first_half, oa, ob))
        for p in range(n_pair):
            s_ref[p] = states[p]

        @pl.when(i == pl.num_programs(1) - 1)
        def _():
            for p in range(DN_HEADS // 2):
                sf_ref[0, 2 * p] = s_ref[p, 0:DN_DK, 0:DN_DV]
                sf_ref[0, 2 * p + 1] = s_ref[p, DN_DK:, DN_DV:]


def _mm_split_lhs_rhs(a_bf16, b):
    hi, lo = _split(b)
    return (jnp.dot(a_bf16, hi, preferred_element_type=F32)
            + jnp.dot(a_bf16, lo, preferred_element_type=F32))


def _delta_call(qkv, buf8, conv_w8, sm, alog_row, dtb_row, rows, chunk, valid, s0=None):
    b, t, _ = qkv.shape
    nblk = t // rows
    scan = s0 is not None
    r = jnp.arange(rows)
    same = (r[:, None] // chunk) == (r[None, :] // chunk)
    ltri = ((r[:, None] >= r[None, :]) & same).astype(BF16)
    last = (r[None, :] == (r[:, None] | (chunk - 1))).astype(BF16)
    fold = ((r[:, None] & (chunk - 1)) == jnp.arange(DN_CHUNK)[None, :]).astype(BF16)
    c = jnp.arange(2 * DN_W)
    bd = ((c[:, None] // DN_DK) == (c[None, :] // DN_DK)).astype(BF16)
    hm_spec = pl.BlockSpec((1, DN_HEADS, rows, DN_DV), lambda bi, i: (bi, 0, i, 0))
    hm_shape = jax.ShapeDtypeStruct((b, DN_HEADS, t, DN_DV), F32)
    st_spec = pl.BlockSpec((1, DN_HEADS, DN_DK, DN_DV), lambda bi, i: (bi, 0, 0, 0))
    in_specs = [pl.BlockSpec((1, rows, DN_CONV_CH), lambda bi, i: (bi, i, 0)),
                pl.BlockSpec((1, 8, DN_CONV_CH), lambda bi, i: (bi, jnp.maximum(i * (rows // 8) - 1, 0), 0)),
                pl.BlockSpec((1, 8, DN_CONV_CH), lambda bi, i: (bi, 0, 0)),
                pl.BlockSpec((8, DN_CONV_CH), lambda bi, i: (0, 0)),
                pl.BlockSpec((1, rows, 128), lambda bi, i: (bi, i, 0)),
                pl.BlockSpec((1, 128), lambda bi, i: (0, 0)),
                pl.BlockSpec((1, 128), lambda bi, i: (0, 0)),
                pl.BlockSpec((rows, rows), lambda bi, i: (0, 0)),
                pl.BlockSpec((rows, rows), lambda bi, i: (0, 0)),
                pl.BlockSpec((rows, DN_CHUNK), lambda bi, i: (0, 0)),
                pl.BlockSpec((2 * DN_W, 2 * DN_W), lambda bi, i: (0, 0))]
    args = [qkv, qkv, buf8, conv_w8, sm, alog_row, dtb_row, ltri, last, fold, bd]
    scratch = [pltpu.VMEM((rows + 8, DN_CONV_CH), F32)]
    if scan:
        in_specs.append(st_spec)
        args.append(s0)
        out_specs = [pl.BlockSpec((1, rows, DN_W), lambda bi, i: (bi, i, 0)), st_spec]
        out_shape = [jax.ShapeDtypeStruct((b, t, DN_W), F32),
                     jax.ShapeDtypeStruct((b, DN_HEADS, DN_DK, DN_DV), F32)]
        scratch.append(pltpu.VMEM((DN_HEADS // 2, 2 * DN_DK, 2 * DN_DV), F32))
    else:
        out_specs = [hm_spec] * 5 + [pl.BlockSpec((1, rows, 128), lambda bi, i: (bi, i, 0))]
        out_shape = [hm_shape] * 5 + [jax.ShapeDtypeStruct((b, t, 128), F32)]
    return pl.pallas_call(
        functools.partial(_delta_kernel, rows=rows, chunk=chunk, valid=valid, scan=scan),
        grid=(b, nblk),
        in_specs=in_specs,
        out_specs=out_specs,
        out_shape=out_shape,
        scratch_shapes=scratch,
        compiler_params=_params("parallel", "arbitrary" if scan else "parallel"),
        name="delta_scan" if scan else "delta_local",
    )(*args)


def _delta_step_kernel(u_ref, w_ref, qd_ref, kd_ref, at_ref, gc_ref, s0_ref, o_ref, sf_ref, *, bb, chunk):
    todo = [(bi, hd, slice(bi * chunk, (bi + 1) * chunk)) for bi in range(bb) for hd in range(DN_HEADS)]
    rs = [_mm(jnp.concatenate([w_ref[0, hd, rows, :], qd_ref[0, hd, rows, :]], axis=0), s0_ref[bi, hd])
          for bi, hd, rows in todo]
    v_news = [u_ref[0, hd, rows, :] - r[0:chunk] for (bi, hd, rows), r in zip(todo, rs)]
    for (bi, hd, rows), r, v_new in zip(todo, rs, v_news):
        o_ref[0, hd, rows, :] = r[chunk:] + _mm(at_ref[0, hd, rows, 0:chunk], v_new)
    for (bi, hd, rows), v_new in zip(todo, v_news):
        last = (bi + 1) * chunk - 1
        g_last = jnp.exp(gc_ref[0, last:last + 1, DN_HEADS + hd:DN_HEADS + hd + 1])
        sf_ref[bi, hd] = s0_ref[bi, hd] * g_last + _mm_tn(kd_ref[0, hd, rows, :], v_new)


def _delta_step_call(u, w, qd, kd, at, gc, s0, chunk):
    nseq = s0.shape[0]
    bb = 4
    hm_spec = pl.BlockSpec((1, DN_HEADS, bb * chunk, DN_DV), lambda i: (0, 0, i, 0))
    st_spec = pl.BlockSpec((bb, DN_HEADS, DN_DK, DN_DV), lambda i: (i, 0, 0, 0))
    return pl.pallas_call(
        functools.partial(_delta_step_kernel, bb=bb, chunk=chunk),
        grid=(nseq // bb,),
        in_specs=[hm_spec] * 5 + [pl.BlockSpec((1, bb * chunk, 128), lambda i: (0, i, 0)), st_spec],
        out_specs=[hm_spec, st_spec],
        out_shape=[jax.ShapeDtypeStruct(u.shape, F32),
                   jax.ShapeDtypeStruct((nseq, DN_HEADS, DN_DK, DN_DV), F32)],
        compiler_params=_params("parallel"),
        name="delta_step",
    )(u, w, qd, kd, at, gc, s0)


def _s5_epilogue(y, u, sg, d_ref, wglu_ref, bglu_ref):
    z = _gelu_tanh(y + d_ref[...] * u)
    gate = jax.nn.sigmoid(_mm(z, wglu_ref[...]) + bglu_ref[...])
    return z * gate * _silu(sg)


def _s5_kernel(u_ref, sg_ref, h0r_ref, h0i_ref, bblk_ref, ar_ref, ai_ref, apr_ref, api_ref, pw_ref,
               cblk_ref, d_ref, wglu_ref, bglu_ref, o_ref, hr_o, hi_o, hs_ref, cr_ref, ci_ref):
    i = pl.program_id(1)
    ns = S5_SLABS

    @pl.when(i == 0)
    def _():
        cr_ref[...] = h0r_ref[0]
        ci_ref[...] = h0i_ref[0]

    def slab(c):
        return slice(c * 128, (c + 1) * 128)

    u = u_ref[0]
    bu = _mm(u, bblk_ref[...])
    for c in range(2 * ns):
        hs_ref[c] = bu[:, slab(c)]
    ar = [jnp.broadcast_to(ar_ref[:, slab(c)], (8, 128)) for c in range(ns)]
    ai = [jnp.broadcast_to(ai_ref[:, slab(c)], (8, 128)) for c in range(ns)]

    def scan_body(j, carry):
        rows = pl.ds(pl.multiple_of(j * 8, 8), 8)
        new = []
        for c in range(ns):
            hr, hi = carry[2 * c], carry[2 * c + 1]
            nr = ar[c] * hr - ai[c] * hi + hs_ref[c, rows, :]
            ni = ar[c] * hi + ai[c] * hr + hs_ref[ns + c, rows, :]
            hs_ref[c, rows, :] = nr
            hs_ref[ns + c, rows, :] = ni
            new += [nr, ni]
        return tuple(new)

    zero = jnp.zeros((8, 128), F32)
    ends = lax.fori_loop(0, S5_SUB, scan_body, (zero,) * (2 * ns))

    h_in = []
    for c in range(ns):
        apr = apr_ref[:, slab(c)]
        api = api_ref[:, slab(c)]
        hr = cr_ref[:, slab(c)]
        hi = ci_ref[:, slab(c)]
        er, ei = ends[2 * c], ends[2 * c + 1]
        rows_r, rows_i = [], []
        for s in range(8):
            rows_r.append(hr)
            rows_i.append(hi)
            nr = apr * hr - api * hi + er[s:s + 1]
            ni = apr * hi + api * hr + ei[s:s + 1]
            hr, hi = nr, ni
        cr_ref[:, slab(c)] = hr
        ci_ref[:, slab(c)] = hi
        h_in += [jnp.concatenate(rows_r, axis=0), jnp.concatenate(rows_i, axis=0)]

    def fix_body(j, carry):
        rows = pl.ds(pl.multiple_of(j * 8, 8), 8)
        for c in range(ns):
            pr = pw_ref[rows, slab(c)]
            pi = pw_ref[rows, slab(ns + c)]
            hr, hi = h_in[2 * c], h_in[2 * c + 1]
            hs_ref[c, rows, :] = hs_ref[c, rows, :] + pr * hr - pi * hi
            hs_ref[ns + c, rows, :] = hs_ref[ns + c, rows, :] + pr * hi + pi * hr
        return carry

    lax.fori_loop(0, S5_SUB, fix_body, 0)

    y = jnp.zeros((S5_ROWS, SSM_W), F32)
    for c in range(0, 2 * ns, 2):
        y = y + _mm(jnp.concatenate([hs_ref[c], hs_ref[c + 1]], axis=1), cblk_ref[c * 128:(c + 2) * 128, :])

    o_ref[0] = _s5_epilogue(y, u, sg_ref[0], d_ref, wglu_ref, bglu_ref)

    @pl.when(i == pl.num_programs(1) - 1)
    def _():
        hr_o[0] = cr_ref[...]
        hi_o[0] = ci_ref[...]


def _s5_call(u, sg, h0r, h0i, sp):
    b, t, _ = u.shape
    n = SSM_N
    row_spec = pl.BlockSpec((1, S5_ROWS, SSM_W), lambda bi, i: (bi, i, 0))
    st_spec = pl.BlockSpec((1, 1, n), lambda bi, i: (bi, 0, 0))
    full = lambda shape: pl.BlockSpec(shape, lambda bi, i: (0,) * len(shape))
    return pl.pallas_call(
        _s5_kernel,
        grid=(b, t // S5_ROWS),
        in_specs=[row_spec, row_spec, st_spec, st_spec,
                  full((SSM_W, 2 * n)), full((1, n)), full((1, n)), full((1, n)), full((1, n)),
                  full((S5_ROWS, 2 * n)), full((2 * n, SSM_W)), full((1, SSM_W)),
                  full((SSM_W, SSM_W)), full((1, SSM_W))],
        out_specs=[row_spec, st_spec, st_spec],
        out_shape=[jax.ShapeDtypeStruct((b, t, SSM_W), F32),
                   jax.ShapeDtypeStruct((b, 1, n), F32), jax.ShapeDtypeStruct((b, 1, n), F32)],
        scratch_shapes=[pltpu.VMEM((2 * S5_SLABS, S5_ROWS, 128), F32), pltpu.VMEM((1, n), F32),
                        pltpu.VMEM((1, n), F32)],
        compiler_params=_params("parallel", "arbitrary"),
        name="s5",
    )(u, sg, h0r, h0i, sp["bblk"], sp["ar"], sp["ai"], sp["apr"], sp["api"], sp["pw"],
      sp["cblk"], sp["d"], sp["wglu"], sp["bglu"])


def _s5_step_kernel(u_ref, sg_ref, h0r_ref, h0i_ref, bblk_ref, ar_ref, ai_ref, cblk_ref, d_ref, wglu_ref,
                    bglu_ref, o_ref, hr_o, hi_o, *, nseq, t):
    hr = h0r_ref[...]
    hi = h0i_ref[...]
    ar = ar_ref[...]
    ai = ai_ref[...]
    n = SSM_N
    for step in range(t):
        u = u_ref[step]
        sg = sg_ref[step]
        bu = _mm(u, bblk_ref[...])
        nr = ar * hr - ai * hi + bu[:, 0:n]
        ni = ar * hi + ai * hr + bu[:, n:]
        hr, hi = nr, ni
        y = _mm(jnp.concatenate([hr, hi], axis=1), cblk_ref[...])
        o_ref[step] = _s5_epilogue(y, u, sg, d_ref, wglu_ref, bglu_ref)
    hr_o[...] = hr
    hi_o[...] = hi


def _s5_step_call(u, sg, h0r, h0i, sp, nseq, t):
    n = SSM_N
    return pl.pallas_call(
        functools.partial(_s5_step_kernel, nseq=nseq, t=t),
        out_shape=[jax.ShapeDtypeStruct((t, nseq, SSM_W), F32),
                   jax.ShapeDtypeStruct((nseq, n), F32), jax.ShapeDtypeStruct((nseq, n), F32)],
        compiler_params=pltpu.CompilerParams(vmem_limit_bytes=VMEM_LIMIT),
        name="s5_step",
    )(u, sg, h0r, h0i, sp["bblk"], sp["ar"], sp["ai"], sp["cblk"], sp["d"], sp["wglu"], sp["bglu"])


def _attn_prompt_kernel(qi_tab, ki_tab, lam_ref, q_ref, k_ref, v_ref, o_ref, qcat_ref, m_ref, acc_ref):
    p = pl.program_id(2)
    qi = qi_tab[p]
    ki = ki_tab[p]
    tq = q_ref.shape[3]

    sub = TK_SUB
    nsub = tq // sub

    @pl.when(ki == 0)
    def _():
        q = q_ref[0, 0]
        feat = lax.broadcasted_iota(jnp.int32, (DA_VD, sub), 0)
        zero = jnp.zeros((DA_VD, sub), q.dtype)
        for blk in range(nsub):
            qb = q[:, blk * sub:(blk + 1) * sub]
            qcat_ref[:, 2 * blk * sub:(2 * blk + 1) * sub] = jnp.where(feat < DA_HD, qb, zero)
            qcat_ref[:, (2 * blk + 1) * sub:(2 * blk + 2) * sub] = jnp.where(feat >= DA_HD, qb, zero)
        m_ref[...] = jnp.full(m_ref.shape, NEG_BIG, F32)
        acc_ref[...] = jnp.zeros(acc_ref.shape, F32)

    def step(diagonal):
        def col_range(c):
            return slice(2 * c * sub, 2 * tq) if diagonal else slice(0, 2 * tq)

        def scores(c):
            return _mm_tn(k_ref[0, 0, :, c * sub:(c + 1) * sub], qcat_ref[:, col_range(c)])

        s_next = scores(0)
        for c in range(nsub):
            keys = slice(c * sub, (c + 1) * sub)
            cols = col_range(c)
            s = s_next
            if c + 1 < nsub:
                s_next = scores(c + 1)
            if diagonal:
                kr = lax.broadcasted_iota(jnp.int32, (sub, 2 * sub), 0)
                qc = lax.broadcasted_iota(jnp.int32, (sub, 2 * sub), 1) & (sub - 1)
                own = jnp.where(kr <= qc, s[:, 0:2 * sub], NEG_BIG)
                s = own if c == nsub - 1 else jnp.concatenate([own, s[:, 2 * sub:]], axis=1)
            m = m_ref[:, cols]
            m_new = jnp.maximum(m, jnp.max(s, axis=0, keepdims=True))
            alpha = jnp.exp2(m - m_new)
            pm = jnp.exp2(s - m_new).astype(BF16)
            v = v_ref[0, 0, :, keys].astype(BF16)
            v_ext = jnp.concatenate([v, jnp.ones((ATT_PAD, sub), BF16)], axis=0)
            acc_ref[:, cols] = alpha * acc_ref[:, cols] + jnp.dot(v_ext, pm, preferred_element_type=F32)
            m_ref[:, cols] = m_new

    @pl.when(ki < qi)
    def _():
        step(False)

    @pl.when(ki == qi)
    def _():
        step(True)
        acc = acc_ref[...]
        outs = []
        for blk in range(nsub):
            a1 = acc[:, 2 * blk * sub:(2 * blk + 1) * sub]
            a2 = acc[:, (2 * blk + 1) * sub:(2 * blk + 2) * sub]
            o1 = a1[0:DA_VD] / a1[DA_VD:DA_VD + 1]
            o2 = a2[0:DA_VD] / a2[DA_VD:DA_VD + 1]
            outs.append(o1 - lam_ref[...] * o2)
        o_ref[0, 0] = jnp.concatenate(outs, axis=1).T


def _attn_prompt_call(qh, k_all, v_all, lam, layer):
    b, h, _, t = qh.shape
    tq = min(TQ, t)
    nq = t // tq
    qi_tab = jnp.asarray([i for i in range(nq) for _ in range(i + 1)], jnp.int32)
    ki_tab = jnp.asarray([j for i in range(nq) for j in range(i + 1)], jnp.int32)
    grid_spec = pltpu.PrefetchScalarGridSpec(
        num_scalar_prefetch=2,
        grid=(b, h, int(qi_tab.shape[0])),
        in_specs=[pl.BlockSpec((1, 1), lambda bi, hi, p, qt, kt: (0, 0)),
                  pl.BlockSpec((1, 1, DA_VD, tq), lambda bi, hi, p, qt, kt: (bi, hi, 0, qt[p])),
                  pl.BlockSpec((None, 1, 1, DA_VD, tq), lambda bi, hi, p, qt, kt: (layer, bi, hi, 0, kt[p])),
                  pl.BlockSpec((None, 1, 1, DA_VD, tq), lambda bi, hi, p, qt, kt: (layer, bi, hi, 0, kt[p]))],
        out_specs=pl.BlockSpec((1, 1, tq, DA_VD), lambda bi, hi, p, qt, kt: (bi, hi, qt[p], 0)),
        scratch_shapes=[pltpu.VMEM((DA_VD, 2 * tq), BF16), pltpu.VMEM((1, 2 * tq), F32),
                        pltpu.VMEM((DA_VD + ATT_PAD, 2 * tq), F32)])
    return pl.pallas_call(
        _attn_prompt_kernel,
        grid_spec=grid_spec,
        out_shape=jax.ShapeDtypeStruct((b, h, t, DA_VD), F32),
        compiler_params=_params("parallel", "parallel", "arbitrary"),
        name="attn_prompt",
    )(qi_tab, ki_tab, lam, qh, k_all, v_all)


def _attn_sample_kernel(pt_ref, lam_ref, q_ref, kn_ref, vn_ref, *rest, t_new, n_pages):
    del pt_ref
    pp = PAGES_PER_STEP
    k_refs = rest[0:pp]
    v_refs = rest[pp:2 * pp]
    o_ref, qrows_ref, m_ref, l_ref, acc_ref = rest[2 * pp:]
    j = pl.program_id(1)
    nrow = 2 * t_new * 8

    def update(s, values, mm):
        m_old = m_ref[...]
        m_new = jnp.maximum(m_old, jnp.max(s, axis=1, keepdims=True))
        alpha = jnp.exp2(m_old - m_new)
        pm = jnp.exp2(s - m_new)
        l_ref[...] = alpha * l_ref[...] + jnp.sum(pm, axis=1, keepdims=True)
        acc = alpha * acc_ref[...]
        width = s.shape[1] // len(values)
        for idx, v in enumerate(values):
            acc = acc + mm(pm[:, idx * width:(idx + 1) * width], v)
        acc_ref[...] = acc
        m_ref[...] = m_new

    @pl.when(j == 0)
    def _():
        q = q_ref[0]
        sub = lax.broadcasted_iota(jnp.int32, (8, DA_W), 0)
        lane = lax.broadcasted_iota(jnp.int32, (8, DA_W), 1)
        for mp in range(2):
            keep = ((lane >> 6) == sub) & (((lane >> 5) & 1) == mp)
            for qi in range(t_new):
                r0 = mp * t_new * 8 + qi * 8
                qb = jnp.broadcast_to(q[qi:qi + 1, :], (8, DA_W))
                qrows_ref[r0:r0 + 8, :] = jnp.where(keep, qb, 0.0).astype(BF16)
        m_ref[...] = jnp.full(m_ref.shape, NEG_BIG, F32)
        l_ref[...] = jnp.zeros(l_ref.shape, F32)
        acc_ref[...] = jnp.zeros(acc_ref.shape, F32)
        pad = jnp.zeros((16 - t_new, DA_W), F32)
        k8 = jnp.concatenate([kn_ref[0], pad], axis=0)
        v8 = jnp.concatenate([vn_ref[0], pad], axis=0)
        s = _mm_nt(qrows_ref[...], k8)
        key = lax.broadcasted_iota(jnp.int32, s.shape, 1)
        qidx = (lax.broadcasted_iota(jnp.int32, s.shape, 0) >> 3) & (t_new - 1)
        s = jnp.where(key <= qidx, s, NEG_BIG)
        update(s, [v8], _mm)

    def pairs(refs):
        return [jnp.concatenate([refs[i][...].astype(BF16), refs[i + 1][...].astype(BF16)], axis=1)
                for i in range(0, pp, 2)]

    s_all = jnp.concatenate([_mm(qrows_ref[...], kp) for kp in pairs(k_refs)], axis=1)
    update(s_all, pairs(v_refs), _mm_nt)

    @pl.when(j == n_pages // pp - 1)
    def _():
        o = acc_ref[...] / l_ref[...]
        half = nrow // 2
        oc = o[0:half] - lam_ref[...] * o[half:]
        hd = lax.broadcasted_iota(jnp.int32, oc.shape, 0) & 7
        lane = lax.broadcasted_iota(jnp.int32, oc.shape, 1)
        oc = jnp.where((lane >> 6) == hd, oc, 0.0)
        o_ref[0] = jnp.sum(oc.reshape(t_new, 8, DA_W), axis=1)


def _attn_sample_call(q, k_new, v_new, cache_k, cache_v, page_table, lam, layer):
    nseq, t_new, _ = q.shape
    n_pages = page_table.shape[1]
    pp = PAGES_PER_STEP
    nrow = 2 * t_new * 8
    pt_flat = page_table.reshape(-1).astype(jnp.int32)

    def page_spec(idx):
        return pl.BlockSpec((None, None, DA_W, PAGE_SIZE),
                            lambda bi, j, pt: (layer, pt[bi * n_pages + j * pp + idx], 0, 0))

    tok_spec = pl.BlockSpec((1, t_new, DA_W), lambda bi, j, pt: (bi, 0, 0))
    grid_spec = pltpu.PrefetchScalarGridSpec(
        num_scalar_prefetch=1,
        grid=(nseq, n_pages // pp),
        in_specs=[pl.BlockSpec((1, 1), lambda bi, j, pt: (0, 0)), tok_spec, tok_spec, tok_spec]
        + [page_spec(i) for i in range(pp)] + [page_spec(i) for i in range(pp)],
        out_specs=tok_spec,
        scratch_shapes=[pltpu.VMEM((nrow, DA_W), BF16), pltpu.VMEM((nrow, 1), F32),
                        pltpu.VMEM((nrow, 1), F32), pltpu.VMEM((nrow, DA_W), F32)])
    return pl.pallas_call(
        functools.partial(_attn_sample_kernel, t_new=t_new, n_pages=n_pages),
        grid_spec=grid_spec,
        out_shape=jax.ShapeDtypeStruct((nseq, t_new, DA_W), F32),
        compiler_params=_params("parallel", "arbitrary"),
        name="attn_sample",
    )(pt_flat, lam, q, k_new, v_new, *([cache_k] * pp), *([cache_v] * pp))


def _mix_kernel(x_ref, gate_ref, odn_ref, dng_ref, ossm_ref, oda_ref, dag_ref, onorm_ref, subln_ref, bd_ref,
                w_ref, y_ref, *, head_major):
    odn = odn_ref[0]
    if head_major:
        oda = jnp.concatenate([oda_ref[0, hd] for hd in range(DA_HEADS)], axis=1)
    else:
        oda = oda_ref[0]

    def head_norm(o, gain):
        ms = _mm_split_lhs(o * o, bd_ref[...]) * (1.0 / DN_DV)
        return o * lax.rsqrt(ms + NORM_EPS) * gain

    a = head_norm(odn, onorm_ref[...]) * _silu(dng_ref[0])
    c = head_norm(oda, subln_ref[...]) * _silu(dag_ref[0])
    mixed = (jnp.dot(a.astype(BF16), w_ref[0, 0:DN_W, :], preferred_element_type=F32)
             + jnp.dot(ossm_ref[0].astype(BF16), w_ref[0, DN_W:DN_W + SSM_W, :], preferred_element_type=F32)
             + jnp.dot(c.astype(BF16), w_ref[0, DN_W + SSM_W:, :], preferred_element_type=F32))
    y_ref[0] = x_ref[0] + gate_ref[0] * mixed


def _mix_call(x, gate, odn, dng, ossm, oda, dag, onorm_row, subln_row, w_out_bf16, layer, head_major):
    b, t, _ = x.shape
    tm = min(TM_PROJ, t)
    per_row = gate.shape[1] != 1
    tmm = tm if per_row else 1
    mod_map = (lambda bi, i: (bi, i, 0)) if per_row else (lambda bi, i: (bi, 0, 0))
    row_map = lambda bi, i: (bi, i, 0)
    rows_spec = pl.BlockSpec((1, tm, DN_W), row_map)
    oda_spec = pl.BlockSpec((1, DA_HEADS, tm, DA_VD), lambda bi, i: (bi, 0, i, 0)) if head_major else rows_spec
    c = jnp.arange(DN_W)
    bd = ((c[:, None] // DN_DV) == (c[None, :] // DN_DV)).astype(BF16)
    return pl.pallas_call(
        functools.partial(_mix_kernel, head_major=head_major),
        grid=(b, t // tm),
        in_specs=[pl.BlockSpec((1, tm, D_MODEL), row_map),
                  pl.BlockSpec((1, tmm, D_MODEL), mod_map),
                  rows_spec,
                  pl.BlockSpec((1, tm, DN_W), row_map),
                  pl.BlockSpec((1, tm, SSM_W), row_map),
                  oda_spec,
                  pl.BlockSpec((1, tm, DA_W), row_map),
                  pl.BlockSpec((1, DN_W), lambda bi, i: (0, 0)),
                  pl.BlockSpec((1, DA_W), lambda bi, i: (0, 0)),
                  pl.BlockSpec((DN_W, DN_W), lambda bi, i: (0, 0)),
                  pl.BlockSpec((1, MIX_W, D_MODEL), lambda bi, i: (layer, 0, 0))],
        out_specs=pl.BlockSpec((1, tm, D_MODEL), row_map),
        out_shape=jax.ShapeDtypeStruct((b, t, D_MODEL), F32),
        compiler_params=_params("parallel", "parallel"),
        name="mix",
    )(x, gate, odn, dng, ossm, oda, dag, onorm_row, subln_row, bd, w_out_bf16)


def _final_norm_kernel(x_ref, g_ref, o_ref):
    x = x_ref[0]
    ms = jnp.mean(x * x, axis=-1, keepdims=True)
    o_ref[0] = x * lax.rsqrt(ms + NORM_EPS) * g_ref[...]


def _final_norm_call(x, g):
    b, t, _ = x.shape
    tm = min(1024, t)
    return pl.pallas_call(
        _final_norm_kernel,
        grid=(b, t // tm),
        in_specs=[pl.BlockSpec((1, tm, D_MODEL), lambda bi, i: (bi, i, 0)),
                  pl.BlockSpec((1, D_MODEL), lambda bi, i: (0, 0))],
        out_specs=pl.BlockSpec((1, tm, D_MODEL), lambda bi, i: (bi, i, 0)),
        out_shape=jax.ShapeDtypeStruct((b, t, D_MODEL), F32),
        compiler_params=_params("parallel", "parallel"),
        name="final_norm",
    )(x, g)


def _permute_w_in(w_in):
    splits = (DN_CONV_CH, DN_HEADS, DN_HEADS, DN_W, SSM_W, SSM_W, DA_W, DA_W, DA_W, DA_W)
    offs = [0]
    for n in splits:
        offs.append(offs[-1] + n)
    qkv, dnb, dna, dng, su, sg, q, k, v, dag = [w_in[..., offs[i]:offs[i + 1]] for i in range(10)]
    pad = jnp.zeros(w_in.shape[:-1] + (128 - 2 * DN_HEADS,), w_in.dtype)
    return jnp.concatenate([qkv, dng, su, sg, q, k, v, dag, dnb, dna, pad], axis=-1).astype(BF16)


def _rope_tables(pos):
    half = DA_HD // 2
    inv = jnp.power(ROPE_THETA, -jnp.arange(half, dtype=F32) * 2.0 / DA_HD)
    ang = pos.astype(F32)[:, None] * inv[None, :]
    cos = jnp.tile(jnp.cos(ang), (1, 128 // half))
    sin = jnp.tile(jnp.sin(ang), (1, 128 // half))
    sign = jnp.where((jnp.arange(128) & half) == 0, -1.0, 1.0).astype(F32)
    return cos, sin * sign[None, :]


def _s5_params(lam_re, lam_im, log_dt, b_re, b_im, c_re, c_im, d_skip, w_glu, b_glu):
    g, p, cg = SSM_GROUPS, SSM_P, SSM_GROUP_CH
    lam = lax.complex(lam_re.astype(F32), lam_im.astype(F32))
    dt = jnp.exp(log_dt.astype(F32))[:, None]
    lam_bar = jnp.exp(lam * dt)
    b_bar = ((lam_bar - 1.0) / lam)[..., None] * lax.complex(b_re.astype(F32), b_im.astype(F32))
    eye = jnp.eye(g, dtype=F32)
    b_t = jnp.transpose(b_bar, (0, 2, 1))
    bb_re = jnp.einsum("gcp,gh->gchp", jnp.real(b_t), eye).reshape(g * cg, g * p)
    bb_im = jnp.einsum("gcp,gh->gchp", jnp.imag(b_t), eye).reshape(g * cg, g * p)
    bblk = jnp.concatenate([bb_re, bb_im], axis=1).astype(BF16)
    c_t_re = jnp.transpose(c_re.astype(F32), (0, 2, 1))
    c_t_im = jnp.transpose(c_im.astype(F32), (0, 2, 1))
    cc_re = jnp.einsum("gpc,gh->gphc", c_t_re, eye).reshape(g * p, g * cg)
    cc_im = jnp.einsum("gpc,gh->gphc", c_t_im, eye).reshape(g * p, g * cg)
    cblk = jnp.concatenate([cc_re, -cc_im], axis=0).astype(BF16)
    a = lam_bar.reshape(1, g * p)
    steps = jnp.arange(1, S5_SUB + 1, dtype=F32)[:, None]
    pw = jnp.exp((lam * dt).reshape(1, g * p) * steps)
    ap = pw[S5_SUB - 1:S5_SUB]
    return {"bblk": bblk, "cblk": cblk,
            "ar": jnp.real(a), "ai": jnp.imag(a),
            "apr": jnp.real(ap), "api": jnp.imag(ap),
            "pw": jnp.repeat(jnp.concatenate([jnp.real(pw), jnp.imag(pw)], axis=1), 8, axis=0),
            "d": d_skip.astype(F32).reshape(1, SSM_W),
            "wglu": w_glu.astype(BF16), "bglu": b_glu.astype(F32).reshape(1, SSM_W)}


def _s5_block_order(a, outer, inner):
    b, t, w = a.shape
    a = a.reshape(b, t // (outer * inner), outer, inner, w)
    return jnp.transpose(a, (0, 1, 3, 2, 4)).reshape(b, t, w)


def _lane_row(vals, offset):
    return jnp.zeros((1, 128), F32).at[0, offset:offset + vals.shape[0]].set(vals.astype(F32))


def kernel(x_prompt, x_sample, c_prompt, c_sample, cache_k, cache_v, page_table, state_conv, state_delta, state_ssm_re, state_ssm_im, norm_g, w_ada, b_ada, w_in, conv_w, dn_a_log, dn_dt_bias, dn_onorm, ssm_lam_re, ssm_lam_im, ssm_log_dt, ssm_b_re, ssm_b_im, ssm_c_re, ssm_c_im, ssm_d, ssm_w_glu, ssm_b_glu, da_lam_q1, da_lam_k1, da_lam_q2, da_lam_k2, da_subln, w_out, final_g):
    bp, tp, _ = x_prompt.shape
    bs, ts, _ = x_sample.shape
    n_pages = page_table.shape[1]
    past = n_pages * PAGE_SIZE
    n_pool = cache_k.shape[1]
    rs = bs * ts

    n_c = bp + bs
    c_rows = -(-n_c // 8) * 8
    c_all = jnp.concatenate([c_prompt, c_sample, jnp.zeros((c_rows - n_c, D_MODEL), F32)], axis=0)
    mods = _ada_call(c_all, w_ada, b_ada)

    w_perm = _permute_w_in(w_in)
    w_out_bf = w_out.astype(BF16)
    cos_p, sin_p = _rope_tables(jnp.arange(tp, dtype=jnp.int32))
    cos_s, sin_s = _rope_tables(past + (jnp.arange(rs, dtype=jnp.int32) % ts))
    cache_k4 = jnp.transpose(cache_k, (0, 1, 3, 4, 2)).reshape(DEPTH, n_pool, DA_W, PAGE_SIZE)
    cache_v4 = jnp.transpose(cache_v, (0, 1, 3, 4, 2)).reshape(DEPTH, n_pool, DA_W, PAGE_SIZE)
    conv_w8 = jnp.concatenate([conv_w, jnp.zeros((DEPTH, 8 - CONV_K, DN_CONV_CH), F32)], axis=1)
    zero_buf = jnp.zeros((bp, 8, DN_CONV_CH), F32)
    zero_delta = jnp.zeros((bp, DN_HEADS, DN_DK, DN_DV), F32)
    zero_h = jnp.zeros((bp, 1, SSM_N), F32)

    xp = x_prompt
    xs = x_sample.reshape(1, rs, D_MODEL)
    outs = {k: [] for k in ("ks", "vs", "cp", "cs", "dp", "ds", "hrp", "hip", "hrs", "his")}
    kv_all = None
    for l in range(DEPTH):
        lam_init = 0.8 - 0.6 * math.exp(-0.3 * l)
        lam = (jnp.exp(jnp.sum(da_lam_q1[l].astype(F32) * da_lam_k1[l].astype(F32)))
               - jnp.exp(jnp.sum(da_lam_q2[l].astype(F32) * da_lam_k2[l].astype(F32))) + lam_init).reshape(1, 1)
        g_row = norm_g[l].reshape(1, D_MODEL)
        alog_row = _lane_row(dn_a_log[l], DN_HEADS)
        dtb_row = _lane_row(dn_dt_bias[l], DN_HEADS)
        onorm_row = jnp.tile(dn_onorm[l].astype(F32), DN_HEADS).reshape(1, DN_W)
        subln_row = (jnp.tile(da_subln[l].astype(F32), DA_HEADS) * (1.0 - lam_init)).reshape(1, DA_W)
        sp = _s5_params(ssm_lam_re[l], ssm_lam_im[l], ssm_log_dt[l], ssm_b_re[l], ssm_b_im[l],
                        ssm_c_re[l], ssm_c_im[l], ssm_d[l], ssm_w_glu[l], ssm_b_glu[l])

        mp = mods[l, 0:bp].reshape(bp, 1, 3 * D_MODEL)
        shift, scale, gate = mp[..., 0:D_MODEL], mp[..., D_MODEL:2 * D_MODEL], mp[..., 2 * D_MODEL:]
        qkv, dng, su, sg, dag, sm, qh, k_all, v_all = _inproj_call(
            xp, scale, shift, g_row, cos_p, sin_p, w_perm, l, True, kv_all)
        kv_all = (k_all, v_all)
        odn, s_fin = _delta_call(qkv, zero_buf, conv_w8[l], sm, alog_row, dtb_row,
                                 min(DN_ROWS, tp), DN_CHUNK, (0, DN_CHUNK), zero_delta)
        ossm, hr, hi = _s5_call(_s5_block_order(su, 8, S5_SUB), _s5_block_order(sg, 8, S5_SUB),
                                zero_h, zero_h, sp)
        ossm = _s5_block_order(ossm, S5_SUB, 8)
        oda = _attn_prompt_call(qh, k_all, v_all, lam, l)
        xp = _mix_call(xp, gate, odn, dng, ossm, oda, dag, onorm_row, subln_row, w_out_bf, l, True)
        outs["cp"].append(qkv[:, tp - (CONV_K - 1):, :])
        outs["dp"].append(s_fin)
        outs["hrp"].append(hr.reshape(bp, SSM_GROUPS, SSM_P))
        outs["hip"].append(hi.reshape(bp, SSM_GROUPS, SSM_P))

        ms_ = jnp.repeat(mods[l, bp:bp + bs], ts, axis=0).reshape(1, rs, 3 * D_MODEL)
        shift, scale, gate = ms_[..., 0:D_MODEL], ms_[..., D_MODEL:2 * D_MODEL], ms_[..., 2 * D_MODEL:]
        qkv, dng, su, sg, dag, sm, qrow, kf, vf = _inproj_call(
            xs, scale, shift, g_row, cos_s, sin_s, w_perm, l, False)
        qkv_seq = qkv.reshape(bs, ts, DN_CONV_CH)
        nb = CONV_K - 1
        pad_t = DN_CHUNK_S - nb - ts
        qkv_cat = jnp.pad(jnp.concatenate([state_conv[l], qkv_seq], axis=1), ((0, 0), (0, pad_t), (0, 0)))
        sm_cat = jnp.pad(sm.reshape(bs, ts, 128), ((0, 0), (nb, pad_t), (0, 0)))
        u, w, qd, kd, at, gc = _delta_call(
            qkv_cat.reshape(1, bs * DN_CHUNK_S, DN_CONV_CH), zero_buf[0:1], conv_w8[l],
            sm_cat.reshape(1, bs * DN_CHUNK_S, 128), alog_row, dtb_row,
            min(DN_ROWS, bs * DN_CHUNK_S), DN_CHUNK_S, (nb, nb + ts))
        odn, s_fin = _delta_step_call(u, w, qd, kd, at, gc, state_delta[l], DN_CHUNK_S)
        odn = odn.reshape(DN_HEADS, bs, DN_CHUNK_S, DN_DV)[:, :, nb:nb + ts, :]
        odn = jnp.transpose(odn, (1, 2, 0, 3)).reshape(1, rs, DN_W)
        su_t = jnp.transpose(su.reshape(bs, ts, SSM_W), (1, 0, 2))
        sg_t = jnp.transpose(sg.reshape(bs, ts, SSM_W), (1, 0, 2))
        ossm, hr, hi = _s5_step_call(su_t, sg_t, state_ssm_re[l].reshape(bs, SSM_N),
                                     state_ssm_im[l].reshape(bs, SSM_N), sp, bs, ts)
        ossm = jnp.transpose(ossm, (1, 0, 2))
        oda = _attn_sample_call(qrow.reshape(bs, ts, DA_W), kf.reshape(bs, ts, DA_W), vf.reshape(bs, ts, DA_W),
                                cache_k4, cache_v4, page_table, lam, l)
        xs = _mix_call(xs, gate, odn, dng, ossm.reshape(1, rs, SSM_W), oda.reshape(1, rs, DA_W), dag,
                       onorm_row, subln_row, w_out_bf, l, False)
        xp_conv = jnp.concatenate([state_conv[l], qkv_seq], axis=1)
        outs["ks"].append(kf.reshape(bs, ts, DA_HEADS, DA_VD))
        outs["vs"].append(vf.reshape(bs, ts, DA_HEADS, DA_VD))
        outs["cs"].append(xp_conv[:, xp_conv.shape[1] - (CONV_K - 1):, :])
        outs["ds"].append(s_fin)
        outs["hrs"].append(hr.reshape(bs, SSM_GROUPS, SSM_P))
        outs["his"].append(hi.reshape(bs, SSM_GROUPS, SSM_P))

    fg = final_g.reshape(1, D_MODEL)
    y_prompt = _final_norm_call(xp, fg)
    y_sample = _final_norm_call(xs, fg).reshape(bs, ts, D_MODEL)
    st = {k: jnp.stack(v) for k, v in outs.items()}
    k_prompt = jnp.transpose(kv_all[0], (0, 1, 4, 2, 3))
    v_prompt = jnp.transpose(kv_all[1], (0, 1, 4, 2, 3))
    return (y_prompt, y_sample, k_prompt, v_prompt, st["ks"], st["vs"], st["cp"], st["cs"],
            st["dp"], st["ds"], st["hrp"], st["hip"], st["hrs"], st["his"])
```

```python
import functools
import math

import jax
import jax.numpy as jnp
from jax import lax
from jax.experimental import pallas as pl
from jax.experimental.pallas import tpu as pltpu

F32 = jnp.float32
BF16 = jnp.bfloat16

D_MODEL = 1024
DEPTH = 4
PAGE_SIZE = 128
DN_HEADS = 6
DN_DK = 64
DN_DV = 64
DN_W = DN_HEADS * DN_DV
DN_CONV_CH = 2 * DN_HEADS * DN_DK + DN_W
CONV_K = 4
DN_CHUNK = 64
SSM_GROUPS = 16
SSM_GROUP_CH = 16
SSM_W = SSM_GROUPS * SSM_GROUP_CH
SSM_P = 64
SSM_N = SSM_GROUPS * SSM_P
DA_HEADS = 6
DA_HD = 32
DA_VD = 2 * DA_HD
DA_W = DA_HEADS * DA_VD
MIX_W = DN_W + SSM_W + DA_W
ROPE_THETA = 10000.0
NORM_EPS = 1e-6
NEG_BIG = -1e30

SEG_QKV = (0, 1152)
SEG_DNG = (1152, 1536)
SEG_SU = (1536, 1792)
SEG_SG = (1792, 2048)
SEG_Q = (2048, 2432)
SEG_K = (2432, 2816)
SEG_V = (2816, 3200)
SEG_DAG = (3200, 3584)
SEG_SM = (3584, 3712)
IN_W_PAD = 3712

TM_PROJ = 512
DN_ROWS = 256
DN_GROUP = 128
DN_CHUNK_S = 16
S5_ROWS = 512
S5_SUB = 64
S5_SLABS = SSM_N // 128
TQ = 2048
TK_SUB = 512
PAGES_PER_STEP = 32
ATT_PAD = 16
VMEM_LIMIT = 56 * 1024 * 1024

Q_SCALE = (DA_HD ** -0.5) * math.log2(math.e)


def _mm(a, b):
    return jnp.dot(a.astype(BF16), b.astype(BF16), preferred_element_type=F32)


def _mm_nt(a, b):
    return lax.dot_general(a.astype(BF16), b.astype(BF16), (((1,), (1,)), ((), ())),
                           preferred_element_type=F32)


def _mm_tn(a, b):
    return lax.dot_general(a.astype(BF16), b.astype(BF16), (((0,), (0,)), ((), ())),
                           preferred_element_type=F32)


def _split(a):
    hi = a.astype(BF16)
    lo = (a - hi.astype(F32)).astype(BF16)
    return hi, lo


def _mm_split_lhs(a, b_bf16):
    hi, lo = _split(a)
    return (jnp.dot(hi, b_bf16, preferred_element_type=F32)
            + jnp.dot(lo, b_bf16, preferred_element_type=F32))


def _mm3(a, b):
    ah, al = _split(a)
    bh, bl = _split(b)
    return (jnp.dot(ah, bh, preferred_element_type=F32)
            + jnp.dot(ah, bl, preferred_element_type=F32)
            + jnp.dot(al, bh, preferred_element_type=F32))


def _silu(x):
    return x * jax.nn.sigmoid(x)


def _softplus(x):
    return jnp.maximum(x, 0.0) + jnp.log1p(jnp.exp(-jnp.abs(x)))


def _gelu_tanh(x):
    c = math.sqrt(2.0 / math.pi)
    return x * (0.5 * (1.0 + jnp.tanh(c * (x + 0.044715 * (x * x * x)))))


def _params(*sem):
    return pltpu.CompilerParams(dimension_semantics=sem, vmem_limit_bytes=VMEM_LIMIT)


def _ada_kernel(c_ref, w_ref, b_ref, o_ref):
    c = c_ref[...]
    o_ref[0] = _mm3(_silu(c), w_ref[0]) + b_ref[0]


def _ada_call(c_all, w_ada, b_ada):
    rows = c_all.shape[0]
    tn = 1024
    return pl.pallas_call(
        _ada_kernel,
        grid=(DEPTH, 3 * D_MODEL // tn),
        in_specs=[pl.BlockSpec((rows, D_MODEL), lambda l, n: (0, 0)),
                  pl.BlockSpec((1, D_MODEL, tn), lambda l, n: (l, 0, n)),
                  pl.BlockSpec((1, 1, tn), lambda l, n: (l, 0, n))],
        out_specs=pl.BlockSpec((1, rows, tn), lambda l, n: (l, 0, n)),
        out_shape=jax.ShapeDtypeStruct((DEPTH, rows, 3 * D_MODEL), F32),
        compiler_params=_params("parallel", "parallel"),
        name="adaln",
    )(c_all, w_ada, b_ada.reshape(DEPTH, 1, 3 * D_MODEL))


def _rope(x, cos, sin):
    lane = lax.broadcasted_iota(jnp.int32, cos.shape, 1)
    low = (lane & 16) == 0
    outs = []
    for c in range(x.shape[1] // 128):
        xc = x[:, c * 128:(c + 1) * 128]
        sw = jnp.where(low, pltpu.roll(xc, 112, 1), pltpu.roll(xc, 16, 1))
        outs.append(xc * cos + sw * sin)
    return jnp.concatenate(outs, axis=1)


def _inproj_kernel(x_ref, sc_ref, sh_ref, g_ref, cos_ref, sin_ref, w_ref, *rest, head_major, n_alias):
    outs = rest[n_alias:]
    x = x_ref[0]
    ms = jnp.mean(x * x, axis=-1, keepdims=True)
    h = x * lax.rsqrt(ms + NORM_EPS) * g_ref[...] * (1.0 + sc_ref[0]) + sh_ref[0]
    hb = h.astype(BF16)

    def seg(ab):
        return jnp.dot(hb, w_ref[0, :, ab[0]:ab[1]], preferred_element_type=F32)

    qkv_o, dng_o, su_o, sg_o, dag_o, sm_o, q_o, k_o, v_o = outs
    cos = cos_ref[...]
    sin = sin_ref[...]

    def emit(o_ref, val, dt):
        if head_major:
            for hd in range(DA_HEADS):
                o_ref[0, hd] = val[:, hd * DA_VD:(hd + 1) * DA_VD].T.astype(dt)
        else:
            o_ref[0] = val

    emit(q_o, _rope(seg(SEG_Q), cos, sin) * Q_SCALE, BF16)
    qkv_o[0] = seg(SEG_QKV)
    emit(k_o, _rope(seg(SEG_K), cos, sin), F32)
    dng_o[0] = seg(SEG_DNG)
    su_o[0] = seg(SEG_SU)
    emit(v_o, seg(SEG_V), F32)
    sg_o[0] = seg(SEG_SG)
    dag_o[0] = seg(SEG_DAG)
    sm_o[0] = seg(SEG_SM)


def _inproj_call(x, scale, shift, norm_g, cos, sin, w_perm, layer, head_major, kv_all=None):
    b, t, _ = x.shape
    tm = min(TM_PROJ, t)
    per_row = scale.shape[1] != 1
    tmm = tm if per_row else 1
    mod_map = (lambda bi, i: (bi, i, 0)) if per_row else (lambda bi, i: (bi, 0, 0))
    row_map = lambda bi, i: (bi, i, 0)

    def row_spec(w):
        return pl.BlockSpec((1, tm, w), row_map)

    def row_shape(w):
        return jax.ShapeDtypeStruct((b, t, w), F32)

    out_specs = [row_spec(1152), row_spec(384), row_spec(256), row_spec(256), row_spec(384), row_spec(128)]
    out_shape = [row_shape(1152), row_shape(384), row_shape(256), row_shape(256), row_shape(384), row_shape(128)]
    in_specs = [row_spec(D_MODEL),
                pl.BlockSpec((1, tmm, D_MODEL), mod_map),
                pl.BlockSpec((1, tmm, D_MODEL), mod_map),
                pl.BlockSpec((1, D_MODEL), lambda bi, i: (0, 0)),
                pl.BlockSpec((tm, 128), lambda bi, i: (i, 0)),
                pl.BlockSpec((tm, 128), lambda bi, i: (i, 0)),
                pl.BlockSpec((1, D_MODEL, IN_W_PAD), lambda bi, i: (layer, 0, 0))]
    args = [x, scale, shift, norm_g, cos, sin, w_perm]
    aliases = {}
    if head_major:
        out_specs.append(pl.BlockSpec((1, DA_HEADS, DA_VD, tm), lambda bi, i: (bi, 0, 0, i)))
        out_shape.append(jax.ShapeDtypeStruct((b, DA_HEADS, DA_VD, t), BF16))
        for _ in range(2):
            out_specs.append(pl.BlockSpec((None, 1, DA_HEADS, DA_VD, tm), lambda bi, i: (layer, bi, 0, 0, i)))
            out_shape.append(jax.ShapeDtypeStruct((DEPTH, b, DA_HEADS, DA_VD, t), F32))
        if kv_all is not None:
            for arr in kv_all:
                aliases[len(args)] = len(out_shape) - 2 + len(aliases)
                in_specs.append(pl.BlockSpec(memory_space=pl.ANY))
                args.append(arr)
    else:
        out_specs += [row_spec(384)] * 3
        out_shape += [row_shape(384)] * 3
    return pl.pallas_call(
        functools.partial(_inproj_kernel, head_major=head_major, n_alias=len(aliases)),
        grid=(b, t // tm),
        in_specs=in_specs,
        out_specs=out_specs,
        out_shape=out_shape,
        input_output_aliases=aliases,
        compiler_params=_params("parallel", "parallel"),
        name="inproj",
    )(*args)


def _delta_kernel(x_ref, prev_ref, buf_ref, cw_ref, sm_ref, alog_ref, dtb_ref, ltri_ref, last_ref,
                  fold_ref, bd_ref, *rest, rows, chunk, valid, scan):
    if scan:
        s0_ref, o_ref, sf_ref, xs_ref, s_ref = rest
    else:
        u_o, w_o, qd_o, kd_o, at_o, gc_o, xs_ref = rest
    i = pl.program_id(1)

    if scan:
        @pl.when(i == 0)
        def _():
            zero = jnp.zeros((DN_DK, DN_DV), F32)
            for p in range(DN_HEADS // 2):
                s_ref[p] = jnp.concatenate(
                    [jnp.concatenate([s0_ref[0, 2 * p], zero], axis=1),
                     jnp.concatenate([zero, s0_ref[0, 2 * p + 1]], axis=1)], axis=0)
    halo = jnp.where(i == 0, buf_ref[0], prev_ref[0])
    xs_ref[0:8, :] = halo
    xs_ref[8:8 + rows, :] = x_ref[0]
    cw = cw_ref[...]
    y = (xs_ref[pl.ds(5, rows), :] * cw[0:1] + xs_ref[pl.ds(6, rows), :] * cw[1:2]
         + xs_ref[pl.ds(7, rows), :] * cw[2:3] + xs_ref[pl.ds(8, rows), :] * cw[3:4])
    y = _silu(y)
    qk = y[:, 0:2 * DN_W]
    ss = _mm_split_lhs(qk * qk, bd_ref[...])
    qkn = qk * lax.rsqrt(ss + NORM_EPS)
    v_all = y[:, 2 * DN_W:]

    sm = sm_ref[0]
    pos = lax.broadcasted_iota(jnp.int32, sm.shape, 0) & (chunk - 1)
    real = (pos >= valid[0]) & (pos < valid[1])
    beta = jnp.where(real, jax.nn.sigmoid(sm), 0.0)
    g = jnp.where(real, -jnp.exp(alog_ref[...]) * _softplus(sm + dtb_ref[...]), 0.0)
    gc = _mm_split_lhs_rhs(ltri_ref[...], g)
    gc_last = _mm_split_lhs_rhs(last_ref[...], gc)
    if not scan:
        gc_o[0] = gc
    gc_t = gc.T

    grp = min(rows, DN_GROUP)
    n_grp = rows // grp
    per_grp = grp // chunk
    shift = chunk.bit_length() - 1
    ri = lax.broadcasted_iota(jnp.int32, (grp, grp), 0)
    ci = lax.broadcasted_iota(jnp.int32, (grp, grp), 1)
    same_chunk = (ri >> shift) == (ci >> shift)
    causal = same_chunk & (ri >= ci)
    strict_b = jnp.where(same_chunk & (ri > ci), 1.0, 0.0).astype(BF16)
    eye_f = jnp.where(ri == ci, 1.0, 0.0)
    level_b = []
    s = 1
    while s < chunk:
        sh = s.bit_length()
        m = ((ri >> sh) == (ci >> sh)) & ((ri & s) != 0) & ((ci & s) == 0)
        level_b.append(jnp.where(m, 1.0, 0.0).astype(BF16))
        s *= 2
    scale = DN_DK ** -0.5
    n_pair = DN_HEADS // 2
    first_half = lax.broadcasted_iota(jnp.int32, (grp, 128), 1) < DN_DK

    probs = []
    for g in range(n_grp):
        gr = slice(g * grp, (g + 1) * grp)
        for p in range(n_pair):
            q_pair = qkn[gr, p * 128:(p + 1) * 128] * scale
            k_pair = qkn[gr, DN_W + p * 128:DN_W + (p + 1) * 128]
            v_pair = v_all[gr, p * 128:(p + 1) * 128]
            k_pair_b = k_pair.astype(BF16)
            for hh in range(2):
                hd = 2 * p + hh
                own = first_half if hh == 0 else jnp.logical_not(first_half)
                qh = jnp.where(own, q_pair, 0.0)
                kh = jnp.where(own, k_pair, 0.0)
                vh = jnp.where(own, v_pair, 0.0)
                beta_c = beta[gr, hd:hd + 1]
                gcol = gc[gr, DN_HEADS + hd:DN_HEADS + hd + 1]
                grow = gc_t[DN_HEADS + hd:DN_HEADS + hd + 1, gr]
                decay = jnp.exp(jnp.where(causal, gcol - grow, NEG_BIG))
                kb = kh * beta_c
                m_b = (_mm_nt(kb, k_pair_b) * decay).astype(BF16) * strict_b
                probs.append((g, p, hh, qh, kh, vh, k_pair_b, beta_c, gcol, decay, kb, m_b))
    x_invs = [eye_f - (pr[11] * level_b[0]).astype(F32) for pr in probs]
    for lb in level_b[1:]:
        for n, pr in enumerate(probs):
            x_b = x_invs[n].astype(BF16)
            t_b = jnp.dot(x_b, pr[11] * lb, preferred_element_type=F32).astype(BF16)
            x_invs[n] = x_invs[n] - jnp.dot(t_b, x_b, preferred_element_type=F32)

    pairs = {}
    for n, (g, p, hh, qh, kh, vh, k_pair_b, beta_c, gcol, decay, kb, _) in enumerate(probs):
        gr = slice(g * grp, (g + 1) * grp)
        hd = 2 * p + hh
        e_g = jnp.exp(gcol)
        sol = _mm(x_invs[n], jnp.concatenate([vh * beta_c, kb * e_g], axis=1))
        attn_full = _mm_nt(qh, k_pair_b) * decay
        u = sol[:, 0:128]
        w = sol[:, 128:256]
        qd = qh * e_g
        gl_col = gc_last[gr, DN_HEADS + hd:DN_HEADS + hd + 1]
        kd = kh * jnp.exp(gl_col - gcol)
        if scan:
            ent = pairs.setdefault((g, p), {"sum": None, "attn": [], "e_last": []})
            ent["sum"] = (u, w, qd, kd) if hh == 0 else tuple(a + b for a, b in zip(ent["sum"], (u, w, qd, kd)))
            ent["attn"].append(attn_full.astype(BF16))
            ent["e_last"].append(jnp.exp(gl_col))
        else:
            hs = slice(hh * DN_DK, (hh + 1) * DN_DK)
            u_o[0, hd, gr, :] = u[:, hs]
            w_o[0, hd, gr, :] = w[:, hs]
            qd_o[0, hd, gr, :] = qd[:, hs]
            kd_o[0, hd, gr, :] = kd[:, hs]
            at_o[0, hd, gr, :] = _mm(attn_full, fold_ref[0:grp, :])

    if scan:
        r2 = lax.broadcasted_iota(jnp.int32, (128, 128), 0) < DN_DK
        c2 = lax.broadcasted_iota(jnp.int32, (128, 128), 1) < DN_DK
        on_diag = r2 == c2
        lane_first = lax.broadcasted_iota(jnp.int32, (1, 128), 1) < DN_DK
        states = [s_ref[p] for p in range(n_pair)]
        for g in range(n_grp):
            v_news = [[] for _ in range(n_pair)]
            o_states = [[] for _ in range(n_pair)]
            for c in range(per_grp):
                rc = slice(c * chunk, (c + 1) * chunk)
                for p in range(n_pair):
                    ent = pairs[(g, p)]
                    u, w, qd, kd = ent["sum"]
                    r = _mm(jnp.concatenate([w[rc], qd[rc]], axis=0), states[p])
                    v_new = u[rc] - r[0:chunk]
                    o_states[p].append(r[chunk:])
                    e_last = jnp.where(lane_first, ent["e_last"][0][c * chunk:c * chunk + 1],
                                       ent["e_last"][1][c * chunk:c * chunk + 1])
                    states[p] = jnp.where(on_diag, states[p] * e_last + _mm_tn(kd[rc], v_new), 0.0)
                    v_news[p].append(v_new)
            for p in range(n_pair):
                vn_b = jnp.concatenate(v_news[p], axis=0).astype(BF16)
                oa = jnp.dot(pairs[(g, p)]["attn"][0], vn_b, preferred_element_type=F32)
                ob = jnp.dot(pairs[(g, p)]["attn"][1], vn_b, preferred_element_type=F32)
                o_ref[0, g * grp:(g + 1) * grp, p * 128:(p + 1) * 128] = (
                    jnp.concatenate(o_states[p], axis=0) + jnp.where(first_half, oa, ob))
        for p in range(n_pair):
            s_ref[p] = states[p]

        @pl.when(i == pl.num_programs(1) - 1)
        def _():
            for p in range(DN_HEADS // 2):
                sf_ref[0, 2 * p] = s_ref[p, 0:DN_DK, 0:DN_DV]
                sf_ref[0, 2 * p + 1] = s_ref[p, DN_DK:, DN_DV:]


def _mm_split_lhs_rhs(a_bf16, b):
    hi, lo = _split(b)
    return (jnp.dot(a_bf16, hi, preferred_element_type=F32)
            + jnp.dot(a_bf16, lo, preferred_element_type=F32))


def _delta_call(qkv, buf8, conv_w8, sm, alog_row, dtb_row, rows, chunk, valid, s0=None):
    b, t, _ = qkv.shape
    nblk = t // rows
    scan = s0 is not None
    r = jnp.arange(rows)
    same = (r[:, None] // chunk) == (r[None, :] // chunk)
    ltri = ((r[:, None] >= r[None, :]) & same).astype(BF16)
    last = (r[None, :] == (r[:, None] | (chunk - 1))).astype(BF16)
    fold = ((r[:, None] & (chunk - 1)) == jnp.arange(DN_CHUNK)[None, :]).astype(BF16)
    c = jnp.arange(2 * DN_W)
    bd = ((c[:, None] // DN_DK) == (c[None, :] // DN_DK)).astype(BF16)
    hm_spec = pl.BlockSpec((1, DN_HEADS, rows, DN_DV), lambda bi, i: (bi, 0, i, 0))
    hm_shape = jax.ShapeDtypeStruct((b, DN_HEADS, t, DN_DV), F32)
    st_spec = pl.BlockSpec((1, DN_HEADS, DN_DK, DN_DV), lambda bi, i: (bi, 0, 0, 0))
    in_specs = [pl.BlockSpec((1, rows, DN_CONV_CH), lambda bi, i: (bi, i, 0)),
                pl.BlockSpec((1, 8, DN_CONV_CH), lambda bi, i: (bi, jnp.maximum(i * (rows // 8) - 1, 0), 0)),
                pl.BlockSpec((1, 8, DN_CONV_CH), lambda bi, i: (bi, 0, 0)),
                pl.BlockSpec((8, DN_CONV_CH), lambda bi, i: (0, 0)),
                pl.BlockSpec((1, rows, 128), lambda bi, i: (bi, i, 0)),
                pl.BlockSpec((1, 128), lambda bi, i: (0, 0)),
                pl.BlockSpec((1, 128), lambda bi, i: (0, 0)),
                pl.BlockSpec((rows, rows), lambda bi, i: (0, 0)),
                pl.BlockSpec((rows, rows), lambda bi, i: (0, 0)),
                pl.BlockSpec((rows, DN_CHUNK), lambda bi, i: (0, 0)),
                pl.BlockSpec((2 * DN_W, 2 * DN_W), lambda bi, i: (0, 0))]
    args = [qkv, qkv, buf8, conv_w8, sm, alog_row, dtb_row, ltri, last, fold, bd]
    scratch = [pltpu.VMEM((rows + 8, DN_CONV_CH), F32)]
    if scan:
        in_specs.append(st_spec)
        args.append(s0)
        out_specs = [pl.BlockSpec((1, rows, DN_W), lambda bi, i: (bi, i, 0)), st_spec]
        out_shape = [jax.ShapeDtypeStruct((b, t, DN_W), F32),
                     jax.ShapeDtypeStruct((b, DN_HEADS, DN_DK, DN_DV), F32)]
        scratch.append(pltpu.VMEM((DN_HEADS // 2, 2 * DN_DK, 2 * DN_DV), F32))
    else:
        out_specs = [hm_spec] * 5 + [pl.BlockSpec((1, rows, 128), lambda bi, i: (bi, i, 0))]
        out_shape = [hm_shape] * 5 + [jax.ShapeDtypeStruct((b, t, 128), F32)]
    return pl.pallas_call(
        functools.partial(_delta_kernel, rows=rows, chunk=chunk, valid=valid, scan=scan),
        grid=(b, nblk),
        in_specs=in_specs,
        out_specs=out_specs,
        out_shape=out_shape,
        scratch_shapes=scratch,
        compiler_params=_params("parallel", "arbitrary" if scan else "parallel"),
        name="delta_scan" if scan else "delta_local",
    )(*args)


def _delta_step_kernel(u_ref, w_ref, qd_ref, kd_ref, at_ref, gc_ref, s0_ref, o_ref, sf_ref, *, bb, chunk):
    todo = [(bi, hd, slice(bi * chunk, (bi + 1) * chunk)) for bi in range(bb) for hd in range(DN_HEADS)]
    rs = [_mm(jnp.concatenate([w_ref[0, hd, rows, :], qd_ref[0, hd, rows, :]], axis=0), s0_ref[bi, hd])
          for bi, hd, rows in todo]
    v_news = [u_ref[0, hd, rows, :] - r[0:chunk] for (bi, hd, rows), r in zip(todo, rs)]
    for (bi, hd, rows), r, v_new in zip(todo, rs, v_news):
        o_ref[0, hd, rows, :] = r[chunk:] + _mm(at_ref[0, hd, rows, 0:chunk], v_new)
    for (bi, hd, rows), v_new in zip(todo, v_news):
        last = (bi + 1) * chunk - 1
        g_last = jnp.exp(gc_ref[0, last:last + 1, DN_HEADS + hd:DN_HEADS + hd + 1])
        sf_ref[bi, hd] = s0_ref[bi, hd] * g_last + _mm_tn(kd_ref[0, hd, rows, :], v_new)


def _delta_step_call(u, w, qd, kd, at, gc, s0, chunk):
    nseq = s0.shape[0]
    bb = 4
    hm_spec = pl.BlockSpec((1, DN_HEADS, bb * chunk, DN_DV), lambda i: (0, 0, i, 0))
    st_spec = pl.BlockSpec((bb, DN_HEADS, DN_DK, DN_DV), lambda i: (i, 0, 0, 0))
    return pl.pallas_call(
        functools.partial(_delta_step_kernel, bb=bb, chunk=chunk),
        grid=(nseq // bb,),
        in_specs=[hm_spec] * 5 + [pl.BlockSpec((1, bb * chunk, 128), lambda i: (0, i, 0)), st_spec],
        out_specs=[hm_spec, st_spec],
        out_shape=[jax.ShapeDtypeStruct(u.shape, F32),
                   jax.ShapeDtypeStruct((nseq, DN_HEADS, DN_DK, DN_DV), F32)],
        compiler_params=_params("parallel"),
        name="delta_step",
    )(u, w, qd, kd, at, gc, s0)


def _s5_epilogue(y, u, sg, d_ref, wglu_ref, bglu_ref):
    z = _gelu_tanh(y + d_ref[...] * u)
    gate = jax.nn.sigmoid(_mm(z, wglu_ref[...]) + bglu_ref[...])
    return z * gate * _silu(sg)


def _s5_kernel(u_ref, sg_ref, h0r_ref, h0i_ref, bblk_ref, ar_ref, ai_ref, apr_ref, api_ref, pw_ref,
               cblk_ref, d_ref, wglu_ref, bglu_ref, o_ref, hr_o, hi_o, hs_ref, cr_ref, ci_ref):
    i = pl.program_id(1)
    ns = S5_SLABS

    @pl.when(i == 0)
    def _():
        cr_ref[...] = h0r_ref[0]
        ci_ref[...] = h0i_ref[0]

    def slab(c):
        return slice(c * 128, (c + 1) * 128)

    u = u_ref[0]
    bu = _mm(u, bblk_ref[...])
    for c in range(2 * ns):
        hs_ref[c] = bu[:, slab(c)]
    ar = [jnp.broadcast_to(ar_ref[:, slab(c)], (8, 128)) for c in range(ns)]
    ai = [jnp.broadcast_to(ai_ref[:, slab(c)], (8, 128)) for c in range(ns)]

    def scan_body(j, carry):
        rows = pl.ds(pl.multiple_of(j * 8, 8), 8)
        new = []
        for c in range(ns):
            hr, hi = carry[2 * c], carry[2 * c + 1]
            nr = ar[c] * hr - ai[c] * hi + hs_ref[c, rows, :]
            ni = ar[c] * hi + ai[c] * hr + hs_ref[ns + c, rows, :]
            hs_ref[c, rows, :] = nr
            hs_ref[ns + c, rows, :] = ni
            new += [nr, ni]
        return tuple(new)

    zero = jnp.zeros((8, 128), F32)
    ends = lax.fori_loop(0, S5_SUB, scan_body, (zero,) * (2 * ns))

    h_in = []
    for c in range(ns):
        apr = apr_ref[:, slab(c)]
        api = api_ref[:, slab(c)]
        hr = cr_ref[:, slab(c)]
        hi = ci_ref[:, slab(c)]
        er, ei = ends[2 * c], ends[2 * c + 1]
        rows_r, rows_i = [], []
        for s in range(8):
            rows_r.append(hr)
            rows_i.append(hi)
            nr = apr * hr - api * hi + er[s:s + 1]
            ni = apr * hi + api * hr + ei[s:s + 1]
            hr, hi = nr, ni
        cr_ref[:, slab(c)] = hr
        ci_ref[:, slab(c)] = hi
        h_in += [jnp.concatenate(rows_r, axis=0), jnp.concatenate(rows_i, axis=0)]

    def fix_body(j, carry):
        rows = pl.ds(pl.multiple_of(j * 8, 8), 8)
        for c in range(ns):
            pr = pw_ref[rows, slab(c)]
            pi = pw_ref[rows, slab(ns + c)]
            hr, hi = h_in[2 * c], h_in[2 * c + 1]
            hs_ref[c, rows, :] = hs_ref[c, rows, :] + pr * hr - pi * hi
            hs_ref[ns + c, rows, :] = hs_ref[ns + c, rows, :] + pr * hi + pi * hr
        return carry

    lax.fori_loop(0, S5_SUB, fix_body, 0)

    y = jnp.zeros((S5_ROWS, SSM_W), F32)
    for c in range(0, 2 * ns, 2):
        y = y + _mm(jnp.concatenate([hs_ref[c], hs_ref[c + 1]], axis=1), cblk_ref[c * 128:(c + 2) * 128, :])

    o_ref[0] = _s5_epilogue(y, u, sg_ref[0], d_ref, wglu_ref, bglu_ref)

    @pl.when(i == pl.num_programs(1) - 1)
    def _():
        hr_o[0] = cr_ref[...]
        hi_o[0] = ci_ref[...]


def _s5_call(u, sg, h0r, h0i, sp):
    b, t, _ = u.shape
    n = SSM_N
    row_spec = pl.BlockSpec((1, S5_ROWS, SSM_W), lambda bi, i: (bi, i, 0))
    st_spec = pl.BlockSpec((1, 1, n), lambda bi, i: (bi, 0, 0))
    full = lambda shape: pl.BlockSpec(shape, lambda bi, i: (0,) * len(shape))
    return pl.pallas_call(
        _s5_kernel,
        grid=(b, t // S5_ROWS),
        in_specs=[row_spec, row_spec, st_spec, st_spec,
                  full((SSM_W, 2 * n)), full((1, n)), full((1, n)), full((1, n)), full((1, n)),
                  full((S5_ROWS, 2 * n)), full((2 * n, SSM_W)), full((1, SSM_W)),
                  full((SSM_W, SSM_W)), full((1, SSM_W))],
        out_specs=[row_spec, st_spec, st_spec],
        out_shape=[jax.ShapeDtypeStruct((b, t, SSM_W), F32),
                   jax.ShapeDtypeStruct((b, 1, n), F32), jax.ShapeDtypeStruct((b, 1, n), F32)],
        scratch_shapes=[pltpu.VMEM((2 * S5_SLABS, S5_ROWS, 128), F32), pltpu.VMEM((1, n), F32),
                        pltpu.VMEM((1, n), F32)],
        compiler_params=_params("parallel", "arbitrary"),
        name="s5",
    )(u, sg, h0r, h0i, sp["bblk"], sp["ar"], sp["ai"], sp["apr"], sp["api"], sp["pw"],
      sp["cblk"], sp["d"], sp["wglu"], sp["bglu"])


def _s5_step_kernel(u_ref, sg_ref, h0r_ref, h0i_ref, bblk_ref, ar_ref, ai_ref, cblk_ref, d_ref, wglu_ref,
                    bglu_ref, o_ref, hr_o, hi_o, *, nseq, t):
    hr = h0r_ref[...]
    hi = h0i_ref[...]
    ar = ar_ref[...]
    ai = ai_ref[...]
    n = SSM_N
    for step in range(t):
        u = u_ref[step]
        sg = sg_ref[step]
        bu = _mm(u, bblk_ref[...])
        nr = ar * hr - ai * hi + bu[:, 0:n]
        ni = ar * hi + ai * hr + bu[:, n:]
        hr, hi = nr, ni
        y = _mm(jnp.concatenate([hr, hi], axis=1), cblk_ref[...])
        o_ref[step] = _s5_epilogue(y, u, sg, d_ref, wglu_ref, bglu_ref)
    hr_o[...] = hr
    hi_o[...] = hi


def _s5_step_call(u, sg, h0r, h0i, sp, nseq, t):
    n = SSM_N
    return pl.pallas_call(
        functools.partial(_s5_step_kernel, nseq=nseq, t=t),
        out_shape=[jax.ShapeDtypeStruct((t, nseq, SSM_W), F32),
                   jax.ShapeDtypeStruct((nseq, n), F32), jax.ShapeDtypeStruct((nseq, n), F32)],
        compiler_params=pltpu.CompilerParams(vmem_limit_bytes=VMEM_LIMIT),
        name="s5_step",
    )(u, sg, h0r, h0i, sp["bblk"], sp["ar"], sp["ai"], sp["cblk"], sp["d"], sp["wglu"], sp["bglu"])


def _attn_prompt_kernel(qi_tab, ki_tab, lam_ref, q_ref, k_ref, v_ref, o_ref, qcat_ref, m_ref, acc_ref):
    p = pl.program_id(2)
    qi = qi_tab[p]
    ki = ki_tab[p]
    tq = q_ref.shape[3]

    sub = TK_SUB
    nsub = tq // sub

    @pl.when(ki == 0)
    def _():
        q = q_ref[0, 0]
        feat = lax.broadcasted_iota(jnp.int32, (DA_VD, sub), 0)
        zero = jnp.zeros((DA_VD, sub), q.dtype)
        for blk in range(nsub):
            qb = q[:, blk * sub:(blk + 1) * sub]
            qcat_ref[:, 2 * blk * sub:(2 * blk + 1) * sub] = jnp.where(feat < DA_HD, qb, zero)
            qcat_ref[:, (2 * blk + 1) * sub:(2 * blk + 2) * sub] = jnp.where(feat >= DA_HD, qb, zero)
        m_ref[...] = jnp.full(m_ref.shape, NEG_BIG, F32)
        acc_ref[...] = jnp.zeros(acc_ref.shape, F32)

    def step(diagonal):
        def col_range(c):
            return slice(2 * c * sub, 2 * tq) if diagonal else slice(0, 2 * tq)

        def scores(c):
            return _mm_tn(k_ref[0, 0, :, c * sub:(c + 1) * sub], qcat_ref[:, col_range(c)])

        s_next = scores(0)
        for c in range(nsub):
            keys = slice(c * sub, (c + 1) * sub)
            cols = col_range(c)
            s = s_next
            if c + 1 < nsub:
                s_next = scores(c + 1)
            if diagonal:
                kr = lax.broadcasted_iota(jnp.int32, (sub, 2 * sub), 0)
                qc = lax.broadcasted_iota(jnp.int32, (sub, 2 * sub), 1) & (sub - 1)
                own = jnp.where(kr <= qc, s[:, 0:2 * sub], NEG_BIG)
                s = own if c == nsub - 1 else jnp.concatenate([own, s[:, 2 * sub:]], axis=1)
            m = m_ref[:, cols]
            m_new = jnp.maximum(m, jnp.max(s, axis=0, keepdims=True))
            alpha = jnp.exp2(m - m_new)
            pm = jnp.exp2(s - m_new).astype(BF16)
            v = v_ref[0, 0, :, keys].astype(BF16)
            v_ext = jnp.concatenate([v, jnp.ones((ATT_PAD, sub), BF16)], axis=0)
            acc_ref[:, cols] = alpha * acc_ref[:, cols] + jnp.dot(v_ext, pm, preferred_element_type=F32)
            m_ref[:, cols] = m_new

    @pl.when(ki < qi)
    def _():
        step(False)

    @pl.when(ki == qi)
    def _():
        step(True)
        acc = acc_ref[...]
        outs = []
        for blk in range(nsub):
            a1 = acc[:, 2 * blk * sub:(2 * blk + 1) * sub]
            a2 = acc[:, (2 * blk + 1) * sub:(2 * blk + 2) * sub]
            o1 = a1[0:DA_VD] / a1[DA_VD:DA_VD + 1]
            o2 = a2[0:DA_VD] / a2[DA_VD:DA_VD + 1]
            outs.append(o1 - lam_ref[...] * o2)
        o_ref[0, 0] = jnp.concatenate(outs, axis=1).T


def _attn_prompt_call(qh, k_all, v_all, lam, layer):
    b, h, _, t = qh.shape
    tq = min(TQ, t)
    nq = t // tq
    qi_tab = jnp.asarray([i for i in range(nq) for _ in range(i + 1)], jnp.int32)
    ki_tab = jnp.asarray([j for i in range(nq) for j in range(i + 1)], jnp.int32)
    grid_spec = pltpu.PrefetchScalarGridSpec(
        num_scalar_prefetch=2,
        grid=(b, h, int(qi_tab.shape[0])),
        in_specs=[pl.BlockSpec((1, 1), lambda bi, hi, p, qt, kt: (0, 0)),
                  pl.BlockSpec((1, 1, DA_VD, tq), lambda bi, hi, p, qt, kt: (bi, hi, 0, qt[p])),
                  pl.BlockSpec((None, 1, 1, DA_VD, tq), lambda bi, hi, p, qt, kt: (layer, bi, hi, 0, kt[p])),
                  pl.BlockSpec((None, 1, 1, DA_VD, tq), lambda bi, hi, p, qt, kt: (layer, bi, hi, 0, kt[p]))],
        out_specs=pl.BlockSpec((1, 1, tq, DA_VD), lambda bi, hi, p, qt, kt: (bi, hi, qt[p], 0)),
        scratch_shapes=[pltpu.VMEM((DA_VD, 2 * tq), BF16), pltpu.VMEM((1, 2 * tq), F32),
                        pltpu.VMEM((DA_VD + ATT_PAD, 2 * tq), F32)])
    return pl.pallas_call(
        _attn_prompt_kernel,
        grid_spec=grid_spec,
        out_shape=jax.ShapeDtypeStruct((b, h, t, DA_VD), F32),
        compiler_params=_params("parallel", "parallel", "arbitrary"),
        name="attn_prompt",
    )(qi_tab, ki_tab, lam, qh, k_all, v_all)


def _attn_sample_kernel(pt_ref, lam_ref, q_ref, kn_ref, vn_ref, *rest, t_new, n_pages):
    del pt_ref
    pp = PAGES_PER_STEP
    k_refs = rest[0:pp]
    v_refs = rest[pp:2 * pp]
    o_ref, qrows_ref, m_ref, l_ref, acc_ref = rest[2 * pp:]
    j = pl.program_id(1)
    nrow = 2 * t_new * 8

    def update(s, values, mm):
        m_old = m_ref[...]
        m_new = jnp.maximum(m_old, jnp.max(s, axis=1, keepdims=True))
        alpha = jnp.exp2(m_old - m_new)
        pm = jnp.exp2(s - m_new)
        l_ref[...] = alpha * l_ref[...] + jnp.sum(pm, axis=1, keepdims=True)
        acc = alpha * acc_ref[...]
        width = s.shape[1] // len(values)
        for idx, v in enumerate(values):
            acc = acc + mm(pm[:, idx * width:(idx + 1) * width], v)
        acc_ref[...] = acc
        m_ref[...] = m_new

    @pl.when(j == 0)
    def _():
        q = q_ref[0]
        sub = lax.broadcasted_iota(jnp.int32, (8, DA_W), 0)
        lane = lax.broadcasted_iota(jnp.int32, (8, DA_W), 1)
        for mp in range(2):
            keep = ((lane >> 6) == sub) & (((lane >> 5) & 1) == mp)
            for qi in range(t_new):
                r0 = mp * t_new * 8 + qi * 8
                qb = jnp.broadcast_to(q[qi:qi + 1, :], (8, DA_W))
                qrows_ref[r0:r0 + 8, :] = jnp.where(keep, qb, 0.0).astype(BF16)
        m_ref[...] = jnp.full(m_ref.shape, NEG_BIG, F32)
        l_ref[...] = jnp.zeros(l_ref.shape, F32)
        acc_ref[...] = jnp.zeros(acc_ref.shape, F32)
        pad = jnp.zeros((16 - t_new, DA_W), F32)
        k8 = jnp.concatenate([kn_ref[0], pad], axis=0)
        v8 = jnp.concatenate([vn_ref[0], pad], axis=0)
        s = _mm_nt(qrows_ref[...], k8)
        key = lax.broadcasted_iota(jnp.int32, s.shape, 1)
        qidx = (lax.broadcasted_iota(jnp.int32, s.shape, 0) >> 3) & (t_new - 1)
        s = jnp.where(key <= qidx, s, NEG_BIG)
        update(s, [v8], _mm)

    def pairs(refs):
        return [jnp.concatenate([refs[i][...].astype(BF16), refs[i + 1][...].astype(BF16)], axis=1)
                for i in range(0, pp, 2)]

    s_all = jnp.concatenate([_mm(qrows_ref[...], kp) for kp in pairs(k_refs)], axis=1)
    update(s_all, pairs(v_refs), _mm_nt)

    @pl.when(j == n_pages // pp - 1)
    def _():
        o = acc_ref[...] / l_ref[...]
        half = nrow // 2
        oc = o[0:half] - lam_ref[...] * o[half:]
        hd = lax.broadcasted_iota(jnp.int32, oc.shape, 0) & 7
        lane = lax.broadcasted_iota(jnp.int32, oc.shape, 1)
        oc = jnp.where((lane >> 6) == hd, oc, 0.0)
        o_ref[0] = jnp.sum(oc.reshape(t_new, 8, DA_W), axis=1)


def _attn_sample_call(q, k_new, v_new, cache_k, cache_v, page_table, lam, layer):
    nseq, t_new, _ = q.shape
    n_pages = page_table.shape[1]
    pp = PAGES_PER_STEP
    nrow = 2 * t_new * 8
    pt_flat = page_table.reshape(-1).astype(jnp.int32)

    def page_spec(idx):
        return pl.BlockSpec((None, None, DA_W, PAGE_SIZE),
                            lambda bi, j, pt: (layer, pt[bi * n_pages + j * pp + idx], 0, 0))

    tok_spec = pl.BlockSpec((1, t_new, DA_W), lambda bi, j, pt: (bi, 0, 0))
    grid_spec = pltpu.PrefetchScalarGridSpec(
        num_scalar_prefetch=1,
        grid=(nseq, n_pages // pp),
        in_specs=[pl.BlockSpec((1, 1), lambda bi, j, pt: (0, 0)), tok_spec, tok_spec, tok_spec]
        + [page_spec(i) for i in range(pp)] + [page_spec(i) for i in range(pp)],
        out_specs=tok_spec,
        scratch_shapes=[pltpu.VMEM((nrow, DA_W), BF16), pltpu.VMEM((nrow, 1), F32),
                        pltpu.VMEM((nrow, 1), F32), pltpu.VMEM((nrow, DA_W), F32)])
    return pl.pallas_call(
        functools.partial(_attn_sample_kernel, t_new=t_new, n_pages=n_pages),
        grid_spec=grid_spec,
        out_shape=jax.ShapeDtypeStruct((nseq, t_new, DA_W), F32),
        compiler_params=_params("parallel", "arbitrary"),
        name="attn_sample",
    )(pt_flat, lam, q, k_new, v_new, *([cache_k] * pp), *([cache_v] * pp))


def _mix_kernel(x_ref, gate_ref, odn_ref, dng_ref, ossm_ref, oda_ref, dag_ref, onorm_ref, subln_ref, bd_ref,
                w_ref, fg_ref, y_ref, *, head_major, final):
    odn = odn_ref[0]
    if head_major:
        oda = jnp.concatenate([oda_ref[0, hd] for hd in range(DA_HEADS)], axis=1)
    else:
        oda = oda_ref[0]

    def head_norm(o, gain):
        ms = _mm_split_lhs(o * o, bd_ref[...]) * (1.0 / DN_DV)
        return o * lax.rsqrt(ms + NORM_EPS) * gain

    a = head_norm(odn, onorm_ref[...]) * _silu(dng_ref[0])
    c = head_norm(oda, subln_ref[...]) * _silu(dag_ref[0])
    mixed = (jnp.dot(a.astype(BF16), w_ref[0, 0:DN_W, :], preferred_element_type=F32)
             + jnp.dot(ossm_ref[0].astype(BF16), w_ref[0, DN_W:DN_W + SSM_W, :], preferred_element_type=F32)
             + jnp.dot(c.astype(BF16), w_ref[0, DN_W + SSM_W:, :], preferred_element_type=F32))
    y = x_ref[0] + gate_ref[0] * mixed
    if final:
        ms = jnp.mean(y * y, axis=-1, keepdims=True)
        y = y * lax.rsqrt(ms + NORM_EPS) * fg_ref[...]
    y_ref[0] = y


def _mix_call(x, gate, odn, dng, ossm, oda, dag, onorm_row, subln_row, w_out_bf16, layer, head_major, final_g,
              final):
    b, t, _ = x.shape
    tm = min(TM_PROJ, t)
    per_row = gate.shape[1] != 1
    tmm = tm if per_row else 1
    mod_map = (lambda bi, i: (bi, i, 0)) if per_row else (lambda bi, i: (bi, 0, 0))
    row_map = lambda bi, i: (bi, i, 0)
    rows_spec = pl.BlockSpec((1, tm, DN_W), row_map)
    oda_spec = pl.BlockSpec((1, DA_HEADS, tm, DA_VD), lambda bi, i: (bi, 0, i, 0)) if head_major else rows_spec
    c = jnp.arange(DN_W)
    bd = ((c[:, None] // DN_DV) == (c[None, :] // DN_DV)).astype(BF16)
    return pl.pallas_call(
        functools.partial(_mix_kernel, head_major=head_major, final=final),
        grid=(b, t // tm),
        in_specs=[pl.BlockSpec((1, tm, D_MODEL), row_map),
                  pl.BlockSpec((1, tmm, D_MODEL), mod_map),
                  rows_spec,
                  pl.BlockSpec((1, tm, DN_W), row_map),
                  pl.BlockSpec((1, tm, SSM_W), row_map),
                  oda_spec,
                  pl.BlockSpec((1, tm, DA_W), row_map),
                  pl.BlockSpec((1, DN_W), lambda bi, i: (0, 0)),
                  pl.BlockSpec((1, DA_W), lambda bi, i: (0, 0)),
                  pl.BlockSpec((DN_W, DN_W), lambda bi, i: (0, 0)),
                  pl.BlockSpec((1, MIX_W, D_MODEL), lambda bi, i: (layer, 0, 0)),
                  pl.BlockSpec((1, D_MODEL), lambda bi, i: (0, 0))],
        out_specs=pl.BlockSpec((1, tm, D_MODEL), row_map),
        out_shape=jax.ShapeDtypeStruct((b, t, D_MODEL), F32),
        compiler_params=_params("parallel", "parallel"),
        name="mix",
    )(x, gate, odn, dng, ossm, oda, dag, onorm_row, subln_row, bd, w_out_bf16, final_g)


def _permute_w_in(w_in):
    splits = (DN_CONV_CH, DN_HEADS, DN_HEADS, DN_W, SSM_W, SSM_W, DA_W, DA_W, DA_W, DA_W)
    offs = [0]
    for n in splits:
        offs.append(offs[-1] + n)
    qkv, dnb, dna, dng, su, sg, q, k, v, dag = [w_in[..., offs[i]:offs[i + 1]] for i in range(10)]
    pad = jnp.zeros(w_in.shape[:-1] + (128 - 2 * DN_HEADS,), w_in.dtype)
    return jnp.concatenate([qkv, dng, su, sg, q, k, v, dag, dnb, dna, pad], axis=-1).astype(BF16)


def _rope_tables(pos):
    half = DA_HD // 2
    inv = jnp.power(ROPE_THETA, -jnp.arange(half, dtype=F32) * 2.0 / DA_HD)
    ang = pos.astype(F32)[:, None] * inv[None, :]
    cos = jnp.tile(jnp.cos(ang), (1, 128 // half))
    sin = jnp.tile(jnp.sin(ang), (1, 128 // half))
    sign = jnp.where((jnp.arange(128) & half) == 0, -1.0, 1.0).astype(F32)
    return cos, sin * sign[None, :]


def _s5_params(lam_re, lam_im, log_dt, b_re, b_im, c_re, c_im, d_skip, w_glu, b_glu):
    g, p, cg = SSM_GROUPS, SSM_P, SSM_GROUP_CH
    lam = lax.complex(lam_re.astype(F32), lam_im.astype(F32))
    dt = jnp.exp(log_dt.astype(F32))[:, None]
    lam_bar = jnp.exp(lam * dt)
    b_bar = ((lam_bar - 1.0) / lam)[..., None] * lax.complex(b_re.astype(F32), b_im.astype(F32))
    eye = jnp.eye(g, dtype=F32)
    b_t = jnp.transpose(b_bar, (0, 2, 1))
    bb_re = jnp.einsum("gcp,gh->gchp", jnp.real(b_t), eye).reshape(g * cg, g * p)
    bb_im = jnp.einsum("gcp,gh->gchp", jnp.imag(b_t), eye).reshape(g * cg, g * p)
    bblk = jnp.concatenate([bb_re, bb_im], axis=1).astype(BF16)
    c_t_re = jnp.transpose(c_re.astype(F32), (0, 2, 1))
    c_t_im = jnp.transpose(c_im.astype(F32), (0, 2, 1))
    cc_re = jnp.einsum("gpc,gh->gphc", c_t_re, eye).reshape(g * p, g * cg)
    cc_im = jnp.einsum("gpc,gh->gphc", c_t_im, eye).reshape(g * p, g * cg)
    cblk = jnp.concatenate([cc_re, -cc_im], axis=0).astype(BF16)
    a = lam_bar.reshape(1, g * p)
    steps = jnp.arange(1, S5_SUB + 1, dtype=F32)[:, None]
    pw = jnp.exp((lam * dt).reshape(1, g * p) * steps)
    ap = pw[S5_SUB - 1:S5_SUB]
    return {"bblk": bblk, "cblk": cblk,
            "ar": jnp.real(a), "ai": jnp.imag(a),
            "apr": jnp.real(ap), "api": jnp.imag(ap),
            "pw": jnp.repeat(jnp.concatenate([jnp.real(pw), jnp.imag(pw)], axis=1), 8, axis=0),
            "d": d_skip.astype(F32).reshape(1, SSM_W),
            "wglu": w_glu.astype(BF16), "bglu": b_glu.astype(F32).reshape(1, SSM_W)}


def _s5_block_order(a, outer, inner):
    b, t, w = a.shape
    a = a.reshape(b, t // (outer * inner), outer, inner, w)
    return jnp.transpose(a, (0, 1, 3, 2, 4)).reshape(b, t, w)


def _lane_row(vals, offset):
    return jnp.zeros((1, 128), F32).at[0, offset:offset + vals.shape[0]].set(vals.astype(F32))


def kernel(x_prompt, x_sample, c_prompt, c_sample, cache_k, cache_v, page_table, state_conv, state_delta, state_ssm_re, state_ssm_im, norm_g, w_ada, b_ada, w_in, conv_w, dn_a_log, dn_dt_bias, dn_onorm, ssm_lam_re, ssm_lam_im, ssm_log_dt, ssm_b_re, ssm_b_im, ssm_c_re, ssm_c_im, ssm_d, ssm_w_glu, ssm_b_glu, da_lam_q1, da_lam_k1, da_lam_q2, da_lam_k2, da_subln, w_out, final_g):
    bp, tp, _ = x_prompt.shape
    bs, ts, _ = x_sample.shape
    n_pages = page_table.shape[1]
    past = n_pages * PAGE_SIZE
    n_pool = cache_k.shape[1]
    rs = bs * ts

    n_c = bp + bs
    c_rows = -(-n_c // 8) * 8
    c_all = jnp.concatenate([c_prompt, c_sample, jnp.zeros((c_rows - n_c, D_MODEL), F32)], axis=0)
    mods = _ada_call(c_all, w_ada, b_ada)

    w_perm = _permute_w_in(w_in)
    w_out_bf = w_out.astype(BF16)
    cos_p, sin_p = _rope_tables(jnp.arange(tp, dtype=jnp.int32))
    cos_s, sin_s = _rope_tables(past + (jnp.arange(rs, dtype=jnp.int32) % ts))
    cache_k4 = jnp.transpose(cache_k, (0, 1, 3, 4, 2)).reshape(DEPTH, n_pool, DA_W, PAGE_SIZE)
    cache_v4 = jnp.transpose(cache_v, (0, 1, 3, 4, 2)).reshape(DEPTH, n_pool, DA_W, PAGE_SIZE)
    conv_w8 = jnp.concatenate([conv_w, jnp.zeros((DEPTH, 8 - CONV_K, DN_CONV_CH), F32)], axis=1)
    zero_buf = jnp.zeros((bp, 8, DN_CONV_CH), F32)
    zero_delta = jnp.zeros((bp, DN_HEADS, DN_DK, DN_DV), F32)
    zero_h = jnp.zeros((bp, 1, SSM_N), F32)

    xp = x_prompt
    xs = x_sample.reshape(1, rs, D_MODEL)
    outs = {k: [] for k in ("ks", "vs", "cp", "cs", "dp", "ds", "hrp", "hip", "hrs", "his")}
    kv_all = None
    fg = final_g.reshape(1, D_MODEL)
    for l in range(DEPTH):
        lam_init = 0.8 - 0.6 * math.exp(-0.3 * l)
        lam = (jnp.exp(jnp.sum(da_lam_q1[l].astype(F32) * da_lam_k1[l].astype(F32)))
               - jnp.exp(jnp.sum(da_lam_q2[l].astype(F32) * da_lam_k2[l].astype(F32))) + lam_init).reshape(1, 1)
        g_row = norm_g[l].reshape(1, D_MODEL)
        alog_row = _lane_row(dn_a_log[l], DN_HEADS)
        dtb_row = _lane_row(dn_dt_bias[l], DN_HEADS)
        onorm_row = jnp.tile(dn_onorm[l].astype(F32), DN_HEADS).reshape(1, DN_W)
        subln_row = (jnp.tile(da_subln[l].astype(F32), DA_HEADS) * (1.0 - lam_init)).reshape(1, DA_W)
        sp = _s5_params(ssm_lam_re[l], ssm_lam_im[l], ssm_log_dt[l], ssm_b_re[l], ssm_b_im[l],
                        ssm_c_re[l], ssm_c_im[l], ssm_d[l], ssm_w_glu[l], ssm_b_glu[l])

        mp = mods[l, 0:bp].reshape(bp, 1, 3 * D_MODEL)
        shift, scale, gate = mp[..., 0:D_MODEL], mp[..., D_MODEL:2 * D_MODEL], mp[..., 2 * D_MODEL:]
        qkv, dng, su, sg, dag, sm, qh, k_all, v_all = _inproj_call(
            xp, scale, shift, g_row, cos_p, sin_p, w_perm, l, True, kv_all)
        kv_all = (k_all, v_all)
        odn, s_fin = _delta_call(qkv, zero_buf, conv_w8[l], sm, alog_row, dtb_row,
                                 min(DN_ROWS, tp), DN_CHUNK, (0, DN_CHUNK), zero_delta)
        ossm, hr, hi = _s5_call(_s5_block_order(su, 8, S5_SUB), _s5_block_order(sg, 8, S5_SUB),
                                zero_h, zero_h, sp)
        ossm = _s5_block_order(ossm, S5_SUB, 8)
        oda = _attn_prompt_call(qh, k_all, v_all, lam, l)
        xp = _mix_call(xp, gate, odn, dng, ossm, oda, dag, onorm_row, subln_row, w_out_bf, l, True, fg,
                       l == DEPTH - 1)
        outs["cp"].append(qkv[:, tp - (CONV_K - 1):, :])
        outs["dp"].append(s_fin)
        outs["hrp"].append(hr.reshape(bp, SSM_GROUPS, SSM_P))
        outs["hip"].append(hi.reshape(bp, SSM_GROUPS, SSM_P))

        ms_ = jnp.repeat(mods[l, bp:bp + bs], ts, axis=0).reshape(1, rs, 3 * D_MODEL)
        shift, scale, gate = ms_[..., 0:D_MODEL], ms_[..., D_MODEL:2 * D_MODEL], ms_[..., 2 * D_MODEL:]
        qkv, dng, su, sg, dag, sm, qrow, kf, vf = _inproj_call(
            xs, scale, shift, g_row, cos_s, sin_s, w_perm, l, False)
        qkv_seq = qkv.reshape(bs, ts, DN_CONV_CH)
        nb = CONV_K - 1
        pad_t = DN_CHUNK_S - nb - ts
        qkv_cat = jnp.pad(jnp.concatenate([state_conv[l], qkv_seq], axis=1), ((0, 0), (0, pad_t), (0, 0)))
        sm_cat = jnp.pad(sm.reshape(bs, ts, 128), ((0, 0), (nb, pad_t), (0, 0)))
        u, w, qd, kd, at, gc = _delta_call(
            qkv_cat.reshape(1, bs * DN_CHUNK_S, DN_CONV_CH), zero_buf[0:1], conv_w8[l],
            sm_cat.reshape(1, bs * DN_CHUNK_S, 128), alog_row, dtb_row,
            min(DN_ROWS, bs * DN_CHUNK_S), DN_CHUNK_S, (nb, nb + ts))
        odn, s_fin = _delta_step_call(u, w, qd, kd, at, gc, state_delta[l], DN_CHUNK_S)
        odn = odn.reshape(DN_HEADS, bs, DN_CHUNK_S, DN_DV)[:, :, nb:nb + ts, :]
        odn = jnp.transpose(odn, (1, 2, 0, 3)).reshape(1, rs, DN_W)
        su_t = jnp.transpose(su.reshape(bs, ts, SSM_W), (1, 0, 2))
        sg_t = jnp.transpose(sg.reshape(bs, ts, SSM_W), (1, 0, 2))
        ossm, hr, hi = _s5_step_call(su_t, sg_t, state_ssm_re[l].reshape(bs, SSM_N),
                                     state_ssm_im[l].reshape(bs, SSM_N), sp, bs, ts)
        ossm = jnp.transpose(ossm, (1, 0, 2))
        oda = _attn_sample_call(qrow.reshape(bs, ts, DA_W), kf.reshape(bs, ts, DA_W), vf.reshape(bs, ts, DA_W),
                                cache_k4, cache_v4, page_table, lam, l)
        xs = _mix_call(xs, gate, odn, dng, ossm.reshape(1, rs, SSM_W), oda.reshape(1, rs, DA_W), dag,
                       onorm_row, subln_row, w_out_bf, l, False, fg, l == DEPTH - 1)
        xp_conv = jnp.concatenate([state_conv[l], qkv_seq], axis=1)
        outs["ks"].append(kf.reshape(bs, ts, DA_HEADS, DA_VD))
        outs["vs"].append(vf.reshape(bs, ts, DA_HEADS, DA_VD))
        outs["cs"].append(xp_conv[:, xp_conv.shape[1] - (CONV_K - 1):, :])
        outs["ds"].append(s_fin)
        outs["hrs"].append(hr.reshape(bs, SSM_GROUPS, SSM_P))
        outs["his"].append(hi.reshape(bs, SSM_GROUPS, SSM_P))

    y_prompt = xp
    y_sample = xs.reshape(bs, ts, D_MODEL)
    st = {k: jnp.stack(v) for k, v in outs.items()}
    k_prompt = jnp.transpose(kv_all[0], (0, 1, 4, 2, 3))
    v_prompt = jnp.transpose(kv_all[1], (0, 1, 4, 2, 3))
    return (y_prompt, y_sample, k_prompt, v_prompt, st["ks"], st["vs"], st["cp"], st["cs"],
            st["dp"], st["ds"], st["hrp"], st["hip"], st["hrs"], st["his"])
```

```python
import functools
import math

import jax
import jax.numpy as jnp
from jax import lax
from jax.experimental import pallas as pl
from jax.experimental.pallas import tpu as pltpu

F32 = jnp.float32
BF16 = jnp.bfloat16

D_MODEL = 1024
DEPTH = 4
PAGE_SIZE = 128
DN_HEADS = 6
DN_DK = 64
DN_DV = 64
DN_W = DN_HEADS * DN_DV
DN_CONV_CH = 2 * DN_HEADS * DN_DK + DN_W
CONV_K = 4
DN_CHUNK = 64
SSM_GROUPS = 16
SSM_GROUP_CH = 16
SSM_W = SSM_GROUPS * SSM_GROUP_CH
SSM_P = 64
SSM_N = SSM_GROUPS * SSM_P
DA_HEADS = 6
DA_HD = 32
DA_VD = 2 * DA_HD
DA_W = DA_HEADS * DA_VD
MIX_W = DN_W + SSM_W + DA_W
ROPE_THETA = 10000.0
NORM_EPS = 1e-6
NEG_BIG = -1e30

SEG_QKV = (0, 1152)
SEG_DNG = (1152, 1536)
SEG_SU = (1536, 1792)
SEG_SG = (1792, 2048)
SEG_Q = (2048, 2432)
SEG_K = (2432, 2816)
SEG_V = (2816, 3200)
SEG_DAG = (3200, 3584)
SEG_SM = (3584, 3712)
IN_W_PAD = 3712

TM_PROJ = 512
DN_ROWS = 512
DN_GROUP = 128
DN_CHUNK_S = 16
S5_ROWS = 512
S5_SUB = 64
S5_SLABS = SSM_N // 128
TQ = 2048
TK_SUB = 512
PAGES_PER_STEP = 32
ATT_PAD = 16
VMEM_LIMIT = 56 * 1024 * 1024

Q_SCALE = (DA_HD ** -0.5) * math.log2(math.e)


def _mm(a, b):
    return jnp.dot(a.astype(BF16), b.astype(BF16), preferred_element_type=F32)


def _mm_nt(a, b):
    return lax.dot_general(a.astype(BF16), b.astype(BF16), (((1,), (1,)), ((), ())),
                           preferred_element_type=F32)


def _mm_tn(a, b):
    return lax.dot_general(a.astype(BF16), b.astype(BF16), (((0,), (0,)), ((), ())),
                           preferred_element_type=F32)


def _split(a):
    hi = a.astype(BF16)
    lo = (a - hi.astype(F32)).astype(BF16)
    return hi, lo


def _mm_split_lhs(a, b_bf16):
    hi, lo = _split(a)
    return (jnp.dot(hi, b_bf16, preferred_element_type=F32)
            + jnp.dot(lo, b_bf16, preferred_element_type=F32))


def _mm3(a, b):
    ah, al = _split(a)
    bh, bl = _split(b)
    return (jnp.dot(ah, bh, preferred_element_type=F32)
            + jnp.dot(ah, bl, preferred_element_type=F32)
            + jnp.dot(al, bh, preferred_element_type=F32))


def _silu(x):
    return x * jax.nn.sigmoid(x)


def _softplus(x):
    return jnp.maximum(x, 0.0) + jnp.log1p(jnp.exp(-jnp.abs(x)))


def _gelu_tanh(x):
    c = math.sqrt(2.0 / math.pi)
    return x * (0.5 * (1.0 + jnp.tanh(c * (x + 0.044715 * (x * x * x)))))


def _params(*sem):
    return pltpu.CompilerParams(dimension_semantics=sem, vmem_limit_bytes=VMEM_LIMIT)


def _ada_kernel(c_ref, w_ref, b_ref, o_ref):
    c = c_ref[...]
    o_ref[0] = _mm3(_silu(c), w_ref[0]) + b_ref[0]


def _ada_call(c_all, w_ada, b_ada):
    rows = c_all.shape[0]
    tn = 1024
    return pl.pallas_call(
        _ada_kernel,
        grid=(DEPTH, 3 * D_MODEL // tn),
        in_specs=[pl.BlockSpec((rows, D_MODEL), lambda l, n: (0, 0)),
                  pl.BlockSpec((1, D_MODEL, tn), lambda l, n: (l, 0, n)),
                  pl.BlockSpec((1, 1, tn), lambda l, n: (l, 0, n))],
        out_specs=pl.BlockSpec((1, rows, tn), lambda l, n: (l, 0, n)),
        out_shape=jax.ShapeDtypeStruct((DEPTH, rows, 3 * D_MODEL), F32),
        compiler_params=_params("parallel", "parallel"),
        name="adaln",
    )(c_all, w_ada, b_ada.reshape(DEPTH, 1, 3 * D_MODEL))


def _rope(x, cos, sin):
    lane = lax.broadcasted_iota(jnp.int32, cos.shape, 1)
    low = (lane & 16) == 0
    outs = []
    for c in range(x.shape[1] // 128):
        xc = x[:, c * 128:(c + 1) * 128]
        sw = jnp.where(low, pltpu.roll(xc, 112, 1), pltpu.roll(xc, 16, 1))
        outs.append(xc * cos + sw * sin)
    return jnp.concatenate(outs, axis=1)


def _inproj_kernel(x_ref, sc_ref, sh_ref, g_ref, cos_ref, sin_ref, w_ref, *rest, head_major, n_alias):
    outs = rest[n_alias:]
    x = x_ref[0]
    ms = jnp.mean(x * x, axis=-1, keepdims=True)
    h = x * lax.rsqrt(ms + NORM_EPS) * g_ref[...] * (1.0 + sc_ref[0]) + sh_ref[0]
    hb = h.astype(BF16)

    def seg(ab):
        return jnp.dot(hb, w_ref[0, :, ab[0]:ab[1]], preferred_element_type=F32)

    qkv_o, dng_o, su_o, sg_o, dag_o, sm_o, q_o, k_o, v_o = outs
    cos = cos_ref[...]
    sin = sin_ref[...]

    def emit(o_ref, val, dt):
        if head_major:
            for hd in range(DA_HEADS):
                o_ref[0, hd] = val[:, hd * DA_VD:(hd + 1) * DA_VD].T.astype(dt)
        else:
            o_ref[0] = val

    emit(q_o, _rope(seg(SEG_Q), cos, sin) * Q_SCALE, BF16)
    qkv_o[0] = seg(SEG_QKV)
    emit(k_o, _rope(seg(SEG_K), cos, sin), F32)
    dng_o[0] = seg(SEG_DNG)
    su_o[0] = seg(SEG_SU)
    emit(v_o, seg(SEG_V), F32)
    sg_o[0] = seg(SEG_SG)
    dag_o[0] = seg(SEG_DAG)
    sm_o[0] = seg(SEG_SM)


def _inproj_call(x, scale, shift, norm_g, cos, sin, w_perm, layer, head_major, kv_all=None):
    b, t, _ = x.shape
    tm = min(TM_PROJ, t)
    per_row = scale.shape[1] != 1
    tmm = tm if per_row else 1
    mod_map = (lambda bi, i: (bi, i, 0)) if per_row else (lambda bi, i: (bi, 0, 0))
    row_map = lambda bi, i: (bi, i, 0)

    def row_spec(w):
        return pl.BlockSpec((1, tm, w), row_map)

    def row_shape(w):
        return jax.ShapeDtypeStruct((b, t, w), F32)

    out_specs = [row_spec(1152), row_spec(384), row_spec(256), row_spec(256), row_spec(384), row_spec(128)]
    out_shape = [row_shape(1152), row_shape(384), row_shape(256), row_shape(256), row_shape(384), row_shape(128)]
    in_specs = [row_spec(D_MODEL),
                pl.BlockSpec((1, tmm, D_MODEL), mod_map),
                pl.BlockSpec((1, tmm, D_MODEL), mod_map),
                pl.BlockSpec((1, D_MODEL), lambda bi, i: (0, 0)),
                pl.BlockSpec((tm, 128), lambda bi, i: (i, 0)),
                pl.BlockSpec((tm, 128), lambda bi, i: (i, 0)),
                pl.BlockSpec((1, D_MODEL, IN_W_PAD), lambda bi, i: (layer, 0, 0))]
    args = [x, scale, shift, norm_g, cos, sin, w_perm]
    aliases = {}
    if head_major:
        out_specs.append(pl.BlockSpec((1, DA_HEADS, DA_VD, tm), lambda bi, i: (bi, 0, 0, i)))
        out_shape.append(jax.ShapeDtypeStruct((b, DA_HEADS, DA_VD, t), BF16))
        for _ in range(2):
            out_specs.append(pl.BlockSpec((None, 1, DA_HEADS, DA_VD, tm), lambda bi, i: (layer, bi, 0, 0, i)))
            out_shape.append(jax.ShapeDtypeStruct((DEPTH, b, DA_HEADS, DA_VD, t), F32))
        if kv_all is not None:
            for arr in kv_all:
                aliases[len(args)] = len(out_shape) - 2 + len(aliases)
                in_specs.append(pl.BlockSpec(memory_space=pl.ANY))
                args.append(arr)
    else:
        out_specs += [row_spec(384)] * 3
        out_shape += [row_shape(384)] * 3
    return pl.pallas_call(
        functools.partial(_inproj_kernel, head_major=head_major, n_alias=len(aliases)),
        grid=(b, t // tm),
        in_specs=in_specs,
        out_specs=out_specs,
        out_shape=out_shape,
        input_output_aliases=aliases,
        compiler_params=_params("parallel", "parallel"),
        name="inproj",
    )(*args)


def _delta_kernel(x_ref, prev_ref, buf_ref, cw_ref, sm_ref, alog_ref, dtb_ref, ltri_ref, last_ref,
                  fold_ref, bd_ref, *rest, rows, chunk, valid, scan):
    if scan:
        s0_ref, o_ref, sf_ref, xs_ref, s_ref = rest
    else:
        u_o, w_o, qd_o, kd_o, at_o, gc_o, xs_ref = rest
    i = pl.program_id(1)

    if scan:
        @pl.when(i == 0)
        def _():
            zero = jnp.zeros((DN_DK, DN_DV), F32)
            for p in range(DN_HEADS // 2):
                s_ref[p] = jnp.concatenate(
                    [jnp.concatenate([s0_ref[0, 2 * p], zero], axis=1),
                     jnp.concatenate([zero, s0_ref[0, 2 * p + 1]], axis=1)], axis=0)
    halo = jnp.where(i == 0, buf_ref[0], prev_ref[0])
    xs_ref[0:8, :] = halo
    xs_ref[8:8 + rows, :] = x_ref[0]
    cw = cw_ref[...]
    y = (xs_ref[pl.ds(5, rows), :] * cw[0:1] + xs_ref[pl.ds(6, rows), :] * cw[1:2]
         + xs_ref[pl.ds(7, rows), :] * cw[2:3] + xs_ref[pl.ds(8, rows), :] * cw[3:4])
    y = _silu(y)
    qk = y[:, 0:2 * DN_W]
    ss = _mm_split_lhs(qk * qk, bd_ref[...])
    qkn = qk * lax.rsqrt(ss + NORM_EPS)
    v_all = y[:, 2 * DN_W:]

    sm = sm_ref[0]
    pos = lax.broadcasted_iota(jnp.int32, sm.shape, 0) & (chunk - 1)
    real = (pos >= valid[0]) & (pos < valid[1])
    beta = jnp.where(real, jax.nn.sigmoid(sm), 0.0)
    g = jnp.where(real, -jnp.exp(alog_ref[...]) * _softplus(sm + dtb_ref[...]), 0.0)
    gc = _mm_split_lhs_rhs(ltri_ref[...], g)
    gc_last = _mm_split_lhs_rhs(last_ref[...], gc)
    if not scan:
        gc_o[0] = gc
    gc_t = gc.T

    grp = min(rows, DN_GROUP)
    n_grp = rows // grp
    per_grp = grp // chunk
    shift = chunk.bit_length() - 1
    ri = lax.broadcasted_iota(jnp.int32, (grp, grp), 0)
    ci = lax.broadcasted_iota(jnp.int32, (grp, grp), 1)
    same_chunk = (ri >> shift) == (ci >> shift)
    causal = same_chunk & (ri >= ci)
    strict_b = jnp.where(same_chunk & (ri > ci), 1.0, 0.0).astype(BF16)
    eye_f = jnp.where(ri == ci, 1.0, 0.0)
    level_b = []
    s = 1
    while s < chunk:
        sh = s.bit_length()
        m = ((ri >> sh) == (ci >> sh)) & ((ri & s) != 0) & ((ci & s) == 0)
        level_b.append(jnp.where(m, 1.0, 0.0).astype(BF16))
        s *= 2
    scale = DN_DK ** -0.5
    n_pair = DN_HEADS // 2
    first_half = lax.broadcasted_iota(jnp.int32, (grp, 128), 1) < DN_DK

    probs = []
    for g in range(n_grp):
        gr = slice(g * grp, (g + 1) * grp)
        for p in range(n_pair):
            q_pair = qkn[gr, p * 128:(p + 1) * 128] * scale
            k_pair = qkn[gr, DN_W + p * 128:DN_W + (p + 1) * 128]
            v_pair = v_all[gr, p * 128:(p + 1) * 128]
            k_pair_b = k_pair.astype(BF16)
            for hh in range(2):
                hd = 2 * p + hh
                own = first_half if hh == 0 else jnp.logical_not(first_half)
                qh = jnp.where(own, q_pair, 0.0)
                kh = jnp.where(own, k_pair, 0.0)
                vh = jnp.where(own, v_pair, 0.0)
                beta_c = jnp.broadcast_to(beta[gr, hd:hd + 1], (grp, 128))
                gcol = gc[gr, DN_HEADS + hd:DN_HEADS + hd + 1]
                grow = gc_t[DN_HEADS + hd:DN_HEADS + hd + 1, gr]
                decay = jnp.exp(jnp.where(causal, gcol - grow, NEG_BIG))
                gcol = jnp.broadcast_to(gcol, (grp, 128))
                kb = kh * beta_c
                m_b = (_mm_nt(kb, k_pair_b) * decay).astype(BF16) * strict_b
                probs.append((g, p, hh, qh, kh, vh, k_pair_b, beta_c, gcol, decay, kb, m_b))
    x_invs = [eye_f - (pr[11] * level_b[0]).astype(F32) for pr in probs]
    for lb in level_b[1:]:
        for n, pr in enumerate(probs):
            x_b = x_invs[n].astype(BF16)
            t_b = jnp.dot(x_b, pr[11] * lb, preferred_element_type=F32).astype(BF16)
            x_invs[n] = x_invs[n] - jnp.dot(t_b, x_b, preferred_element_type=F32)

    pairs = {}
    for n, (g, p, hh, qh, kh, vh, k_pair_b, beta_c, gcol, decay, kb, _) in enumerate(probs):
        gr = slice(g * grp, (g + 1) * grp)
        hd = 2 * p + hh
        e_g = jnp.exp(gcol)
        sol = _mm(x_invs[n], jnp.concatenate([vh * beta_c, kb * e_g], axis=1))
        attn_full = _mm_nt(qh, k_pair_b) * decay
        u = sol[:, 0:128]
        w = sol[:, 128:256]
        qd = qh * e_g
        gl_col = jnp.broadcast_to(gc_last[gr, DN_HEADS + hd:DN_HEADS + hd + 1], (grp, 128))
        kd = kh * jnp.exp(gl_col - gcol)
        if scan:
            ent = pairs.setdefault((g, p), {"sum": None, "attn": [], "e_last": []})
            ent["sum"] = (u, w, qd, kd) if hh == 0 else tuple(a + b for a, b in zip(ent["sum"], (u, w, qd, kd)))
            ent["attn"].append(attn_full.astype(BF16))
            ent["e_last"].append(jnp.exp(gl_col))
        else:
            hs = slice(hh * DN_DK, (hh + 1) * DN_DK)
            u_o[0, hd, gr, :] = u[:, hs]
            w_o[0, hd, gr, :] = w[:, hs]
            qd_o[0, hd, gr, :] = qd[:, hs]
            kd_o[0, hd, gr, :] = kd[:, hs]
            at_o[0, hd, gr, :] = _mm(attn_full, fold_ref[0:grp, :])

    if scan:
        r2 = lax.broadcasted_iota(jnp.int32, (128, 128), 0) < DN_DK
        c2 = lax.broadcasted_iota(jnp.int32, (128, 128), 1) < DN_DK
        on_diag = r2 == c2
        lane_first = lax.broadcasted_iota(jnp.int32, (1, 128), 1) < DN_DK
        states = [s_ref[p] for p in range(n_pair)]
        for g in range(n_grp):
            v_news = [[] for _ in range(n_pair)]
            o_states = [[] for _ in range(n_pair)]
            for c in range(per_grp):
                rc = slice(c * chunk, (c + 1) * chunk)
                for p in range(n_pair):
                    ent = pairs[(g, p)]
                    u, w, qd, kd = ent["sum"]
                    r = _mm(jnp.concatenate([w[rc], qd[rc]], axis=0), states[p])
                    v_new = u[rc] - r[0:chunk]
                    o_states[p].append(r[chunk:])
                    e_last = jnp.where(lane_first, ent["e_last"][0][c * chunk:c * chunk + 1],
                                       ent["e_last"][1][c * chunk:c * chunk + 1])
                    states[p] = jnp.where(on_diag, states[p] * e_last + _mm_tn(kd[rc], v_new), 0.0)
                    v_news[p].append(v_new)
            for p in range(n_pair):
                vn_b = jnp.concatenate(v_news[p], axis=0).astype(BF16)
                oa = jnp.dot(pairs[(g, p)]["attn"][0], vn_b, preferred_element_type=F32)
                ob = jnp.dot(pairs[(g, p)]["attn"][1], vn_b, preferred_element_type=F32)
                o_ref[0, g * grp:(g + 1) * grp, p * 128:(p + 1) * 128] = (
                    jnp.concatenate(o_states[p], axis=0) + jnp.where(first_half, oa, ob))
        for p in range(n_pair):
            s_ref[p] = states[p]

        @pl.when(i == pl.num_programs(1) - 1)
        def _():
            for p in range(DN_HEADS // 2):
                sf_ref[0, 2 * p] = s_ref[p, 0:DN_DK, 0:DN_DV]
                sf_ref[0, 2 * p + 1] = s_ref[p, DN_DK:, DN_DV:]


def _mm_split_lhs_rhs(a_bf16, b):
    hi, lo = _split(b)
    return (jnp.dot(a_bf16, hi, preferred_element_type=F32)
            + jnp.dot(a_bf16, lo, preferred_element_type=F32))


def _delta_call(qkv, buf8, conv_w8, sm, alog_row, dtb_row, rows, chunk, valid, s0=None):
    b, t, _ = qkv.shape
    nblk = t // rows
    scan = s0 is not None
    r = jnp.arange(rows)
    same = (r[:, None] // chunk) == (r[None, :] // chunk)
    ltri = ((r[:, None] >= r[None, :]) & same).astype(BF16)
    last = (r[None, :] == (r[:, None] | (chunk - 1))).astype(BF16)
    fold = ((r[:, None] & (chunk - 1)) == jnp.arange(DN_CHUNK)[None, :]).astype(BF16)
    c = jnp.arange(2 * DN_W)
    bd = ((c[:, None] // DN_DK) == (c[None, :] // DN_DK)).astype(BF16)
    hm_spec = pl.BlockSpec((1, DN_HEADS, rows, DN_DV), lambda bi, i: (bi, 0, i, 0))
    hm_shape = jax.ShapeDtypeStruct((b, DN_HEADS, t, DN_DV), F32)
    st_spec = pl.BlockSpec((1, DN_HEADS, DN_DK, DN_DV), lambda bi, i: (bi, 0, 0, 0))
    in_specs = [pl.BlockSpec((1, rows, DN_CONV_CH), lambda bi, i: (bi, i, 0)),
                pl.BlockSpec((1, 8, DN_CONV_CH), lambda bi, i: (bi, jnp.maximum(i * (rows // 8) - 1, 0), 0)),
                pl.BlockSpec((1, 8, DN_CONV_CH), lambda bi, i: (bi, 0, 0)),
                pl.BlockSpec((8, DN_CONV_CH), lambda bi, i: (0, 0)),
                pl.BlockSpec((1, rows, 128), lambda bi, i: (bi, i, 0)),
                pl.BlockSpec((1, 128), lambda bi, i: (0, 0)),
                pl.BlockSpec((1, 128), lambda bi, i: (0, 0)),
                pl.BlockSpec((rows, rows), lambda bi, i: (0, 0)),
                pl.BlockSpec((rows, rows), lambda bi, i: (0, 0)),
                pl.BlockSpec((rows, DN_CHUNK), lambda bi, i: (0, 0)),
                pl.BlockSpec((2 * DN_W, 2 * DN_W), lambda bi, i: (0, 0))]
    args = [qkv, qkv, buf8, conv_w8, sm, alog_row, dtb_row, ltri, last, fold, bd]
    scratch = [pltpu.VMEM((rows + 8, DN_CONV_CH), F32)]
    if scan:
        in_specs.append(st_spec)
        args.append(s0)
        out_specs = [pl.BlockSpec((1, rows, DN_W), lambda bi, i: (bi, i, 0)), st_spec]
        out_shape = [jax.ShapeDtypeStruct((b, t, DN_W), F32),
                     jax.ShapeDtypeStruct((b, DN_HEADS, DN_DK, DN_DV), F32)]
        scratch.append(pltpu.VMEM((DN_HEADS // 2, 2 * DN_DK, 2 * DN_DV), F32))
    else:
        out_specs = [hm_spec] * 5 + [pl.BlockSpec((1, rows, 128), lambda bi, i: (bi, i, 0))]
        out_shape = [hm_shape] * 5 + [jax.ShapeDtypeStruct((b, t, 128), F32)]
    return pl.pallas_call(
        functools.partial(_delta_kernel, rows=rows, chunk=chunk, valid=valid, scan=scan),
        grid=(b, nblk),
        in_specs=in_specs,
        out_specs=out_specs,
        out_shape=out_shape,
        scratch_shapes=scratch,
        compiler_params=_params("parallel", "arbitrary" if scan else "parallel"),
        name="delta_scan" if scan else "delta_local",
    )(*args)


def _delta_step_kernel(u_ref, w_ref, qd_ref, kd_ref, at_ref, gc_ref, s0_ref, o_ref, sf_ref, *, bb, chunk):
    todo = [(bi, hd, slice(bi * chunk, (bi + 1) * chunk)) for bi in range(bb) for hd in range(DN_HEADS)]
    rs = [_mm(jnp.concatenate([w_ref[0, hd, rows, :], qd_ref[0, hd, rows, :]], axis=0), s0_ref[bi, hd])
          for bi, hd, rows in todo]
    v_news = [u_ref[0, hd, rows, :] - r[0:chunk] for (bi, hd, rows), r in zip(todo, rs)]
    for (bi, hd, rows), r, v_new in zip(todo, rs, v_news):
        o_ref[0, hd, rows, :] = r[chunk:] + _mm(at_ref[0, hd, rows, 0:chunk], v_new)
    for (bi, hd, rows), v_new in zip(todo, v_news):
        last = (bi + 1) * chunk - 1
        g_last = jnp.exp(gc_ref[0, last:last + 1, DN_HEADS + hd:DN_HEADS + hd + 1])
        sf_ref[bi, hd] = s0_ref[bi, hd] * g_last + _mm_tn(kd_ref[0, hd, rows, :], v_new)


def _delta_step_call(u, w, qd, kd, at, gc, s0, chunk):
    nseq = s0.shape[0]
    bb = 4
    hm_spec = pl.BlockSpec((1, DN_HEADS, bb * chunk, DN_DV), lambda i: (0, 0, i, 0))
    st_spec = pl.BlockSpec((bb, DN_HEADS, DN_DK, DN_DV), lambda i: (i, 0, 0, 0))
    return pl.pallas_call(
        functools.partial(_delta_step_kernel, bb=bb, chunk=chunk),
        grid=(nseq // bb,),
        in_specs=[hm_spec] * 5 + [pl.BlockSpec((1, bb * chunk, 128), lambda i: (0, i, 0)), st_spec],
        out_specs=[hm_spec, st_spec],
        out_shape=[jax.ShapeDtypeStruct(u.shape, F32),
                   jax.ShapeDtypeStruct((nseq, DN_HEADS, DN_DK, DN_DV), F32)],
        compiler_params=_params("parallel"),
        name="delta_step",
    )(u, w, qd, kd, at, gc, s0)


def _s5_epilogue(y, u, sg, d_ref, wglu_ref, bglu_ref):
    z = _gelu_tanh(y + d_ref[...] * u)
    gate = jax.nn.sigmoid(_mm(z, wglu_ref[...]) + bglu_ref[...])
    return z * gate * _silu(sg)


def _s5_kernel(u_ref, sg_ref, h0r_ref, h0i_ref, bblk_ref, ar_ref, ai_ref, apr_ref, api_ref, pw_ref,
               cblk_ref, d_ref, wglu_ref, bglu_ref, o_ref, hr_o, hi_o, hs_ref, cr_ref, ci_ref):
    i = pl.program_id(1)
    ns = S5_SLABS

    @pl.when(i == 0)
    def _():
        cr_ref[...] = h0r_ref[0]
        ci_ref[...] = h0i_ref[0]

    def slab(c):
        return slice(c * 128, (c + 1) * 128)

    u = u_ref[0]
    bu = _mm(u, bblk_ref[...])
    for c in range(2 * ns):
        hs_ref[c] = bu[:, slab(c)]
    ar = [jnp.broadcast_to(ar_ref[:, slab(c)], (8, 128)) for c in range(ns)]
    ai = [jnp.broadcast_to(ai_ref[:, slab(c)], (8, 128)) for c in range(ns)]

    def scan_body(j, carry):
        rows = pl.ds(pl.multiple_of(j * 8, 8), 8)
        new = []
        for c in range(ns):
            hr, hi = carry[2 * c], carry[2 * c + 1]
            nr = ar[c] * hr - ai[c] * hi + hs_ref[c, rows, :]
            ni = ar[c] * hi + ai[c] * hr + hs_ref[ns + c, rows, :]
            hs_ref[c, rows, :] = nr
            hs_ref[ns + c, rows, :] = ni
            new += [nr, ni]
        return tuple(new)

    zero = jnp.zeros((8, 128), F32)
    ends = lax.fori_loop(0, S5_SUB, scan_body, (zero,) * (2 * ns))

    h_in = []
    for c in range(ns):
        apr = apr_ref[:, slab(c)]
        api = api_ref[:, slab(c)]
        hr = cr_ref[:, slab(c)]
        hi = ci_ref[:, slab(c)]
        er, ei = ends[2 * c], ends[2 * c + 1]
        rows_r, rows_i = [], []
        for s in range(8):
            rows_r.append(hr)
            rows_i.append(hi)
            nr = apr * hr - api * hi + er[s:s + 1]
            ni = apr * hi + api * hr + ei[s:s + 1]
            hr, hi = nr, ni
        cr_ref[:, slab(c)] = hr
        ci_ref[:, slab(c)] = hi
        h_in += [jnp.concatenate(rows_r, axis=0), jnp.concatenate(rows_i, axis=0)]

    def fix_body(j, carry):
        rows = pl.ds(pl.multiple_of(j * 8, 8), 8)
        for c in range(ns):
            pr = pw_ref[rows, slab(c)]
            pi = pw_ref[rows, slab(ns + c)]
            hr, hi = h_in[2 * c], h_in[2 * c + 1]
            hs_ref[c, rows, :] = hs_ref[c, rows, :] + pr * hr - pi * hi
            hs_ref[ns + c, rows, :] = hs_ref[ns + c, rows, :] + pr * hi + pi * hr
        return carry

    lax.fori_loop(0, S5_SUB, fix_body, 0)

    y = jnp.zeros((S5_ROWS, SSM_W), F32)
    for c in range(0, 2 * ns, 2):
        y = y + _mm(jnp.concatenate([hs_ref[c], hs_ref[c + 1]], axis=1), cblk_ref[c * 128:(c + 2) * 128, :])

    o_ref[0] = _s5_epilogue(y, u, sg_ref[0], d_ref, wglu_ref, bglu_ref)

    @pl.when(i == pl.num_programs(1) - 1)
    def _():
        hr_o[0] = cr_ref[...]
        hi_o[0] = ci_ref[...]


def _s5_call(u, sg, h0r, h0i, sp):
    b, t, _ = u.shape
    n = SSM_N
    row_spec = pl.BlockSpec((1, S5_ROWS, SSM_W), lambda bi, i: (bi, i, 0))
    st_spec = pl.BlockSpec((1, 1, n), lambda bi, i: (bi, 0, 0))
    full = lambda shape: pl.BlockSpec(shape, lambda bi, i: (0,) * len(shape))
    return pl.pallas_call(
        _s5_kernel,
        grid=(b, t // S5_ROWS),
        in_specs=[row_spec, row_spec, st_spec, st_spec,
                  full((SSM_W, 2 * n)), full((1, n)), full((1, n)), full((1, n)), full((1, n)),
                  full((S5_ROWS, 2 * n)), full((2 * n, SSM_W)), full((1, SSM_W)),
                  full((SSM_W, SSM_W)), full((1, SSM_W))],
        out_specs=[row_spec, st_spec, st_spec],
        out_shape=[jax.ShapeDtypeStruct((b, t, SSM_W), F32),
                   jax.ShapeDtypeStruct((b, 1, n), F32), jax.ShapeDtypeStruct((b, 1, n), F32)],
        scratch_shapes=[pltpu.VMEM((2 * S5_SLABS, S5_ROWS, 128), F32), pltpu.VMEM((1, n), F32),
                        pltpu.VMEM((1, n), F32)],
        compiler_params=_params("parallel", "arbitrary"),
        name="s5",
    )(u, sg, h0r, h0i, sp["bblk"], sp["ar"], sp["ai"], sp["apr"], sp["api"], sp["pw"],
      sp["cblk"], sp["d"], sp["wglu"], sp["bglu"])


def _s5_step_kernel(u_ref, sg_ref, h0r_ref, h0i_ref, bblk_ref, ar_ref, ai_ref, cblk_ref, d_ref, wglu_ref,
                    bglu_ref, o_ref, hr_o, hi_o, *, nseq, t):
    hr = h0r_ref[...]
    hi = h0i_ref[...]
    ar = ar_ref[...]
    ai = ai_ref[...]
    n = SSM_N
    for step in range(t):
        u = u_ref[step]
        sg = sg_ref[step]
        bu = _mm(u, bblk_ref[...])
        nr = ar * hr - ai * hi + bu[:, 0:n]
        ni = ar * hi + ai * hr + bu[:, n:]
        hr, hi = nr, ni
        y = _mm(jnp.concatenate([hr, hi], axis=1), cblk_ref[...])
        o_ref[step] = _s5_epilogue(y, u, sg, d_ref, wglu_ref, bglu_ref)
    hr_o[...] = hr
    hi_o[...] = hi


def _s5_step_call(u, sg, h0r, h0i, sp, nseq, t):
    n = SSM_N
    return pl.pallas_call(
        functools.partial(_s5_step_kernel, nseq=nseq, t=t),
        out_shape=[jax.ShapeDtypeStruct((t, nseq, SSM_W), F32),
                   jax.ShapeDtypeStruct((nseq, n), F32), jax.ShapeDtypeStruct((nseq, n), F32)],
        compiler_params=pltpu.CompilerParams(vmem_limit_bytes=VMEM_LIMIT),
        name="s5_step",
    )(u, sg, h0r, h0i, sp["bblk"], sp["ar"], sp["ai"], sp["cblk"], sp["d"], sp["wglu"], sp["bglu"])


def _attn_prompt_kernel(qi_tab, ki_tab, lam_ref, q_ref, k_ref, v_ref, o_ref, qcat_ref, m_ref, acc_ref):
    p = pl.program_id(2)
    qi = qi_tab[p]
    ki = ki_tab[p]
    tq = q_ref.shape[3]

    sub = TK_SUB
    nsub = tq // sub

    @pl.when(ki == 0)
    def _():
        q = q_ref[0, 0]
        feat = lax.broadcasted_iota(jnp.int32, (DA_VD, sub), 0)
        zero = jnp.zeros((DA_VD, sub), q.dtype)
        for blk in range(nsub):
            qb = q[:, blk * sub:(blk + 1) * sub]
            qcat_ref[:, 2 * blk * sub:(2 * blk + 1) * sub] = jnp.where(feat < DA_HD, qb, zero)
            qcat_ref[:, (2 * blk + 1) * sub:(2 * blk + 2) * sub] = jnp.where(feat >= DA_HD, qb, zero)
        m_ref[...] = jnp.full(m_ref.shape, NEG_BIG, F32)
        acc_ref[...] = jnp.zeros(acc_ref.shape, F32)

    def step(diagonal):
        def col_range(c):
            return slice(2 * c * sub, 2 * tq) if diagonal else slice(0, 2 * tq)

        def scores(c):
            return _mm_tn(k_ref[0, 0, :, c * sub:(c + 1) * sub], qcat_ref[:, col_range(c)])

        s_next = scores(0)
        for c in range(nsub):
            keys = slice(c * sub, (c + 1) * sub)
            cols = col_range(c)
            s = s_next
            if c + 1 < nsub:
                s_next = scores(c + 1)
            if diagonal:
                kr = lax.broadcasted_iota(jnp.int32, (sub, 2 * sub), 0)
                qc = lax.broadcasted_iota(jnp.int32, (sub, 2 * sub), 1) & (sub - 1)
                own = jnp.where(kr <= qc, s[:, 0:2 * sub], NEG_BIG)
                s = own if c == nsub - 1 else jnp.concatenate([own, s[:, 2 * sub:]], axis=1)
            m = m_ref[:, cols]
            m_new = jnp.maximum(m, jnp.max(s, axis=0, keepdims=True))
            alpha = jnp.exp2(m - m_new)
            pm = jnp.exp2(s - m_new).astype(BF16)
            v = v_ref[0, 0, :, keys].astype(BF16)
            v_ext = jnp.concatenate([v, jnp.ones((ATT_PAD, sub), BF16)], axis=0)
            acc_ref[:, cols] = alpha * acc_ref[:, cols] + jnp.dot(v_ext, pm, preferred_element_type=F32)
            m_ref[:, cols] = m_new

    @pl.when(ki < qi)
    def _():
        step(False)

    @pl.when(ki == qi)
    def _():
        step(True)
        acc = acc_ref[...]
        outs = []
        for blk in range(nsub):
            a1 = acc[:, 2 * blk * sub:(2 * blk + 1) * sub]
            a2 = acc[:, (2 * blk + 1) * sub:(2 * blk + 2) * sub]
            o1 = a1[0:DA_VD] / a1[DA_VD:DA_VD + 1]
            o2 = a2[0:DA_VD] / a2[DA_VD:DA_VD + 1]
            outs.append(o1 - lam_ref[...] * o2)
        o_ref[0, 0] = jnp.concatenate(outs, axis=1).T


def _attn_prompt_call(qh, k_all, v_all, lam, layer):
    b, h, _, t = qh.shape
    tq = min(TQ, t)
    nq = t // tq
    qi_tab = jnp.asarray([i for i in range(nq) for _ in range(i + 1)], jnp.int32)
    ki_tab = jnp.asarray([j for i in range(nq) for j in range(i + 1)], jnp.int32)
    grid_spec = pltpu.PrefetchScalarGridSpec(
        num_scalar_prefetch=2,
        grid=(b, h, int(qi_tab.shape[0])),
        in_specs=[pl.BlockSpec((1, 1), lambda bi, hi, p, qt, kt: (0, 0)),
                  pl.BlockSpec((1, 1, DA_VD, tq), lambda bi, hi, p, qt, kt: (bi, hi, 0, qt[p])),
                  pl.BlockSpec((None, 1, 1, DA_VD, tq), lambda bi, hi, p, qt, kt: (layer, bi, hi, 0, kt[p])),
                  pl.BlockSpec((None, 1, 1, DA_VD, tq), lambda bi, hi, p, qt, kt: (layer, bi, hi, 0, kt[p]))],
        out_specs=pl.BlockSpec((1, 1, tq, DA_VD), lambda bi, hi, p, qt, kt: (bi, hi, qt[p], 0)),
        scratch_shapes=[pltpu.VMEM((DA_VD, 2 * tq), BF16), pltpu.VMEM((1, 2 * tq), F32),
                        pltpu.VMEM((DA_VD + ATT_PAD, 2 * tq), F32)])
    return pl.pallas_call(
        _attn_prompt_kernel,
        grid_spec=grid_spec,
        out_shape=jax.ShapeDtypeStruct((b, h, t, DA_VD), F32),
        compiler_params=_params("parallel", "parallel", "arbitrary"),
        name="attn_prompt",
    )(qi_tab, ki_tab, lam, qh, k_all, v_all)


def _attn_sample_kernel(pt_ref, lam_ref, q_ref, kn_ref, vn_ref, *rest, t_new, n_pages):
    del pt_ref
    pp = PAGES_PER_STEP
    k_refs = rest[0:pp]
    v_refs = rest[pp:2 * pp]
    o_ref, qrows_ref, m_ref, l_ref, acc_ref = rest[2 * pp:]
    j = pl.program_id(1)
    nrow = 2 * t_new * 8

    def update(s, values, mm):
        m_old = m_ref[...]
        m_new = jnp.maximum(m_old, jnp.max(s, axis=1, keepdims=True))
        alpha = jnp.exp2(m_old - m_new)
        pm = jnp.exp2(s - m_new)
        l_ref[...] = alpha * l_ref[...] + jnp.sum(pm, axis=1, keepdims=True)
        acc = alpha * acc_ref[...]
        width = s.shape[1] // len(values)
        for idx, v in enumerate(values):
            acc = acc + mm(pm[:, idx * width:(idx + 1) * width], v)
        acc_ref[...] = acc
        m_ref[...] = m_new

    @pl.when(j == 0)
    def _():
        q = q_ref[0]
        sub = lax.broadcasted_iota(jnp.int32, (8, DA_W), 0)
        lane = lax.broadcasted_iota(jnp.int32, (8, DA_W), 1)
        for mp in range(2):
            keep = ((lane >> 6) == sub) & (((lane >> 5) & 1) == mp)
            for qi in range(t_new):
                r0 = mp * t_new * 8 + qi * 8
                qb = jnp.broadcast_to(q[qi:qi + 1, :], (8, DA_W))
                qrows_ref[r0:r0 + 8, :] = jnp.where(keep, qb, 0.0).astype(BF16)
        m_ref[...] = jnp.full(m_ref.shape, NEG_BIG, F32)
        l_ref[...] = jnp.zeros(l_ref.shape, F32)
        acc_ref[...] = jnp.zeros(acc_ref.shape, F32)
        pad = jnp.zeros((16 - t_new, DA_W), F32)
        k8 = jnp.concatenate([kn_ref[0], pad], axis=0)
        v8 = jnp.concatenate([vn_ref[0], pad], axis=0)
        s = _mm_nt(qrows_ref[...], k8)
        key = lax.broadcasted_iota(jnp.int32, s.shape, 1)
        qidx = (lax.broadcasted_iota(jnp.int32, s.shape, 0) >> 3) & (t_new - 1)
        s = jnp.where(key <= qidx, s, NEG_BIG)
        update(s, [v8], _mm)

    def pairs(refs):
        return [jnp.concatenate([refs[i][...].astype(BF16), refs[i + 1][...].astype(BF16)], axis=1)
                for i in range(0, pp, 2)]

    s_all = jnp.concatenate([_mm(qrows_ref[...], kp) for kp in pairs(k_refs)], axis=1)
    update(s_all, pairs(v_refs), _mm_nt)

    @pl.when(j == n_pages // pp - 1)
    def _():
        o = acc_ref[...] / l_ref[...]
        half = nrow // 2
        oc = o[0:half] - lam_ref[...] * o[half:]
        hd = lax.broadcasted_iota(jnp.int32, oc.shape, 0) & 7
        lane = lax.broadcasted_iota(jnp.int32, oc.shape, 1)
        oc = jnp.where((lane >> 6) == hd, oc, 0.0)
        o_ref[0] = jnp.sum(oc.reshape(t_new, 8, DA_W), axis=1)


def _attn_sample_call(q, k_new, v_new, cache_k, cache_v, page_table, lam, layer):
    nseq, t_new, _ = q.shape
    n_pages = page_table.shape[1]
    pp = PAGES_PER_STEP
    nrow = 2 * t_new * 8
    pt_flat = page_table.reshape(-1).astype(jnp.int32)

    def page_spec(idx):
        return pl.BlockSpec((None, None, DA_W, PAGE_SIZE),
                            lambda bi, j, pt: (layer, pt[bi * n_pages + j * pp + idx], 0, 0))

    tok_spec = pl.BlockSpec((1, t_new, DA_W), lambda bi, j, pt: (bi, 0, 0))
    grid_spec = pltpu.PrefetchScalarGridSpec(
        num_scalar_prefetch=1,
        grid=(nseq, n_pages // pp),
        in_specs=[pl.BlockSpec((1, 1), lambda bi, j, pt: (0, 0)), tok_spec, tok_spec, tok_spec]
        + [page_spec(i) for i in range(pp)] + [page_spec(i) for i in range(pp)],
        out_specs=tok_spec,
        scratch_shapes=[pltpu.VMEM((nrow, DA_W), BF16), pltpu.VMEM((nrow, 1), F32),
                        pltpu.VMEM((nrow, 1), F32), pltpu.VMEM((nrow, DA_W), F32)])
    return pl.pallas_call(
        functools.partial(_attn_sample_kernel, t_new=t_new, n_pages=n_pages),
        grid_spec=grid_spec,
        out_shape=jax.ShapeDtypeStruct((nseq, t_new, DA_W), F32),
        compiler_params=_params("parallel", "arbitrary"),
        name="attn_sample",
    )(pt_flat, lam, q, k_new, v_new, *([cache_k] * pp), *([cache_v] * pp))


def _mix_kernel(x_ref, gate_ref, odn_ref, dng_ref, ossm_ref, oda_ref, dag_ref, onorm_ref, subln_ref, bd_ref,
                w_ref, fg_ref, y_ref, *, head_major, final):
    odn = odn_ref[0]
    if head_major:
        oda = jnp.concatenate([oda_ref[0, hd] for hd in range(DA_HEADS)], axis=1)
    else:
        oda = oda_ref[0]

    def head_norm(o, gain):
        ms = _mm_split_lhs(o * o, bd_ref[...]) * (1.0 / DN_DV)
        return o * lax.rsqrt(ms + NORM_EPS) * gain

    a = head_norm(odn, onorm_ref[...]) * _silu(dng_ref[0])
    c = head_norm(oda, subln_ref[...]) * _silu(dag_ref[0])
    mixed = (jnp.dot(a.astype(BF16), w_ref[0, 0:DN_W, :], preferred_element_type=F32)
             + jnp.dot(ossm_ref[0].astype(BF16), w_ref[0, DN_W:DN_W + SSM_W, :], preferred_element_type=F32)
             + jnp.dot(c.astype(BF16), w_ref[0, DN_W + SSM_W:, :], preferred_element_type=F32))
    y = x_ref[0] + gate_ref[0] * mixed
    if final:
        ms = jnp.mean(y * y, axis=-1, keepdims=True)
        y = y * lax.rsqrt(ms + NORM_EPS) * fg_ref[...]
    y_ref[0] = y


def _mix_call(x, gate, odn, dng, ossm, oda, dag, onorm_row, subln_row, w_out_bf16, layer, head_major, final_g,
              final):
    b, t, _ = x.shape
    tm = min(TM_PROJ, t)
    per_row = gate.shape[1] != 1
    tmm = tm if per_row else 1
    mod_map = (lambda bi, i: (bi, i, 0)) if per_row else (lambda bi, i: (bi, 0, 0))
    row_map = lambda bi, i: (bi, i, 0)
    rows_spec = pl.BlockSpec((1, tm, DN_W), row_map)
    oda_spec = pl.BlockSpec((1, DA_HEADS, tm, DA_VD), lambda bi, i: (bi, 0, i, 0)) if head_major else rows_spec
    c = jnp.arange(DN_W)
    bd = ((c[:, None] // DN_DV) == (c[None, :] // DN_DV)).astype(BF16)
    return pl.pallas_call(
        functools.partial(_mix_kernel, head_major=head_major, final=final),
        grid=(b, t // tm),
        in_specs=[pl.BlockSpec((1, tm, D_MODEL), row_map),
                  pl.BlockSpec((1, tmm, D_MODEL), mod_map),
                  rows_spec,
                  pl.BlockSpec((1, tm, DN_W), row_map),
                  pl.BlockSpec((1, tm, SSM_W), row_map),
                  oda_spec,
                  pl.BlockSpec((1, tm, DA_W), row_map),
                  pl.BlockSpec((1, DN_W), lambda bi, i: (0, 0)),
                  pl.BlockSpec((1, DA_W), lambda bi, i: (0, 0)),
                  pl.BlockSpec((DN_W, DN_W), lambda bi, i: (0, 0)),
                  pl.BlockSpec((1, MIX_W, D_MODEL), lambda bi, i: (layer, 0, 0)),
                  pl.BlockSpec((1, D_MODEL), lambda bi, i: (0, 0))],
        out_specs=pl.BlockSpec((1, tm, D_MODEL), row_map),
        out_shape=jax.ShapeDtypeStruct((b, t, D_MODEL), F32),
        compiler_params=_params("parallel", "parallel"),
        name="mix",
    )(x, gate, odn, dng, ossm, oda, dag, onorm_row, subln_row, bd, w_out_bf16, final_g)


def _permute_w_in(w_in):
    splits = (DN_CONV_CH, DN_HEADS, DN_HEADS, DN_W, SSM_W, SSM_W, DA_W, DA_W, DA_W, DA_W)
    offs = [0]
    for n in splits:
        offs.append(offs[-1] + n)
    qkv, dnb, dna, dng, su, sg, q, k, v, dag = [w_in[..., offs[i]:offs[i + 1]] for i in range(10)]
    pad = jnp.zeros(w_in.shape[:-1] + (128 - 2 * DN_HEADS,), w_in.dtype)
    return jnp.concatenate([qkv, dng, su, sg, q, k, v, dag, dnb, dna, pad], axis=-1).astype(BF16)


def _rope_tables(pos):
    half = DA_HD // 2
    inv = jnp.power(ROPE_THETA, -jnp.arange(half, dtype=F32) * 2.0 / DA_HD)
    ang = pos.astype(F32)[:, None] * inv[None, :]
    cos = jnp.tile(jnp.cos(ang), (1, 128 // half))
    sin = jnp.tile(jnp.sin(ang), (1, 128 // half))
    sign = jnp.where((jnp.arange(128) & half) == 0, -1.0, 1.0).astype(F32)
    return cos, sin * sign[None, :]


def _s5_params(lam_re, lam_im, log_dt, b_re, b_im, c_re, c_im, d_skip, w_glu, b_glu):
    g, p, cg = SSM_GROUPS, SSM_P, SSM_GROUP_CH
    lam = lax.complex(lam_re.astype(F32), lam_im.astype(F32))
    dt = jnp.exp(log_dt.astype(F32))[:, None]
    lam_bar = jnp.exp(lam * dt)
    b_bar = ((lam_bar - 1.0) / lam)[..., None] * lax.complex(b_re.astype(F32), b_im.astype(F32))
    eye = jnp.eye(g, dtype=F32)
    b_t = jnp.transpose(b_bar, (0, 2, 1))
    bb_re = jnp.einsum("gcp,gh->gchp", jnp.real(b_t), eye).reshape(g * cg, g * p)
    bb_im = jnp.einsum("gcp,gh->gchp", jnp.imag(b_t), eye).reshape(g * cg, g * p)
    bblk = jnp.concatenate([bb_re, bb_im], axis=1).astype(BF16)
    c_t_re = jnp.transpose(c_re.astype(F32), (0, 2, 1))
    c_t_im = jnp.transpose(c_im.astype(F32), (0, 2, 1))
    cc_re = jnp.einsum("gpc,gh->gphc", c_t_re, eye).reshape(g * p, g * cg)
    cc_im = jnp.einsum("gpc,gh->gphc", c_t_im, eye).reshape(g * p, g * cg)
    cblk = jnp.concatenate([cc_re, -cc_im], axis=0).astype(BF16)
    a = lam_bar.reshape(1, g * p)
    steps = jnp.arange(1, S5_SUB + 1, dtype=F32)[:, None]
    pw = jnp.exp((lam * dt).reshape(1, g * p) * steps)
    ap = pw[S5_SUB - 1:S5_SUB]
    return {"bblk": bblk, "cblk": cblk,
            "ar": jnp.real(a), "ai": jnp.imag(a),
            "apr": jnp.real(ap), "api": jnp.imag(ap),
            "pw": jnp.repeat(jnp.concatenate([jnp.real(pw), jnp.imag(pw)], axis=1), 8, axis=0),
            "d": d_skip.astype(F32).reshape(1, SSM_W),
            "wglu": w_glu.astype(BF16), "bglu": b_glu.astype(F32).reshape(1, SSM_W)}


def _s5_block_order(a, outer, inner):
    b, t, w = a.shape
    a = a.reshape(b, t // (outer * inner), outer, inner, w)
    return jnp.transpose(a, (0, 1, 3, 2, 4)).reshape(b, t, w)


def _lane_row(vals, offset):
    return jnp.zeros((1, 128), F32).at[0, offset:offset + vals.shape[0]].set(vals.astype(F32))


def kernel(x_prompt, x_sample, c_prompt, c_sample, cache_k, cache_v, page_table, state_conv, state_delta, state_ssm_re, state_ssm_im, norm_g, w_ada, b_ada, w_in, conv_w, dn_a_log, dn_dt_bias, dn_onorm, ssm_lam_re, ssm_lam_im, ssm_log_dt, ssm_b_re, ssm_b_im, ssm_c_re, ssm_c_im, ssm_d, ssm_w_glu, ssm_b_glu, da_lam_q1, da_lam_k1, da_lam_q2, da_lam_k2, da_subln, w_out, final_g):
    bp, tp, _ = x_prompt.shape
    bs, ts, _ = x_sample.shape
    n_pages = page_table.shape[1]
    past = n_pages * PAGE_SIZE
    n_pool = cache_k.shape[1]
    rs = bs * ts

    n_c = bp + bs
    c_rows = -(-n_c // 8) * 8
    c_all = jnp.concatenate([c_prompt, c_sample, jnp.zeros((c_rows - n_c, D_MODEL), F32)], axis=0)
    mods = _ada_call(c_all, w_ada, b_ada)

    w_perm = _permute_w_in(w_in)
    w_out_bf = w_out.astype(BF16)
    cos_p, sin_p = _rope_tables(jnp.arange(tp, dtype=jnp.int32))
    cos_s, sin_s = _rope_tables(past + (jnp.arange(rs, dtype=jnp.int32) % ts))
    cache_k4 = jnp.transpose(cache_k, (0, 1, 3, 4, 2)).reshape(DEPTH, n_pool, DA_W, PAGE_SIZE)
    cache_v4 = jnp.transpose(cache_v, (0, 1, 3, 4, 2)).reshape(DEPTH, n_pool, DA_W, PAGE_SIZE)
    conv_w8 = jnp.concatenate([conv_w, jnp.zeros((DEPTH, 8 - CONV_K, DN_CONV_CH), F32)], axis=1)
    zero_buf = jnp.zeros((bp, 8, DN_CONV_CH), F32)
    zero_delta = jnp.zeros((bp, DN_HEADS, DN_DK, DN_DV), F32)
    zero_h = jnp.zeros((bp, 1, SSM_N), F32)

    xp = x_prompt
    xs = x_sample.reshape(1, rs, D_MODEL)
    outs = {k: [] for k in ("ks", "vs", "cp", "cs", "dp", "ds", "hrp", "hip", "hrs", "his")}
    kv_all = None
    fg = final_g.reshape(1, D_MODEL)
    for l in range(DEPTH):
        lam_init = 0.8 - 0.6 * math.exp(-0.3 * l)
        lam = (jnp.exp(jnp.sum(da_lam_q1[l].astype(F32) * da_lam_k1[l].astype(F32)))
               - jnp.exp(jnp.sum(da_lam_q2[l].astype(F32) * da_lam_k2[l].astype(F32))) + lam_init).reshape(1, 1)
        g_row = norm_g[l].reshape(1, D_MODEL)
        alog_row = _lane_row(dn_a_log[l], DN_HEADS)
        dtb_row = _lane_row(dn_dt_bias[l], DN_HEADS)
        onorm_row = jnp.tile(dn_onorm[l].astype(F32), DN_HEADS).reshape(1, DN_W)
        subln_row = (jnp.tile(da_subln[l].astype(F32), DA_HEADS) * (1.0 - lam_init)).reshape(1, DA_W)
        sp = _s5_params(ssm_lam_re[l], ssm_lam_im[l], ssm_log_dt[l], ssm_b_re[l], ssm_b_im[l],
                        ssm_c_re[l], ssm_c_im[l], ssm_d[l], ssm_w_glu[l], ssm_b_glu[l])

        mp = mods[l, 0:bp].reshape(bp, 1, 3 * D_MODEL)
        shift, scale, gate = mp[..., 0:D_MODEL], mp[..., D_MODEL:2 * D_MODEL], mp[..., 2 * D_MODEL:]
        qkv, dng, su, sg, dag, sm, qh, k_all, v_all = _inproj_call(
            xp, scale, shift, g_row, cos_p, sin_p, w_perm, l, True, kv_all)
        kv_all = (k_all, v_all)
        odn, s_fin = _delta_call(qkv, zero_buf, conv_w8[l], sm, alog_row, dtb_row,
                                 min(DN_ROWS, tp), DN_CHUNK, (0, DN_CHUNK), zero_delta)
        ossm, hr, hi = _s5_call(_s5_block_order(su, 8, S5_SUB), _s5_block_order(sg, 8, S5_SUB),
                                zero_h, zero_h, sp)
        ossm = _s5_block_order(ossm, S5_SUB, 8)
        oda = _attn_prompt_call(qh, k_all, v_all, lam, l)
        xp = _mix_call(xp, gate, odn, dng, ossm, oda, dag, onorm_row, subln_row, w_out_bf, l, True, fg,
                       l == DEPTH - 1)
        outs["cp"].append(qkv[:, tp - (CONV_K - 1):, :])
        outs["dp"].append(s_fin)
        outs["hrp"].append(hr.reshape(bp, SSM_GROUPS, SSM_P))
        outs["hip"].append(hi.reshape(bp, SSM_GROUPS, SSM_P))

        ms_ = jnp.repeat(mods[l, bp:bp + bs], ts, axis=0).reshape(1, rs, 3 * D_MODEL)
        shift, scale, gate = ms_[..., 0:D_MODEL], ms_[..., D_MODEL:2 * D_MODEL], ms_[..., 2 * D_MODEL:]
        qkv, dng, su, sg, dag, sm, qrow, kf, vf = _inproj_call(
            xs, scale, shift, g_row, cos_s, sin_s, w_perm, l, False)
        qkv_seq = qkv.reshape(bs, ts, DN_CONV_CH)
        nb = CONV_K - 1
        pad_t = DN_CHUNK_S - nb - ts
        qkv_cat = jnp.pad(jnp.concatenate([state_conv[l], qkv_seq], axis=1), ((0, 0), (0, pad_t), (0, 0)))
        sm_cat = jnp.pad(sm.reshape(bs, ts, 128), ((0, 0), (nb, pad_t), (0, 0)))
        u, w, qd, kd, at, gc = _delta_call(
            qkv_cat.reshape(1, bs * DN_CHUNK_S, DN_CONV_CH), zero_buf[0:1], conv_w8[l],
            sm_cat.reshape(1, bs * DN_CHUNK_S, 128), alog_row, dtb_row,
            min(DN_ROWS, bs * DN_CHUNK_S), DN_CHUNK_S, (nb, nb + ts))
        odn, s_fin = _delta_step_call(u, w, qd, kd, at, gc, state_delta[l], DN_CHUNK_S)
        odn = odn.reshape(DN_HEADS, bs, DN_CHUNK_S, DN_DV)[:, :, nb:nb + ts, :]
        odn = jnp.transpose(odn, (1, 2, 0, 3)).reshape(1, rs, DN_W)
        su_t = jnp.transpose(su.reshape(bs, ts, SSM_W), (1, 0, 2))
        sg_t = jnp.transpose(sg.reshape(bs, ts, SSM_W), (1, 0, 2))
        ossm, hr, hi = _s5_step_call(su_t, sg_t, state_ssm_re[l].reshape(bs, SSM_N),
                                     state_ssm_im[l].reshape(bs, SSM_N), sp, bs, ts)
        ossm = jnp.transpose(ossm, (1, 0, 2))
        oda = _attn_sample_call(qrow.reshape(bs, ts, DA_W), kf.reshape(bs, ts, DA_W), vf.reshape(bs, ts, DA_W),
                                cache_k4, cache_v4, page_table, lam, l)
        xs = _mix_call(xs, gate, odn, dng, ossm.reshape(1, rs, SSM_W), oda.reshape(1, rs, DA_W), dag,
                       onorm_row, subln_row, w_out_bf, l, False, fg, l == DEPTH - 1)
        xp_conv = jnp.concatenate([state_conv[l], qkv_seq], axis=1)
        outs["ks"].append(kf.reshape(bs, ts, DA_HEADS, DA_VD))
        outs["vs"].append(vf.reshape(bs, ts, DA_HEADS, DA_VD))
        outs["cs"].append(xp_conv[:, xp_conv.shape[1] - (CONV_K - 1):, :])
        outs["ds"].append(s_fin)
        outs["hrs"].append(hr.reshape(bs, SSM_GROUPS, SSM_P))
        outs["his"].append(hi.reshape(bs, SSM_GROUPS, SSM_P))

    y_prompt = xp
    y_sample = xs.reshape(bs, ts, D_MODEL)
    st = {k: jnp.stack(v) for k, v in outs.items()}
    k_prompt = jnp.transpose(kv_all[0], (0, 1, 4, 2, 3))
    v_prompt = jnp.transpose(kv_all[1], (0, 1, 4, 2, 3))
    return (y_prompt, y_sample, k_prompt, v_prompt, st["ks"], st["vs"], st["cp"], st["cs"],
            st["dp"], st["ds"], st["hrp"], st["hip"], st["hrs"], st["his"])
```
